```python
import math
import jax, jax.numpy as jnp
from jax import lax
import numpy as np

D_MODEL = 1024
BATCH = 8
SEQ = 4096
DEPTH = 2
DEC_BATCH = 32
DEC_SEQ = 8
PAST_LEN = 16384
PAGE_SIZE = 128

N_A = DEPTH // 2
N_B = DEPTH - N_A
HEAD_DIM = 64
D_CONV = 3 * D_MODEL // 4
CONV_W = 31
H_B = D_CONV // HEAD_DIM
MEM_H = 4
MEM_LEN = 256
D_MEMQ = MEM_H * HEAD_DIM
MOBA_BLOCK = 256
MOBA_TOPK = 3
Q_CHUNK = 128
ROT_DIM = HEAD_DIM // 4
ROPE_THETA = 500000.0
N_GROUPS = 4
E_PER_GROUP = 16
N_EXPERTS = N_GROUPS * E_PER_GROUP
TOP_E = 2
D_EXPERT = D_MODEL // 4
MOE_BLK = 128
EPS = 1e-6

kernel_name = 'yoco_conformer_moba_hmoe_step'


def rmsnorm(x, g):
    xf = x.astype(jnp.float32)
    y = xf * lax.rsqrt(jnp.mean(xf * xf, axis=-1, keepdims=True) + EPS)
    return (y * g.astype(jnp.float32)).astype(x.dtype)


def layernorm(x, g, b):
    xf = x.astype(jnp.float32)
    mu = jnp.mean(xf, axis=-1, keepdims=True)
    xc = xf - mu
    y = xc * lax.rsqrt(jnp.mean(xc * xc, axis=-1, keepdims=True) + EPS)
    return (y * g.astype(jnp.float32) + b.astype(jnp.float32)).astype(x.dtype)


def rotary(x, pos):
    half = ROT_DIM // 2
    inv = ROPE_THETA ** (-jnp.arange(half, dtype=jnp.float32) * (2.0 / ROT_DIM))
    ang = pos.astype(jnp.float32)[:, None] * inv[None, :]
    cos = jnp.cos(ang)[:, None, :]
    sin = jnp.sin(ang)[:, None, :]
    xr = x[..., :ROT_DIM].astype(jnp.float32)
    x1, x2 = xr[..., :half], xr[..., half:]
    rot = jnp.concatenate([x1 * cos - x2 * sin, x2 * cos + x1 * sin], axis=-1).astype(x.dtype)
    return jnp.concatenate([rot, x[..., ROT_DIM:]], axis=-1)


def depthwise_causal_conv(z_ext, w, b):
    c = z_ext.shape[-1]
    taps = w[:, None, :].astype(z_ext.dtype)
    y = lax.conv_general_dilated(z_ext, taps, (1,), 'VALID',
                                 dimension_numbers=('NWC', 'WIO', 'NWC'),
                                 feature_group_count=c)
    return y + b.astype(y.dtype)


def mem_kv(mem, g, w):
    b, m, _ = mem.shape
    kv = (rmsnorm(mem, g) @ w).reshape(b, m, 2, MEM_H, HEAD_DIM)
    return kv[:, :, 0], kv[:, :, 1]


def mem_attend(q, mk, mv):
    s = jnp.einsum('bshd,bmhd->bhsm', q, mk, preferred_element_type=jnp.float32) * (HEAD_DIM ** -0.5)
    p = jax.nn.softmax(s, axis=-1).astype(mv.dtype)
    return jnp.einsum('bhsm,bmhd->bshd', p, mv)


def moba_select(q, kmean, own_blk):
    nq = q.shape[0]
    nb = kmean.shape[0]
    gate = jnp.einsum('qhd,jhd->qhj', q, kmean, preferred_element_type=jnp.float32)
    gate = jnp.where(jnp.arange(nb)[None, None, :] < own_blk, gate, -jnp.inf)
    if nb < MOBA_TOPK:
        gate = jnp.concatenate([gate, jnp.full((nq, H_B, MOBA_TOPK - nb), -jnp.inf, jnp.float32)], axis=-1)
    top_val, top_idx = lax.top_k(gate, MOBA_TOPK)
    valid = top_val > -jnp.inf
    return jnp.where(valid, top_idx, 0), valid


def moba_attend(q, q_pos, k_sel, v_sel, valid, k_own, v_own, own_pos):
    nq = q.shape[0]
    scale = HEAD_DIM ** -0.5
    s_sel = jnp.einsum('qhd,qhjpd->qhjp', q, k_sel, preferred_element_type=jnp.float32) * scale
    s_sel = jnp.where(valid[..., None], s_sel, -jnp.inf).reshape(nq, H_B, MOBA_TOPK * MOBA_BLOCK)
    s_own = jnp.einsum('qhd,phd->qhp', q, k_own, preferred_element_type=jnp.float32) * scale
    s_own = jnp.where(own_pos[None, None, :] <= q_pos[:, None, None], s_own, -jnp.inf)
    p = jax.nn.softmax(jnp.concatenate([s_sel, s_own], axis=-1), axis=-1).astype(v_own.dtype)
    p_sel = p[..., :MOBA_TOPK * MOBA_BLOCK].reshape(nq, H_B, MOBA_TOPK, MOBA_BLOCK)
    p_own = p[..., MOBA_TOPK * MOBA_BLOCK:]
    return (jnp.einsum('qhjp,qhjpd->qhd', p_sel, v_sel)
            + jnp.einsum('qhp,phd->qhd', p_own, v_own))


def moba_prompt(q, k, v):
    s = q.shape[1]
    L = -(-s // MOBA_BLOCK) * MOBA_BLOCK
    nb = L // MOBA_BLOCK
    n_chunks = s // Q_CHUNK
    hidx = jnp.arange(H_B)[None, :, None]

    def per_seq(args):
        qb, kb, vb = args
        kp = jnp.pad(kb, ((0, L - s), (0, 0), (0, 0)))
        vp = jnp.pad(vb, ((0, L - s), (0, 0), (0, 0)))
        kblk = kp.reshape(nb, MOBA_BLOCK, H_B, HEAD_DIM)
        kmean = jnp.mean(kblk, axis=1, dtype=jnp.float32)
        kbh = kblk.transpose(2, 0, 1, 3)
        vbh = vp.reshape(nb, MOBA_BLOCK, H_B, HEAD_DIM).transpose(2, 0, 1, 3)

        def per_chunk(args2):
            c, qc = args2
            start = c * Q_CHUNK
            q_pos = start + jnp.arange(Q_CHUNK)
            own = start // MOBA_BLOCK
            idx, valid = moba_select(qc, kmean, own)
            k_sel = kbh[hidx, idx]
            v_sel = vbh[hidx, idx]
            own_start = own * MOBA_BLOCK
            k_own = lax.dynamic_slice_in_dim(kp, own_start, MOBA_BLOCK, axis=0)
            v_own = lax.dynamic_slice_in_dim(vp, own_start, MOBA_BLOCK, axis=0)
            own_pos = own_start + jnp.arange(MOBA_BLOCK)
            return moba_attend(qc, q_pos, k_sel, v_sel, valid, k_own, v_own, own_pos)

        out = lax.map(per_chunk, (jnp.arange(n_chunks), qb.reshape(n_chunks, Q_CHUNK, H_B, HEAD_DIM)))
        return out.reshape(s, H_B, HEAD_DIM)

    return lax.map(per_seq, (q, k, v))


def moba_sample(q, k_new, v_new, cache_k, cache_v, page_table):
    ds = q.shape[1]
    past = page_table.shape[1] * PAGE_SIZE
    nb = past // MOBA_BLOCK
    tail = past - nb * MOBA_BLOCK
    pad = MOBA_BLOCK - tail - ds
    q_pos = past + jnp.arange(ds)
    own_pos = nb * MOBA_BLOCK + jnp.arange(MOBA_BLOCK)
    hidx = jnp.arange(H_B)[None, :, None, None]

    def per_seq(args):
        qb, knb, vnb, ptb = args
        k_past = cache_k[ptb].reshape(past, H_B, HEAD_DIM)
        kmean = jnp.mean(k_past[:nb * MOBA_BLOCK].reshape(nb, MOBA_BLOCK, H_B, HEAD_DIM), axis=1, dtype=jnp.float32)
        idx, valid = moba_select(qb, kmean, nb)
        rows = jnp.minimum(idx[..., None] * MOBA_BLOCK + jnp.arange(MOBA_BLOCK), past - 1)
        page = ptb[rows // PAGE_SIZE]
        off = rows % PAGE_SIZE
        k_sel = cache_k[page, off, hidx]
        v_sel = cache_v[page, off, hidx]
        trow = nb * MOBA_BLOCK + jnp.arange(tail)
        v_tail = cache_v[ptb[trow // PAGE_SIZE], trow % PAGE_SIZE]
        k_tail = k_past[nb * MOBA_BLOCK:]
        zk = jnp.zeros((pad, H_B, HEAD_DIM), knb.dtype)
        zv = jnp.zeros((pad, H_B, HEAD_DIM), vnb.dtype)
        k_own = jnp.concatenate([k_tail.astype(knb.dtype), knb, zk], axis=0)
        v_own = jnp.concatenate([v_tail.astype(vnb.dtype), vnb, zv], axis=0)
        return moba_attend(qb, q_pos, k_sel, v_sel, valid, k_own, v_own, own_pos)

    return lax.map(per_seq, (q, k_new, v_new, page_table))


def hier_moe(h, w_rg, b_rg, w_re, b_re, w1, w3, w2):
    n = h.shape[0]
    ar = jnp.arange(n)
    g_logits = jnp.dot(h, w_rg, preferred_element_type=jnp.float32) + b_rg.astype(jnp.float32)
    p_group = jax.nn.softmax(g_logits, axis=-1)
    g_sel = jnp.argmax(g_logits, axis=-1)
    e_all = (jnp.dot(h, w_re, preferred_element_type=jnp.float32) + b_re.astype(jnp.float32)).reshape(n, N_GROUPS, E_PER_GROUP)
    p_exp = jax.nn.softmax(e_all[ar, g_sel], axis=-1)
    top_p, top_i = lax.top_k(p_exp, TOP_E)
    gate = p_group[ar, g_sel][:, None] * top_p / jnp.sum(top_p, axis=-1, keepdims=True)
    eid = g_sel[:, None].astype(jnp.int32) * E_PER_GROUP + top_i.astype(jnp.int32)
    a = n * TOP_E
    flat_e = eid.reshape(a)
    flat_t = jnp.repeat(jnp.arange(n, dtype=jnp.int32), TOP_E)
    flat_g = gate.reshape(a)
    order = jnp.argsort(flat_e)
    se = flat_e[order]
    counts = jnp.zeros((N_EXPERTS,), jnp.int32).at[flat_e].add(1)
    starts = jnp.cumsum(counts) - counts
    pcounts = (counts + MOE_BLK - 1) // MOE_BLK * MOE_BLK
    pends = jnp.cumsum(pcounts)
    pstarts = pends - pcounts
    dest = pstarts[se] + (jnp.arange(a, dtype=jnp.int32) - starts[se])
    n_blk = -(-a // MOE_BLK) + N_EXPERTS
    p_rows = n_blk * MOE_BLK
    row_tok = jnp.zeros((p_rows,), jnp.int32).at[dest].set(flat_t[order])
    row_gate = jnp.zeros((p_rows,), jnp.float32).at[dest].set(flat_g[order])
    blk_exp = jnp.minimum(jnp.searchsorted(pends, jnp.arange(n_blk, dtype=jnp.int32) * MOE_BLK, side='right'), N_EXPERTS - 1)
    xr = h[row_tok].reshape(n_blk, MOE_BLK, h.shape[-1])

    def expert_block(args):
        xb, e = args
        return jnp.dot(jax.nn.silu(jnp.dot(xb, w1[e])) * jnp.dot(xb, w3[e]), w2[e])

    yr = lax.map(expert_block, (xr, blk_exp)).reshape(p_rows, h.shape[-1])
    y = jnp.zeros((n, h.shape[-1]), jnp.float32).at[row_tok].add(yr.astype(jnp.float32) * row_gate[:, None])
    return y.astype(h.dtype)


def trunk(x, pos, conv_ctx, mem_k, mem_v, moba_fn, P):
    b, s, _ = x.shape
    new_conv = []
    k_sh = None
    v_sh = None
    for l in range(DEPTH):
        h = rmsnorm(x, P['g_mix'][l])
        if l < N_A:
            u = h @ P['w_in_a'][l]
            ua, ub, qm = jnp.split(u, [D_CONV, 2 * D_CONV], axis=-1)
            z = ua * jax.nn.sigmoid(ub)
            z_ext = jnp.concatenate([conv_ctx[l].astype(z.dtype), z], axis=1)
            new_conv.append(z_ext[:, -(CONV_W - 1):])
            c = depthwise_causal_conv(z_ext, P['conv_w'][l], P['conv_b'][l])
            c = jax.nn.silu(layernorm(c, P['cln_g'][l], P['cln_b'][l]))
            w_out = P['w_out_a'][l]
        else:
            ib = l - N_A
            u = h @ P['w_in_b'][ib]
            qb, qm = jnp.split(u, [H_B * HEAD_DIM], axis=-1)
            qb = rotary(qb.reshape(b, s, H_B, HEAD_DIM), pos)
            c = moba_fn(qb, k_sh, v_sh).reshape(b, s, H_B * HEAD_DIM)
            w_out = P['w_out_b'][ib]
        om = mem_attend(qm.reshape(b, s, MEM_H, HEAD_DIM), mem_k[l], mem_v[l]).reshape(b, s, D_MEMQ)
        x = x + jnp.concatenate([c, om.astype(c.dtype)], axis=-1) @ w_out
        hf = rmsnorm(x, P['g_ffn'][l]).reshape(b * s, D_MODEL)
        x = x + hier_moe(hf, P['w_rg'][l], P['b_rg'][l], P['w_re'][l], P['b_re'][l],
                         P['w1'][l], P['w3'][l], P['w2'][l]).reshape(b, s, D_MODEL)
        if l == N_A - 1:
            kv = (rmsnorm(x, P['g_kv']) @ P['w_kv']).reshape(b, s, 2, H_B, HEAD_DIM)
            k_sh = rotary(kv[:, :, 0], pos)
            v_sh = kv[:, :, 1]
    return rmsnorm(x, P['g_final']), jnp.stack(new_conv), k_sh, v_sh


def setup_inputs(seed: int = 0) -> dict:
    key = jax.random.key(seed)
    ks = iter(jax.random.split(key, 40))
    f32 = jnp.float32

    def nrm(shape, scale):
        return jax.random.normal(next(ks), shape, f32) * scale

    n_pages = PAST_LEN // PAGE_SIZE
    n_used = DEC_BATCH * n_pages
    n_pool = n_used + max(1, n_used // 4)
    page_table = jax.random.permutation(next(ks), n_pool)[:n_used].reshape(DEC_BATCH, n_pages).astype(jnp.int32)
    d_in_a = 2 * D_CONV + D_MEMQ
    d_mix_a = D_CONV + D_MEMQ
    d_in_b = H_B * HEAD_DIM + D_MEMQ
    return {
        'x_prompt': nrm((BATCH, SEQ, D_MODEL), 1.0),
        'x_sample': nrm((DEC_BATCH, DEC_SEQ, D_MODEL), 1.0),
        'state_conv': nrm((N_A, DEC_BATCH, CONV_W - 1, D_CONV), 0.5),
        'cache_k': nrm((n_pool, PAGE_SIZE, H_B, HEAD_DIM), 1.0),
        'cache_v': nrm((n_pool, PAGE_SIZE, H_B, HEAD_DIM), 1.0),
        'cache_mem_k': nrm((DEPTH, DEC_BATCH, MEM_LEN, MEM_H, HEAD_DIM), 1.0),
        'cache_mem_v': nrm((DEPTH, DEC_BATCH, MEM_LEN, MEM_H, HEAD_DIM), 1.0),
        'page_table': page_table,
        'mem_prompt': nrm((BATCH, MEM_LEN, D_MODEL), 1.0),
        'g_mix': 1.0 + nrm((DEPTH, D_MODEL), 0.01),
        'g_ffn': 1.0 + nrm((DEPTH, D_MODEL), 0.01),
        'g_final': 1.0 + nrm((D_MODEL,), 0.01),
        'g_mem': 1.0 + nrm((DEPTH, D_MODEL), 0.01),
        'w_mem_kv': nrm((DEPTH, D_MODEL, 2 * D_MEMQ), D_MODEL ** -0.5),
        'w_in_a': nrm((N_A, D_MODEL, d_in_a), D_MODEL ** -0.5),
        'conv_w': nrm((N_A, CONV_W, D_CONV), CONV_W ** -0.5),
        'conv_b': nrm((N_A, D_CONV), 0.01),
        'cln_g': 1.0 + nrm((N_A, D_CONV), 0.01),
        'cln_b': nrm((N_A, D_CONV), 0.01),
        'w_out_a': nrm((N_A, d_mix_a, D_MODEL), d_mix_a ** -0.5),
        'g_kv': 1.0 + nrm((D_MODEL,), 0.01),
        'w_kv': nrm((D_MODEL, 2 * H_B * HEAD_DIM), D_MODEL ** -0.5),
        'w_in_b': nrm((N_B, D_MODEL, d_in_b), D_MODEL ** -0.5),
        'w_out_b': nrm((N_B, d_in_b, D_MODEL), d_in_b ** -0.5),
        'w_rg': nrm((DEPTH, D_MODEL, N_GROUPS), D_MODEL ** -0.5),
        'b_rg': nrm((DEPTH, N_GROUPS), 0.01),
        'w_re': nrm((DEPTH, D_MODEL, N_EXPERTS), D_MODEL ** -0.5),
        'b_re': nrm((DEPTH, N_EXPERTS), 0.01),
        'w1': nrm((DEPTH, N_EXPERTS, D_MODEL, D_EXPERT), D_MODEL ** -0.5),
        'w3': nrm((DEPTH, N_EXPERTS, D_MODEL, D_EXPERT), D_MODEL ** -0.5),
        'w2': nrm((DEPTH, N_EXPERTS, D_EXPERT, D_MODEL), D_EXPERT ** -0.5),
    }


def reference(x_prompt, x_sample, state_conv, cache_k, cache_v, cache_mem_k, cache_mem_v, page_table,
              mem_prompt, g_mix, g_ffn, g_final, g_mem, w_mem_kv, w_in_a, conv_w, conv_b, cln_g, cln_b,
              w_out_a, g_kv, w_kv, w_in_b, w_out_b, w_rg, b_rg, w_re, b_re, w1, w3, w2):
    P = {'g_mix': g_mix, 'g_ffn': g_ffn, 'g_final': g_final, 'w_in_a': w_in_a, 'conv_w': conv_w,
         'conv_b': conv_b, 'cln_g': cln_g, 'cln_b': cln_b, 'w_out_a': w_out_a, 'g_kv': g_kv,
         'w_kv': w_kv, 'w_in_b': w_in_b, 'w_out_b': w_out_b, 'w_rg': w_rg, 'b_rg': b_rg,
         'w_re': w_re, 'b_re': b_re, 'w1': w1, 'w3': w3, 'w2': w2}
    mk_list = []
    mv_list = []
    for l in range(DEPTH):
        mk, mv = mem_kv(mem_prompt, g_mem[l], w_mem_kv[l])
        mk_list.append(mk)
        mv_list.append(mv)
    mem_k_p = jnp.stack(mk_list)
    mem_v_p = jnp.stack(mv_list)

    b_p, s_p, _ = x_prompt.shape
    pos_p = jnp.arange(s_p)
    ctx_p = jnp.zeros((N_A, b_p, CONV_W - 1, D_CONV), x_prompt.dtype)
    y_p, conv_p, k_p, v_p = trunk(x_prompt, pos_p, ctx_p, mem_k_p, mem_v_p, moba_prompt, P)

    past = page_table.shape[1] * PAGE_SIZE
    pos_s = past + jnp.arange(x_sample.shape[1])

    def moba_fn_s(q, k, v):
        return moba_sample(q, k, v, cache_k, cache_v, page_table)

    y_s, conv_s, k_s, v_s = trunk(x_sample, pos_s, state_conv, cache_mem_k, cache_mem_v, moba_fn_s, P)
    return (y_p, y_s, conv_p, conv_s, k_p, v_p, k_s, v_s, mem_k_p, mem_v_p)
```

```python
import functools
import math

import jax
import jax.numpy as jnp
from jax import lax
from jax.experimental import pallas as pl
from jax.experimental.pallas import tpu as pltpu

F32 = jnp.float32
BF16 = jnp.bfloat16
I32 = jnp.int32
HIGHEST = lax.Precision.HIGHEST

EPS = 1e-6
HEAD_DIM = 64
MEM_H = 4
D_MEMQ = MEM_H * HEAD_DIM
CONV_W = 31
CTX_ROWS = 32
CTX_PAD = CTX_ROWS - (CONV_W - 1)
MOBA_BLOCK = 256
MOBA_TOPK = 3
Q_CHUNK = 128
ROT_DIM = HEAD_DIM // 4
ROPE_THETA = 500000.0
N_GROUPS = 4
E_PER_GROUP = 16
N_EXPERTS = N_GROUPS * E_PER_GROUP
MOE_BLK = 128
ROUTER_LANES = 128
LANE = 128
SUBLANE = 8
VMEM_LIMIT = 56 * 1024 * 1024

NEG_INF = float("-inf")
BIG_IDX = 1 << 20


def _rms(x, g):
    return x * lax.rsqrt(jnp.mean(x * x, axis=-1, keepdims=True) + EPS) * g


def _first_argmax(vals, lane):
    m = jnp.max(vals, axis=-1, keepdims=True)
    idx = jnp.min(jnp.where(vals == m, lane, BIG_IDX), axis=-1, keepdims=True)
    return m, idx


def _rot(a, cos, sin):
    lane = lax.broadcasted_iota(I32, cos.shape, 1)
    first = (lane % HEAD_DIM) < (ROT_DIM // 2)
    outs = []
    for cb in range(a.shape[1] // LANE):
        ac = a[:, cb * LANE:(cb + 1) * LANE]
        up = pltpu.roll(ac, LANE - ROT_DIM // 2, 1)
        dn = pltpu.roll(ac, ROT_DIM // 2, 1)
        outs.append(ac * cos + jnp.where(first, up, dn) * sin)
    return jnp.concatenate(outs, axis=1)


def _rope_tables(pos):
    half = ROT_DIM // 2
    inv = ROPE_THETA ** (-jnp.arange(half, dtype=F32) * (2.0 / ROT_DIM))
    ang = pos.astype(F32)[:, None] * inv[None, :]
    cos, sin = jnp.cos(ang), jnp.sin(ang)
    n = pos.shape[0]
    one = jnp.ones((n, HEAD_DIM - ROT_DIM), F32)
    zero = jnp.zeros((n, HEAD_DIM - ROT_DIM), F32)
    c64 = jnp.concatenate([cos, cos, one], axis=1)
    s64 = jnp.concatenate([-sin, sin, zero], axis=1)
    return jnp.tile(c64, (1, LANE // HEAD_DIM)), jnp.tile(s64, (1, LANE // HEAD_DIM))


def _memkv_kernel(mem_ref, g_ref, wt_ref, out_ref):
    h = _rms(mem_ref[0], g_ref[0]).astype(BF16)
    out_ref[0, 0] = lax.dot_general(wt_ref[0], h, (((1,), (1,)), ((), ())), preferred_element_type=F32)


def _memkv(mem, g_mem, w_mem_kv_t_bf):
    b, m, d = mem.shape
    depth = g_mem.shape[0]
    n_out = w_mem_kv_t_bf.shape[1]
    return pl.pallas_call(
        _memkv_kernel,
        out_shape=jax.ShapeDtypeStruct((depth, b, n_out, m), F32),
        grid=(depth, b),
        in_specs=[
            pl.BlockSpec((1, m, d), lambda l, i: (i, 0, 0)),
            pl.BlockSpec((1, 1, d), lambda l, i: (l, 0, 0)),
            pl.BlockSpec((1, n_out, d), lambda l, i: (l, 0, 0)),
        ],
        out_specs=pl.BlockSpec((1, 1, n_out, m), lambda l, i: (l, i, 0, 0)),
        compiler_params=pltpu.CompilerParams(dimension_semantics=("arbitrary", "arbitrary")),
        name="memkv",
    )(mem, g_mem.reshape(depth, 1, d), w_mem_kv_t_bf)


def _mem_attn_into(mix, off, qm, mkt_ref, mvt_ref):
    mkt = mkt_ref[0, 0].astype(BF16)
    mvt = mvt_ref[0, 0].astype(BF16)
    scale = HEAD_DIM ** -0.5
    lane = lax.broadcasted_iota(I32, qm.shape, 1)
    om = jnp.zeros(qm.shape, F32)
    for hh in range(MEM_H):
        in_head = (lane // HEAD_DIM) == hh
        q = jnp.where(in_head, qm, 0.0).astype(BF16)
        s = jnp.dot(q, mkt, preferred_element_type=F32) * scale
        m = jnp.max(s, axis=-1, keepdims=True)
        e = jnp.exp(s - m)
        l = jnp.sum(e, axis=-1, keepdims=True)
        o = lax.dot_general(e.astype(BF16), mvt, (((1,), (1,)), ((), ())), preferred_element_type=F32) / l
        om = jnp.where(in_head, o, om)
    mix[:, off:off + D_MEMQ] = om.astype(BF16)


def _out_and_route(x, mix, wout_ref, gffn_ref, wr_ref, br_ref, x1_ref, hf_ref, gate_ref, eid_ref):
    x1 = x + jnp.dot(mix[...], wout_ref[...], preferred_element_type=F32)
    x1_ref[...] = x1
    hf = _rms(x1, gffn_ref[...])
    hf_ref[...] = hf
    logits = jnp.dot(hf, wr_ref[...], preferred_element_type=F32, precision=HIGHEST) + br_ref[...]
    lane = lax.broadcasted_iota(I32, logits.shape, 1)
    gl = jnp.where((lane >= N_EXPERTS) & (lane < N_EXPERTS + N_GROUPS), logits, NEG_INF)
    gmax, glane = _first_argmax(gl, lane)
    p_group = 1.0 / jnp.sum(jnp.exp(gl - gmax), axis=-1, keepdims=True)
    e_lo = (glane - N_EXPERTS) * E_PER_GROUP
    el = jnp.where((lane >= e_lo) & (lane < e_lo + E_PER_GROUP), logits, NEG_INF)
    m1, i1 = _first_argmax(el, lane)
    el2 = jnp.where(lane == i1, NEG_INF, el)
    m2, i2 = _first_argmax(el2, lane)
    e2 = jnp.exp(m2 - m1)
    g1 = p_group / (1.0 + e2)
    g2 = g1 * e2
    lane2 = lax.broadcasted_iota(I32, gate_ref.shape, 1)
    gate_ref[...] = jnp.where(lane2 == 0, g1, g2)
    eid_ref[...] = jnp.where(lane2 == 0, i1, i2)


def _router_weights(w_rg, b_rg, w_re, b_re):
    d = w_rg.shape[0]
    pad = ROUTER_LANES - N_EXPERTS - N_GROUPS
    wr = jnp.concatenate([w_re, w_rg, jnp.zeros((d, pad), F32)], axis=1)
    br = jnp.concatenate([b_re, b_rg, jnp.zeros((pad,), F32)]).reshape(1, ROUTER_LANES)
    return wr, br


def _layer0_kernel(x_ref, ctx_ref, mk_ref, mv_ref, gmix_ref, win_ref, cw_ref, cb_ref, lg_ref, lb_ref,
                   wout_ref, gffn_ref, wr_ref, br_ref,
                   x1_ref, hf_ref, gate_ref, eid_ref, nc_ref,
                   zext, zsh, cbuf, mix, *, ts, rc, d_conv):
    t = pl.program_id(1)
    nt = pl.num_programs(1)

    @pl.when(t == 0)
    def _():
        zext[0:CTX_PAD, :] = jnp.zeros((CTX_PAD, d_conv), F32)
        zext[CTX_PAD:CTX_ROWS, :] = ctx_ref[0]

    @pl.when(t > 0)
    def _():
        zext[0:CTX_ROWS, :] = zext[ts:ts + CTX_ROWS, :]

    x = x_ref[...]
    h = _rms(x, gmix_ref[...]).astype(BF16)
    u = jnp.dot(h, win_ref[...], preferred_element_type=F32)
    z = u[:, :d_conv] * jax.nn.sigmoid(u[:, d_conv:2 * d_conv])
    zext[CTX_ROWS:CTX_ROWS + ts, :] = z
    qm = u[:, 2 * d_conv:]

    @pl.when(t == nt - 1)
    def _():
        nc_ref[0] = zext[ts + CTX_PAD: ts + CTX_ROWS, :]

    for sh in range(1, SUBLANE):
        zsh[sh - 1] = zext[sh:sh + ts + CTX_ROWS - SUBLANE, :]

    def chunk(ci, carry):
        r0 = pl.multiple_of(ci * rc, SUBLANE)
        for cbk in range(d_conv // LANE):
            cs = slice(cbk * LANE, (cbk + 1) * LANE)
            acc = jnp.zeros((rc, LANE), F32) + cb_ref[:, cs]
            for w in range(CONV_W):
                hi, sh = divmod(w + CTX_PAD, SUBLANE)
                if sh == 0:
                    win = zext[pl.ds(r0 + hi * SUBLANE, rc), cs]
                else:
                    win = zsh[sh - 1, pl.ds(r0 + hi * SUBLANE, rc), cs]
                acc = acc + win * cw_ref[w:w + 1, cs]
            cbuf[pl.ds(r0, rc), cs] = acc
        return carry

    lax.fori_loop(0, ts // rc, chunk, 0)

    c = cbuf[...]
    mu = jnp.mean(c, axis=-1, keepdims=True)
    xc = c - mu
    y = xc * lax.rsqrt(jnp.mean(xc * xc, axis=-1, keepdims=True) + EPS) * lg_ref[...] + lb_ref[...]
    mix[:, :d_conv] = (y * jax.nn.sigmoid(y)).astype(BF16)
    _mem_attn_into(mix, d_conv, qm, mk_ref, mv_ref)
    _out_and_route(x, mix, wout_ref, gffn_ref, wr_ref, br_ref, x1_ref, hf_ref, gate_ref, eid_ref)


def _layer0(x2d, b, s, ctx, mk_arr, mv_arr, kv_cols, l, P, ts):
    n, d = x2d.shape
    nt = s // ts
    d_conv = ctx.shape[-1]
    rc = min(32, ts)
    d_in = P["w_in_a_bf"].shape[-1]
    kcol, vcol = kv_cols
    const = lambda *shape: pl.BlockSpec(shape, lambda i, t: (0,) * len(shape))
    tok = lambda w: pl.BlockSpec((ts, w), lambda i, t: (i * nt + t, 0))
    kern = functools.partial(_layer0_kernel, ts=ts, rc=rc, d_conv=d_conv)
    return pl.pallas_call(
        kern,
        out_shape=(
            jax.ShapeDtypeStruct((n, d), F32),
            jax.ShapeDtypeStruct((n, d), F32),
            jax.ShapeDtypeStruct((n, 2), F32),
            jax.ShapeDtypeStruct((n, 2), I32),
            jax.ShapeDtypeStruct((b, CONV_W - 1, d_conv), F32),
        ),
        grid=(b, nt),
        in_specs=[
            tok(d),
            pl.BlockSpec((1, CONV_W - 1, d_conv), lambda i, t: (i, 0, 0)),
            pl.BlockSpec((1, 1, D_MEMQ, mk_arr.shape[3]), lambda i, t: (l, i, kcol, 0)),
            pl.BlockSpec((1, 1, D_MEMQ, mv_arr.shape[3]), lambda i, t: (l, i, vcol, 0)),
            const(1, d), const(d, d_in), const(CONV_W, d_conv), const(1, d_conv), const(1, d_conv),
            const(1, d_conv), const(d_conv + D_MEMQ, d), const(1, d), const(d, ROUTER_LANES),
            const(1, ROUTER_LANES),
        ],
        out_specs=(tok(d), tok(d), tok(2), tok(2),
                   pl.BlockSpec((1, CONV_W - 1, d_conv), lambda i, t: (i, 0, 0))),
        scratch_shapes=[
            pltpu.VMEM((ts + CTX_ROWS, d_conv), F32),
            pltpu.VMEM((SUBLANE - 1, ts + CTX_ROWS - SUBLANE, d_conv), F32),
            pltpu.VMEM((ts, d_conv), F32),
            pltpu.VMEM((ts, d_conv + D_MEMQ), BF16),
        ],
        compiler_params=pltpu.CompilerParams(
            dimension_semantics=("arbitrary", "arbitrary"), vmem_limit_bytes=VMEM_LIMIT),
        name="layer0",
    )(x2d, ctx, mk_arr, mv_arr, P["g_mix"][l].reshape(1, d), P["w_in_a_bf"], P["conv_w"][0],
      P["conv_b"][0].reshape(1, d_conv), P["cln_g"][0].reshape(1, d_conv), P["cln_b"][0].reshape(1, d_conv),
      P["w_out_a_bf"], P["g_ffn"][l].reshape(1, d), P["wr"][l], P["br"][l])


def _layer1_kernel(x_ref, c_ref, qm_ref, mk_ref, mv_ref, wout_ref, gffn_ref, wr_ref, br_ref,
                   x1_ref, hf_ref, gate_ref, eid_ref, mix, *, d_att):
    mix[:, :d_att] = c_ref[...].astype(BF16)
    _mem_attn_into(mix, d_att, qm_ref[...], mk_ref, mv_ref)
    _out_and_route(x_ref[...], mix, wout_ref, gffn_ref, wr_ref, br_ref, x1_ref, hf_ref, gate_ref, eid_ref)


def _layer1(x2d, c2d, qm2d, b, s, mk_arr, mv_arr, kv_cols, l, P, ts):
    n, d = x2d.shape
    nt = s // ts
    d_att = c2d.shape[-1]
    kcol, vcol = kv_cols
    const = lambda *shape: pl.BlockSpec(shape, lambda i, t: (0,) * len(shape))
    tok = lambda w: pl.BlockSpec((ts, w), lambda i, t: (i * nt + t, 0))
    return pl.pallas_call(
        functools.partial(_layer1_kernel, d_att=d_att),
        out_shape=(
            jax.ShapeDtypeStruct((n, d), F32),
            jax.ShapeDtypeStruct((n, d), F32),
            jax.ShapeDtypeStruct((n, 2), F32),
            jax.ShapeDtypeStruct((n, 2), I32),
        ),
        grid=(b, nt),
        in_specs=[
            tok(d), tok(d_att), tok(D_MEMQ),
            pl.BlockSpec((1, 1, D_MEMQ, mk_arr.shape[3]), lambda i, t: (l, i, kcol, 0)),
            pl.BlockSpec((1, 1, D_MEMQ, mv_arr.shape[3]), lambda i, t: (l, i, vcol, 0)),
            const(d_att + D_MEMQ, d), const(1, d), const(d, ROUTER_LANES), const(1, ROUTER_LANES),
        ],
        out_specs=(tok(d), tok(d), tok(2), tok(2)),
        scratch_shapes=[pltpu.VMEM((ts, d_att + D_MEMQ), BF16)],
        compiler_params=pltpu.CompilerParams(
            dimension_semantics=("arbitrary", "arbitrary"), vmem_limit_bytes=VMEM_LIMIT),
        name="layer1",
    )(x2d, c2d, qm2d, mk_arr, mv_arr, P["w_out_b_bf"], P["g_ffn"][l].reshape(1, d), P["wr"][l], P["br"][l])


def _gather_step(step, nsteps, idx_hbm, src_hbm, idx_smem, buf, sem_idx, sem_rows, nrows):
    slot = lax.rem(step, 2)
    nslot = 1 - slot

    def idx_copy(k, sl):
        return pltpu.make_async_copy(idx_hbm.at[k], idx_smem.at[sl], sem_idx.at[sl])

    def row_copy(tok, sl, r):
        return pltpu.make_async_copy(src_hbm.at[pl.ds(tok, 1)], buf.at[sl, pl.ds(r, 1)], sem_rows.at[sl])

    def issue(sl):
        def body(r, carry):
            row_copy(idx_smem[sl, r], sl, r).start()
            return carry
        lax.fori_loop(0, nrows, body, 0)

    @pl.when(step == 0)
    def _():
        first = idx_copy(0, 0)
        first.start()
        first.wait()
        issue(0)

        @pl.when(nsteps > 1)
        def _():
            idx_copy(1, 1).start()

    @pl.when(step + 1 < nsteps)
    def _():
        idx_copy(step + 1, nslot).wait()
        issue(nslot)

        @pl.when(step + 2 < nsteps)
        def _():
            idx_copy(step + 2, slot).start()

    def drain(r, carry):
        row_copy(0, slot, r).wait()
        return carry
    lax.fori_loop(0, nrows, drain, 0)
    return slot


def _moe_kernel(bexp_ref, nused_ref, tok_hbm, hf_hbm, w1_ref, w3_ref, w2_ref, out_ref,
                idx_smem, xbuf, w13, w2s, sem_idx, sem_rows, *, d_exp):
    i = pl.program_id(0)
    nused = nused_ref[0]

    @pl.when(i < nused)
    def _():
        slot = _gather_step(i, nused, tok_hbm, hf_hbm, idx_smem, xbuf, sem_idx, sem_rows, MOE_BLK)
        changed = jnp.logical_or(i == 0, bexp_ref[i] != bexp_ref[jnp.maximum(i - 1, 0)])

        @pl.when(changed)
        def _():
            w13[:, :d_exp] = w1_ref[0, 0].astype(BF16)
            w13[:, d_exp:] = w3_ref[0, 0].astype(BF16)
            w2s[...] = w2_ref[0, 0].astype(BF16)

        xb = xbuf[slot].astype(BF16)
        hcat = jnp.dot(xb, w13[...], preferred_element_type=F32)
        a = hcat[:, :d_exp]
        act = (a * jax.nn.sigmoid(a)) * hcat[:, d_exp:]
        out_ref[...] = jnp.dot(act.astype(BF16), w2s[...], preferred_element_type=F32)

    @pl.when(i >= nused)
    def _():
        out_ref[...] = jnp.zeros(out_ref.shape, F32)


def _moe(hf, eid, l, w1, w3, w2):
    n, d = hf.shape
    d_exp = w1.shape[-1]
    a = 2 * n
    n_blk = a // MOE_BLK + N_EXPERTS
    flat_e = eid.reshape(a)
    onehot = (flat_e[:, None] == jnp.arange(N_EXPERTS, dtype=I32)[None, :]).astype(I32)
    csum = jnp.cumsum(onehot, axis=0)
    counts = csum[-1]
    rank = jnp.sum(csum * onehot, axis=1) - 1
    pcounts = (counts + MOE_BLK - 1) // MOE_BLK * MOE_BLK
    pends = jnp.cumsum(pcounts)
    pstarts = pends - pcounts
    dest = pstarts[flat_e] + rank
    row_tok = jnp.zeros((n_blk * MOE_BLK,), I32).at[dest].set(jnp.arange(a, dtype=I32) // 2)
    blk_exp = jnp.minimum(
        jnp.searchsorted(pends, jnp.arange(n_blk, dtype=I32) * MOE_BLK, side="right"), N_EXPERTS - 1
    ).astype(I32)
    nused = (pends[-1] // MOE_BLK).astype(I32).reshape(1)

    wspec = lambda r, c: pl.BlockSpec((1, 1, r, c), lambda i, be, nu: (l, be[i], 0, 0))
    yr = pl.pallas_call(
        functools.partial(_moe_kernel, d_exp=d_exp),
        out_shape=jax.ShapeDtypeStruct((n_blk * MOE_BLK, d), F32),
        grid_spec=pltpu.PrefetchScalarGridSpec(
            num_scalar_prefetch=2,
            grid=(n_blk,),
            in_specs=[
                pl.BlockSpec(memory_space=pl.ANY),
                pl.BlockSpec(memory_space=pl.ANY),
                wspec(d, d_exp), wspec(d, d_exp), wspec(d_exp, d),
            ],
            out_specs=pl.BlockSpec((MOE_BLK, d), lambda i, be, nu: (i, 0)),
            scratch_shapes=[
                pltpu.SMEM((2, MOE_BLK), I32),
                pltpu.VMEM((2, MOE_BLK, d), F32),
                pltpu.VMEM((d, 2 * d_exp), BF16),
                pltpu.VMEM((d_exp, d), BF16),
                pltpu.SemaphoreType.DMA((2,)),
                pltpu.SemaphoreType.DMA((2,)),
            ],
        ),
        compiler_params=pltpu.CompilerParams(
            dimension_semantics=("arbitrary",), vmem_limit_bytes=VMEM_LIMIT),
        name="moe_experts",
    )(blk_exp, nused, row_tok.reshape(n_blk, MOE_BLK), hf, w1, w3, w2)
    return yr, dest.reshape(n, 2)


def _dest_tiles(dest, tm):
    n = dest.shape[0]
    return dest.reshape(n // tm, tm, 2).transpose(0, 2, 1).reshape(n // tm, 2 * tm)


def _combine(step, nsteps, dest_hbm, yr_hbm, x1_ref, gate_ref, idx_smem, ybuf, sem_idx, sem_rows, tm):
    slot = _gather_step(step, nsteps, dest_hbm, yr_hbm, idx_smem, ybuf, sem_idx, sem_rows, 2 * tm)
    g = gate_ref[...]
    y = g[:, 0:1] * ybuf[slot, 0:tm, :] + g[:, 1:2] * ybuf[slot, tm:2 * tm, :]
    return x1_ref[...] + y


def _mid_kernel(dest_hbm, yr_hbm, x1_ref, gate_ref, cos_ref, sin_ref, cosT_ref, sinT_ref,
                gkv_ref, wkvt_ref, gmix_ref, winb_ref,
                x2_ref, kt_ref, vt_ref, qb_ref, qm_ref,
                idx_smem, ybuf, sem_idx, sem_rows, *, tm, d_att):
    i = pl.program_id(0)
    x2 = _combine(i, pl.num_programs(0), dest_hbm, yr_hbm, x1_ref, gate_ref,
                  idx_smem, ybuf, sem_idx, sem_rows, tm)
    x2_ref[...] = x2
    hk = _rms(x2, gkv_ref[...]).astype(BF16)
    kvt = lax.dot_general(wkvt_ref[...], hk, (((1,), (1,)), ((), ())), preferred_element_type=F32)
    vt_ref[0] = kvt[d_att:, :]
    kt_ref[0] = kvt[:d_att, :]
    half = ROT_DIM // 2
    cos_t = cosT_ref[...]
    sin_t = sinT_ref[...]
    for hh in range(d_att // HEAD_DIM):
        r0 = hh * HEAD_DIM
        xa = kvt[r0:r0 + half, :]
        xb = kvt[r0 + half:r0 + ROT_DIM, :]
        kt_ref[0, r0:r0 + half, :] = xa * cos_t - xb * sin_t
        kt_ref[0, r0 + half:r0 + ROT_DIM, :] = xb * cos_t + xa * sin_t
    u = jnp.dot(_rms(x2, gmix_ref[...]).astype(BF16), winb_ref[...], preferred_element_type=F32)
    qb_ref[...] = _rot(u[:, :d_att], cos_ref[...], sin_ref[...])
    qm_ref[...] = u[:, d_att:]


def _mid(x1, gate, yr, dest, pos, n_seq, P, tm):
    n, d = x1.shape
    d_att = P["w_kv_t_bf"].shape[0] // 2
    nsteps = n // tm
    s_out = n // n_seq
    nt = s_out // tm
    half = ROT_DIM // 2
    reps = s_out // pos.shape[0]
    inv = ROPE_THETA ** (-jnp.arange(half, dtype=F32) * (2.0 / ROT_DIM))
    ang_t = jnp.tile(inv[:, None] * pos.astype(F32)[None, :], (1, reps))
    cos_rows, sin_rows = _rope_tables(jnp.tile(pos, reps))
    const = lambda *shape: pl.BlockSpec(shape, lambda i: (0,) * len(shape))
    tok = lambda w: pl.BlockSpec((tm, w), lambda i: (i, 0))
    tbl = pl.BlockSpec((tm, LANE), lambda i: (i % nt, 0))
    tbl_t = pl.BlockSpec((half, tm), lambda i: (0, i % nt))
    kv_t = pl.BlockSpec((1, d_att, tm), lambda i: (i // nt, 0, i % nt))
    return pl.pallas_call(
        functools.partial(_mid_kernel, tm=tm, d_att=d_att),
        out_shape=(
            jax.ShapeDtypeStruct((n, d), F32),
            jax.ShapeDtypeStruct((n_seq, d_att, s_out), F32),
            jax.ShapeDtypeStruct((n_seq, d_att, s_out), F32),
            jax.ShapeDtypeStruct((n, d_att), F32),
            jax.ShapeDtypeStruct((n, D_MEMQ), F32),
        ),
        grid=(nsteps,),
        in_specs=[
            pl.BlockSpec(memory_space=pl.ANY), pl.BlockSpec(memory_space=pl.ANY),
            tok(d), tok(2), tbl, tbl, tbl_t, tbl_t,
            const(1, d), const(2 * d_att, d), const(1, d), const(d, d_att + D_MEMQ),
        ],
        out_specs=(tok(d), kv_t, kv_t, tok(d_att), tok(D_MEMQ)),
        scratch_shapes=[
            pltpu.SMEM((2, 2 * tm), I32),
            pltpu.VMEM((2, 2 * tm, d), F32),
            pltpu.SemaphoreType.DMA((2,)),
            pltpu.SemaphoreType.DMA((2,)),
        ],
        compiler_params=pltpu.CompilerParams(
            dimension_semantics=("arbitrary",), vmem_limit_bytes=VMEM_LIMIT),
        name="mid",
    )(_dest_tiles(dest, tm), yr, x1, gate, cos_rows, sin_rows, jnp.cos(ang_t), jnp.sin(ang_t),
      P["g_kv"].reshape(1, d), P["w_kv_t_bf"], P["g_mix"][1].reshape(1, d), P["w_in_b_bf"])


def _final_kernel(dest_hbm, yr_hbm, x1_ref, gate_ref, g_ref, y_ref, idx_smem, ybuf, sem_idx, sem_rows, *, tm):
    i = pl.program_id(0)
    x = _combine(i, pl.num_programs(0), dest_hbm, yr_hbm, x1_ref, gate_ref,
                 idx_smem, ybuf, sem_idx, sem_rows, tm)
    y_ref[...] = _rms(x, g_ref[...])


def _final(x1, gate, yr, dest, g_final, tm):
    n, d = x1.shape
    tok = lambda w: pl.BlockSpec((tm, w), lambda i: (i, 0))
    return pl.pallas_call(
        functools.partial(_final_kernel, tm=tm),
        out_shape=jax.ShapeDtypeStruct((n, d), F32),
        grid=(n // tm,),
        in_specs=[
            pl.BlockSpec(memory_space=pl.ANY), pl.BlockSpec(memory_space=pl.ANY),
            tok(d), tok(2), pl.BlockSpec((1, d), lambda i: (0, 0)),
        ],
        out_specs=tok(d),
        scratch_shapes=[
            pltpu.SMEM((2, 2 * tm), I32),
            pltpu.VMEM((2, 2 * tm, d), F32),
            pltpu.SemaphoreType.DMA((2,)),
            pltpu.SemaphoreType.DMA((2,)),
        ],
        compiler_params=pltpu.CompilerParams(
            dimension_semantics=("arbitrary",), vmem_limit_bytes=VMEM_LIMIT),
        name="final",
    )(_dest_tiles(dest, tm), yr, x1, gate, g_final.reshape(1, d))


def _moba_prompt_kernel(q_ref, kt_ref, vt_ref, o_ref, kmean_t, kbf, vbf, *, s_len):
    qi = pl.program_id(2)
    nb = s_len // MOBA_BLOCK
    scale = HEAD_DIM ** -0.5
    nt_dims = (((1,), (1,)), ((), ()))

    @pl.when(qi == 0)
    def _():
        klane = lax.broadcasted_iota(I32, (LANE, LANE), 1)
        km = jnp.zeros((LANE, LANE), F32)
        for jb in range(nb):
            kblk = kt_ref[0, :, jb * MOBA_BLOCK:(jb + 1) * MOBA_BLOCK]
            km = jnp.where(klane == jb, jnp.sum(kblk, axis=-1, keepdims=True) * (1.0 / MOBA_BLOCK), km)
            kbf[jb] = kblk.astype(BF16)
            vbf[jb] = vt_ref[0, :, jb * MOBA_BLOCK:(jb + 1) * MOBA_BLOCK].astype(BF16)
        kmean_t[...] = km

    own = qi // (MOBA_BLOCK // Q_CHUNK)
    q_off = (qi % (MOBA_BLOCK // Q_CHUNK)) * Q_CHUNK
    q = q_ref[...]
    lane = lax.broadcasted_iota(I32, (Q_CHUNK, LANE), 1)
    row = lax.broadcasted_iota(I32, (Q_CHUNK, MOBA_BLOCK), 0)
    col = lax.broadcasted_iota(I32, (Q_CHUNK, MOBA_BLOCK), 1)
    causal = col <= row + q_off

    outs = []
    for hh in range(LANE // HEAD_DIM):
        in_head = (lane // HEAD_DIM) == hh
        qh = jnp.where(in_head, q, 0.0)
        gate = jnp.dot(qh, kmean_t[...], preferred_element_type=F32, precision=HIGHEST)
        gate = jnp.where(lane < own, gate, NEG_INF)
        sel = jnp.zeros((Q_CHUNK, LANE), F32)
        for _ in range(MOBA_TOPK):
            m, idx = _first_argmax(gate, lane)
            pick = jnp.logical_and(lane == idx, m > NEG_INF)
            sel = jnp.where(pick, 1.0, sel)
            gate = jnp.where(pick, NEG_INF, gate)

        qb = qh.astype(BF16)
        s = jnp.dot(qb, kbf[own], preferred_element_type=F32) * scale
        s = jnp.where(causal, s, NEG_INF)
        m0 = jnp.max(s, axis=-1, keepdims=True)
        p = jnp.exp(s - m0)
        l0 = jnp.sum(p, axis=-1, keepdims=True)
        acc0 = lax.dot_general(p.astype(BF16), vbf[own], nt_dims, preferred_element_type=F32)

        def body(j, carry):
            m, l, acc = carry
            chosen = jnp.sum(jnp.where(lane == j, sel, 0.0), axis=-1, keepdims=True) > 0.0
            sj = jnp.dot(qb, kbf[j], preferred_element_type=F32) * scale
            sj = jnp.where(chosen, sj, NEG_INF)
            m_new = jnp.maximum(m, jnp.max(sj, axis=-1, keepdims=True))
            alpha = jnp.exp(m - m_new)
            pj = jnp.exp(sj - m_new)
            l = alpha * l + jnp.sum(pj, axis=-1, keepdims=True)
            acc = alpha * acc + lax.dot_general(pj.astype(BF16), vbf[j], nt_dims, preferred_element_type=F32)
            return m_new, l, acc

        _, l, acc = lax.fori_loop(0, own, body, (m0, l0, acc0))
        outs.append((in_head, acc / l))

    o_ref[...] = jnp.where(outs[0][0], outs[0][1], outs[1][1])


def _moba_prompt(qb, kt, vt, b, s):
    n, d_att = qb.shape
    nq = s // Q_CHUNK
    nb = s // MOBA_BLOCK
    assert nb <= LANE and s % MOBA_BLOCK == 0
    kv_spec = pl.BlockSpec((1, LANE, s), lambda i, hp, qi: (i, hp, 0))
    return pl.pallas_call(
        functools.partial(_moba_prompt_kernel, s_len=s),
        out_shape=jax.ShapeDtypeStruct((n, d_att), F32),
        grid=(b, d_att // LANE, nq),
        in_specs=[
            pl.BlockSpec((Q_CHUNK, LANE), lambda i, hp, qi: (i * nq + qi, hp)),
            kv_spec, kv_spec,
        ],
        out_specs=pl.BlockSpec((Q_CHUNK, LANE), lambda i, hp, qi: (i * nq + qi, hp)),
        scratch_shapes=[
            pltpu.VMEM((LANE, LANE), F32),
            pltpu.VMEM((nb, LANE, MOBA_BLOCK), BF16),
            pltpu.VMEM((nb, LANE, MOBA_BLOCK), BF16),
        ],
        compiler_params=pltpu.CompilerParams(
            dimension_semantics=("arbitrary", "arbitrary", "arbitrary"), vmem_limit_bytes=VMEM_LIMIT),
        name="moba_prompt",
    )(qb, kt, vt)


def _head_diag(o, n_heads, ds):
    return jnp.concatenate(
        [o[h * ds:(h + 1) * ds, h * HEAD_DIM:(h + 1) * HEAD_DIM] for h in range(n_heads)], axis=0)


def _moba_sample_kernel(pt_ref, q_ref, kn_ref, vn_ref, k0_ref, k1_ref, v0_ref, v1_ref, o_ref,
                        qbd, g_all, m_all, l_all, o_all, *, ds, n_heads, nb):
    j = pl.program_id(1)
    rows = n_heads * ds
    d_att = n_heads * HEAD_DIM
    scale = HEAD_DIM ** -0.5
    nt_dims = (((1,), (1,)), ((), ()))
    blane = lax.broadcasted_iota(I32, (rows, LANE), 1)

    @pl.when(j == 0)
    def _():
        q = q_ref[...]
        lane = lax.broadcasted_iota(I32, (ds, d_att), 1)
        for h in range(n_heads):
            qbd[h * ds:(h + 1) * ds, :] = jnp.where(lane // HEAD_DIM == h, q, 0.0)
        g_all[...] = jnp.full((rows, LANE), NEG_INF, F32)
        m_all[...] = jnp.zeros((rows, LANE), F32)
        l_all[...] = jnp.zeros((rows, LANE), F32)

    qb = qbd[...].astype(BF16)
    s0 = jnp.dot(qb, k0_ref[0].astype(BF16), preferred_element_type=F32)
    s1 = jnp.dot(qb, k1_ref[0].astype(BF16), preferred_element_type=F32)
    g = (jnp.sum(s0, axis=-1, keepdims=True) + jnp.sum(s1, axis=-1, keepdims=True)) * (1.0 / MOBA_BLOCK)
    s0 = s0 * scale
    s1 = s1 * scale
    m = jnp.maximum(jnp.max(s0, axis=-1, keepdims=True), jnp.max(s1, axis=-1, keepdims=True))
    p0 = jnp.exp(s0 - m)
    p1 = jnp.exp(s1 - m)
    l = jnp.sum(p0, axis=-1, keepdims=True) + jnp.sum(p1, axis=-1, keepdims=True)
    o = (lax.dot_general(p0.astype(BF16), v0_ref[0].astype(BF16), nt_dims, preferred_element_type=F32)
         + lax.dot_general(p1.astype(BF16), v1_ref[0].astype(BF16), nt_dims, preferred_element_type=F32))
    o_all[j] = _head_diag(o, n_heads, ds)
    here = blane == j
    g_all[...] = jnp.where(here, g, g_all[...])
    m_all[...] = jnp.where(here, m, m_all[...])
    l_all[...] = jnp.where(here, l, l_all[...])

    @pl.when(j == nb - 1)
    def _():
        gate = g_all[...]
        sel = jnp.zeros((rows, LANE), F32)
        for _ in range(MOBA_TOPK):
            gm, idx = _first_argmax(gate, blane)
            pick = jnp.logical_and(blane == idx, gm > NEG_INF)
            sel = jnp.where(pick, 1.0, sel)
            gate = jnp.where(pick, NEG_INF, gate)
        chosen = sel > 0.0
        s_own = lax.dot_general(qb, kn_ref[...].astype(BF16), nt_dims, preferred_element_type=F32) * scale
        r_i = lax.broadcasted_iota(I32, (rows, ds), 0) % ds
        c_i = lax.broadcasted_iota(I32, (rows, ds), 1)
        s_own = jnp.where(c_i <= r_i, s_own, NEG_INF)
        m_sel = jnp.max(jnp.where(chosen, m_all[...], NEG_INF), axis=-1, keepdims=True)
        m_tot = jnp.maximum(m_sel, jnp.max(s_own, axis=-1, keepdims=True))
        wgt = jnp.where(chosen, jnp.exp(m_all[...] - m_tot), 0.0)
        p_own = jnp.exp(s_own - m_tot)
        l_tot = jnp.sum(wgt * l_all[...], axis=-1, keepdims=True) + jnp.sum(p_own, axis=-1, keepdims=True)
        acc = _head_diag(jnp.dot(p_own.astype(BF16), vn_ref[...].astype(BF16), preferred_element_type=F32),
                         n_heads, ds)
        for jj in range(nb):
            acc = acc + wgt[:, jj:jj + 1] * o_all[jj]
        o_ref[0] = acc / l_tot


def _moba_sample(qb, k_new, v_new, cache_k, cache_v, page_table, db, ds):
    n_pool, page, n_heads, hd = cache_k.shape
    d_att = n_heads * hd
    n_pages = page_table.shape[1]
    ppb = MOBA_BLOCK // page
    nb = n_pages // ppb
    rows = n_heads * ds
    assert ppb == 2 and n_pages % ppb == 0 and ds <= MOBA_BLOCK and nb <= LANE
    ck = cache_k.transpose(0, 2, 3, 1).reshape(n_pool, d_att, page)
    cv = cache_v.transpose(0, 2, 3, 1).reshape(n_pool, d_att, page)
    seq = pl.BlockSpec((ds, d_att), lambda i, j, pt: (i, 0))
    pg = lambda off: pl.BlockSpec((1, d_att, page), lambda i, j, pt: (pt[i * n_pages + ppb * j + off], 0, 0))
    out = pl.pallas_call(
        functools.partial(_moba_sample_kernel, ds=ds, n_heads=n_heads, nb=nb),
        out_shape=jax.ShapeDtypeStruct((db, rows, hd), F32),
        grid_spec=pltpu.PrefetchScalarGridSpec(
            num_scalar_prefetch=1,
            grid=(db, nb),
            in_specs=[seq, seq, seq, pg(0), pg(1), pg(0), pg(1)],
            out_specs=pl.BlockSpec((1, rows, hd), lambda i, j, pt: (i, 0, 0)),
            scratch_shapes=[
                pltpu.VMEM((rows, d_att), F32),
                pltpu.VMEM((rows, LANE), F32),
                pltpu.VMEM((rows, LANE), F32),
                pltpu.VMEM((rows, LANE), F32),
                pltpu.VMEM((nb, rows, hd), F32),
            ],
        ),
        compiler_params=pltpu.CompilerParams(
            dimension_semantics=("arbitrary", "arbitrary"), vmem_limit_bytes=VMEM_LIMIT),
        name="moba_sample",
    )(page_table.reshape(-1), qb, k_new, v_new, ck, ck, cv, cv)
    return out.reshape(db, n_heads, ds, hd).transpose(0, 2, 1, 3).reshape(db * ds, d_att)


def _trunk(x, pos, ctx, mk_arr, mv_arr, kv_cols, moba_fn, n_seq_kv, P, ts, tm):
    b, s, d = x.shape
    n = b * s
    x2d = x.reshape(n, d)
    x1, hf, gate, eid, new_conv = _layer0(x2d, b, s, ctx, mk_arr, mv_arr, kv_cols, 0, P, ts)
    yr, dest = _moe(hf, eid, 0, P["w1"], P["w3"], P["w2"])
    x2, kt, vt, qb, qm = _mid(x1, gate, yr, dest, pos, n_seq_kv, P, tm)
    c1 = moba_fn(qb, kt, vt)
    x3, hf, gate, eid = _layer1(x2, c1, qm, b, s, mk_arr, mv_arr, kv_cols, 1, P, ts)
    yr, dest = _moe(hf, eid, 1, P["w1"], P["w3"], P["w2"])
    y = _final(x3, gate, yr, dest, P["g_final"], tm)
    return y.reshape(b, s, d), new_conv[None], kt, vt


def kernel(x_prompt, x_sample, state_conv, cache_k, cache_v, cache_mem_k, cache_mem_v, page_table,
           mem_prompt, g_mix, g_ffn, g_final, g_mem, w_mem_kv, w_in_a, conv_w, conv_b, cln_g, cln_b,
           w_out_a, g_kv, w_kv, w_in_b, w_out_b, w_rg, b_rg, w_re, b_re, w1, w3, w2):
    depth = g_mix.shape[0]
    routers = [_router_weights(w_rg[l], b_rg[l], w_re[l], b_re[l]) for l in range(depth)]
    P = {
        "g_mix": g_mix, "g_ffn": g_ffn, "g_final": g_final, "g_kv": g_kv,
        "w_in_a_bf": w_in_a[0].astype(BF16), "w_out_a_bf": w_out_a[0].astype(BF16),
        "w_kv_t_bf": w_kv.T.astype(BF16), "w_in_b_bf": w_in_b[0].astype(BF16),
        "w_out_b_bf": w_out_b[0].astype(BF16),
        "conv_w": conv_w, "conv_b": conv_b, "cln_g": cln_g, "cln_b": cln_b,
        "wr": [r[0] for r in routers], "br": [r[1] for r in routers],
        "w1": w1, "w3": w3, "w2": w2,
    }
    b_p, s_p, _ = x_prompt.shape
    db, ds, _ = x_sample.shape
    mem_len = mem_prompt.shape[1]

    n_heads = w_kv.shape[1] // (2 * HEAD_DIM)
    past = page_table.shape[1] * cache_k.shape[1]
    assert past % MOBA_BLOCK == 0

    memkv_t = _memkv(mem_prompt, g_mem, w_mem_kv.transpose(0, 2, 1).astype(BF16))
    kv6 = memkv_t.reshape(depth, b_p, 2, MEM_H, HEAD_DIM, mem_len).transpose(2, 0, 1, 5, 3, 4)
    mem_k_p, mem_v_p = kv6[0], kv6[1]

    ctx_p = jnp.zeros((b_p, CONV_W - 1, state_conv.shape[-1]), F32)
    moba_p = lambda q, kt, vt: _moba_prompt(q, kt, vt, b_p, s_p)
    y_p, conv_p, kt_p, vt_p = _trunk(x_prompt, jnp.arange(s_p), ctx_p, memkv_t, memkv_t, (0, 1), moba_p, b_p, P,
                                     ts=256, tm=256)
    k_p = kt_p.reshape(b_p, n_heads, HEAD_DIM, s_p).transpose(0, 3, 1, 2)
    v_p = vt_p.reshape(b_p, n_heads, HEAD_DIM, s_p).transpose(0, 3, 1, 2)

    cmk = cache_mem_k.transpose(0, 1, 3, 4, 2).reshape(depth, db, D_MEMQ, mem_len)
    cmv = cache_mem_v.transpose(0, 1, 3, 4, 2).reshape(depth, db, D_MEMQ, mem_len)

    def moba_s(q, kt, vt):
        k_new, v_new = kt[0].T, vt[0].T
        return _moba_sample(q, k_new, v_new, cache_k, cache_v, page_table, db, ds)

    y_s, conv_s, kt_s, vt_s = _trunk(x_sample, past + jnp.arange(ds), state_conv[0], cmk, cmv, (0, 0), moba_s, 1, P,
                                     ts=ds, tm=db * ds)
    k_s = kt_s[0].T.reshape(db, ds, n_heads, HEAD_DIM)
    v_s = vt_s[0].T.reshape(db, ds, n_heads, HEAD_DIM)
    return (y_p, y_s, conv_p, conv_s, k_p, v_p, k_s, v_s, mem_k_p, mem_v_p)
```

```python
import functools

import jax
import jax.numpy as jnp
from jax import lax
from jax.experimental import pallas as pl
from jax.experimental.pallas import tpu as pltpu

F32 = jnp.float32
BF16 = jnp.bfloat16
I32 = jnp.int32
HIGHEST = lax.Precision.HIGHEST

EPS = 1e-6
HEAD_DIM = 64
MEM_H = 4
D_MEMQ = MEM_H * HEAD_DIM
CONV_W = 31
CTX_ROWS = 32
CTX_PAD = CTX_ROWS - (CONV_W - 1)
MOBA_BLOCK = 256
MOBA_TOPK = 3
ROT_DIM = HEAD_DIM // 4
ROPE_THETA = 500000.0
N_GROUPS = 4
E_PER_GROUP = 16
N_EXPERTS = N_GROUPS * E_PER_GROUP
MOE_BLK = 128
PLAN_TILE = 512
GATHER_UNROLL = 8
ROUTER_LANES = 128
LANE = 128
SUBLANE = 8
VMEM_LIMIT = 56 * 1024 * 1024

NEG_INF = float("-inf")
BIG_IDX = 1 << 20
NT_DIMS = (((1,), (1,)), ((), ()))


def _rms(x, g):
    return x * lax.rsqrt(jnp.mean(x * x, axis=-1, keepdims=True) + EPS) * g


def _first_argmax(vals, idx, axis):
    m = jnp.max(vals, axis=axis, keepdims=True)
    first = jnp.min(jnp.where(vals == m, idx, BIG_IDX), axis=axis, keepdims=True)
    return m, first


def _top_k_mask(gate, idx, axis):
    sel = jnp.zeros(gate.shape, F32)
    for _ in range(MOBA_TOPK):
        m, first = _first_argmax(gate, idx, axis)
        pick = jnp.logical_and(idx == first, m > NEG_INF)
        sel = jnp.where(pick, 1.0, sel)
        gate = jnp.where(pick, NEG_INF, gate)
    return sel


def _rot_t(xt_ref, cos_t, sin_t, n_heads):
    half = ROT_DIM // 2
    for hh in range(n_heads):
        r0 = hh * HEAD_DIM
        xa = xt_ref[0, r0:r0 + half, :]
        xb = xt_ref[0, r0 + half:r0 + ROT_DIM, :]
        xt_ref[0, r0:r0 + half, :] = xa * cos_t - xb * sin_t
        xt_ref[0, r0 + half:r0 + ROT_DIM, :] = xb * cos_t + xa * sin_t


def _memkv_kernel(mem_ref, g_ref, wt_ref, out_ref):
    h = _rms(mem_ref[0], g_ref[0]).astype(BF16)
    out_ref[0, 0] = lax.dot_general(wt_ref[0], h, NT_DIMS, preferred_element_type=F32)


def _memkv(mem, g_mem, w_mem_kv_t_bf):
    b, m, d = mem.shape
    depth = g_mem.shape[0]
    n_out = w_mem_kv_t_bf.shape[1]
    return pl.pallas_call(
        _memkv_kernel,
        out_shape=jax.ShapeDtypeStruct((depth, b, n_out, m), F32),
        grid=(depth, b),
        in_specs=[
            pl.BlockSpec((1, m, d), lambda l, i: (i, 0, 0)),
            pl.BlockSpec((1, 1, d), lambda l, i: (l, 0, 0)),
            pl.BlockSpec((1, n_out, d), lambda l, i: (l, 0, 0)),
        ],
        out_specs=pl.BlockSpec((1, 1, n_out, m), lambda l, i: (l, i, 0, 0)),
        compiler_params=pltpu.CompilerParams(dimension_semantics=("arbitrary", "arbitrary")),
        name="memkv",
    )(mem, g_mem.reshape(depth, 1, d), w_mem_kv_t_bf)


def _mem_attn_into(mix, off, qm, mkt_ref, mvt_ref):
    mkt = mkt_ref[0, 0].astype(BF16)
    mvt = mvt_ref[0, 0].astype(BF16)
    scale = HEAD_DIM ** -0.5
    lane = lax.broadcasted_iota(I32, qm.shape, 1)
    om = jnp.zeros(qm.shape, F32)
    for hh in range(MEM_H):
        in_head = (lane // HEAD_DIM) == hh
        q = jnp.where(in_head, qm, 0.0).astype(BF16)
        s = jnp.dot(q, mkt, preferred_element_type=F32) * scale
        m = jnp.max(s, axis=-1, keepdims=True)
        e = jnp.exp(s - m)
        l = jnp.sum(e, axis=-1, keepdims=True)
        o = lax.dot_general(e.astype(BF16), mvt, NT_DIMS, preferred_element_type=F32) / l
        om = jnp.where(in_head, o, om)
    mix[:, off:off + D_MEMQ] = om.astype(BF16)


def _out_and_route(x, mix, wout_ref, gffn_ref, wr_ref, br_ref, x1_ref, hf_ref, gate_ref, eid_ref):
    x1 = x + jnp.dot(mix[...], wout_ref[...], preferred_element_type=F32)
    x1_ref[...] = x1
    hf = _rms(x1, gffn_ref[...])
    hf_ref[...] = hf
    logits = jnp.dot(hf, wr_ref[...], preferred_element_type=F32, precision=HIGHEST) + br_ref[...]
    lane = lax.broadcasted_iota(I32, logits.shape, 1)
    gl = jnp.where((lane >= N_EXPERTS) & (lane < N_EXPERTS + N_GROUPS), logits, NEG_INF)
    gmax, glane = _first_argmax(gl, lane, -1)
    p_group = 1.0 / jnp.sum(jnp.exp(gl - gmax), axis=-1, keepdims=True)
    e_lo = (glane - N_EXPERTS) * E_PER_GROUP
    el = jnp.where((lane >= e_lo) & (lane < e_lo + E_PER_GROUP), logits, NEG_INF)
    m1, i1 = _first_argmax(el, lane, -1)
    el2 = jnp.where(lane == i1, NEG_INF, el)
    m2, i2 = _first_argmax(el2, lane, -1)
    e2 = jnp.exp(m2 - m1)
    g1 = p_group / (1.0 + e2)
    g2 = g1 * e2
    lane2 = lax.broadcasted_iota(I32, gate_ref.shape, 1)
    gate_ref[...] = jnp.where(lane2 == 0, g1, g2)
    eid_ref[...] = jnp.where(lane2 == 0, i1, i2)


def _router_weights(w_rg, b_rg, w_re, b_re):
    d = w_rg.shape[0]
    pad = ROUTER_LANES - N_EXPERTS - N_GROUPS
    wr = jnp.concatenate([w_re, w_rg, jnp.zeros((d, pad), F32)], axis=1)
    br = jnp.concatenate([b_re, b_rg, jnp.zeros((pad,), F32)]).reshape(1, ROUTER_LANES)
    return wr, br


def _layer0_kernel(x_ref, ctx_ref, mk_ref, mv_ref, gmix_ref, win_ref, cw_ref, cb_ref, lg_ref, lb_ref,
                   wout_ref, gffn_ref, wr_ref, br_ref,
                   x1_ref, hf_ref, gate_ref, eid_ref, nc_ref,
                   zext, zsh, cbuf, mix, *, ts, rc, d_conv):
    t = pl.program_id(1)
    nt = pl.num_programs(1)

    @pl.when(t == 0)
    def _():
        zext[0:CTX_PAD, :] = jnp.zeros((CTX_PAD, d_conv), F32)
        zext[CTX_PAD:CTX_ROWS, :] = ctx_ref[0]

    @pl.when(t > 0)
    def _():
        zext[0:CTX_ROWS, :] = zext[ts:ts + CTX_ROWS, :]

    x = x_ref[...]
    h = _rms(x, gmix_ref[...]).astype(BF16)
    u = jnp.dot(h, win_ref[...], preferred_element_type=F32)
    z = u[:, :d_conv] * jax.nn.sigmoid(u[:, d_conv:2 * d_conv])
    zext[CTX_ROWS:CTX_ROWS + ts, :] = z
    qm = u[:, 2 * d_conv:]

    @pl.when(t == nt - 1)
    def _():
        nc_ref[0] = zext[ts + CTX_PAD: ts + CTX_ROWS, :]

    for sh in range(1, SUBLANE):
        zsh[sh - 1] = zext[sh:sh + ts + CTX_ROWS - SUBLANE, :]

    def chunk(ci, carry):
        r0 = pl.multiple_of(ci * rc, SUBLANE)
        for cbk in range(d_conv // LANE):
            cs = slice(cbk * LANE, (cbk + 1) * LANE)
            acc = jnp.zeros((rc, LANE), F32) + cb_ref[:, cs]
            for w in range(CONV_W):
                hi, sh = divmod(w + CTX_PAD, SUBLANE)
                if sh == 0:
                    win = zext[pl.ds(r0 + hi * SUBLANE, rc), cs]
                else:
                    win = zsh[sh - 1, pl.ds(r0 + hi * SUBLANE, rc), cs]
                acc = acc + win * cw_ref[w:w + 1, cs]
            cbuf[pl.ds(r0, rc), cs] = acc
        return carry

    lax.fori_loop(0, ts // rc, chunk, 0)

    c = cbuf[...]
    mu = jnp.mean(c, axis=-1, keepdims=True)
    xc = c - mu
    y = xc * lax.rsqrt(jnp.mean(xc * xc, axis=-1, keepdims=True) + EPS) * lg_ref[...] + lb_ref[...]
    mix[:, :d_conv] = (y * jax.nn.sigmoid(y)).astype(BF16)
    _mem_attn_into(mix, d_conv, qm, mk_ref, mv_ref)
    _out_and_route(x, mix, wout_ref, gffn_ref, wr_ref, br_ref, x1_ref, hf_ref, gate_ref, eid_ref)


def _layer0(x2d, b, s, ctx, mk_arr, mv_arr, kv_rows, l, P, ts):
    n, d = x2d.shape
    nt = s // ts
    d_conv = ctx.shape[-1]
    rc = min(32, ts)
    d_in = P["w_in_a_bf"].shape[-1]
    krow, vrow = kv_rows
    const = lambda *shape: pl.BlockSpec(shape, lambda i, t: (0,) * len(shape))
    tok = lambda w: pl.BlockSpec((ts, w), lambda i, t: (i * nt + t, 0))
    kern = functools.partial(_layer0_kernel, ts=ts, rc=rc, d_conv=d_conv)
    return pl.pallas_call(
        kern,
        out_shape=(
            jax.ShapeDtypeStruct((n, d), F32),
            jax.ShapeDtypeStruct((n, d), F32),
            jax.ShapeDtypeStruct((n, 2), F32),
            jax.ShapeDtypeStruct((n, 2), I32),
            jax.ShapeDtypeStruct((b, CONV_W - 1, d_conv), F32),
        ),
        grid=(b, nt),
        in_specs=[
            tok(d),
            pl.BlockSpec((1, CONV_W - 1, d_conv), lambda i, t: (i, 0, 0)),
            pl.BlockSpec((1, 1, D_MEMQ, mk_arr.shape[3]), lambda i, t: (l, i, krow, 0)),
            pl.BlockSpec((1, 1, D_MEMQ, mv_arr.shape[3]), lambda i, t: (l, i, vrow, 0)),
            const(1, d), const(d, d_in), const(CONV_W, d_conv), const(1, d_conv), const(1, d_conv),
            const(1, d_conv), const(d_conv + D_MEMQ, d), const(1, d), const(d, ROUTER_LANES),
            const(1, ROUTER_LANES),
        ],
        out_specs=(tok(d), tok(d), tok(2), tok(2),
                   pl.BlockSpec((1, CONV_W - 1, d_conv), lambda i, t: (i, 0, 0))),
        scratch_shapes=[
            pltpu.VMEM((ts + CTX_ROWS, d_conv), F32),
            pltpu.VMEM((SUBLANE - 1, ts + CTX_ROWS - SUBLANE, d_conv), F32),
            pltpu.VMEM((ts, d_conv), F32),
            pltpu.VMEM((ts, d_conv + D_MEMQ), BF16),
        ],
        compiler_params=pltpu.CompilerParams(
            dimension_semantics=("arbitrary", "arbitrary"), vmem_limit_bytes=VMEM_LIMIT),
        name="layer0",
    )(x2d, ctx, mk_arr, mv_arr, P["g_mix"][l].reshape(1, d), P["w_in_a_bf"], P["conv_w"][0],
      P["conv_b"][0].reshape(1, d_conv), P["cln_g"][0].reshape(1, d_conv), P["cln_b"][0].reshape(1, d_conv),
      P["w_out_a_bf"], P["g_ffn"][l].reshape(1, d), P["wr"][l], P["br"][l])


def _layer1_kernel(x_ref, c_ref, qm_ref, mk_ref, mv_ref, wout_ref, gffn_ref, wr_ref, br_ref,
                   x1_ref, hf_ref, gate_ref, eid_ref, mix, *, d_att):
    mix[:, :d_att] = c_ref[...].astype(BF16)
    _mem_attn_into(mix, d_att, qm_ref[...], mk_ref, mv_ref)
    _out_and_route(x_ref[...], mix, wout_ref, gffn_ref, wr_ref, br_ref, x1_ref, hf_ref, gate_ref, eid_ref)


def _layer1(x2d, c2d, qm2d, b, s, mk_arr, mv_arr, kv_rows, l, P, ts):
    n, d = x2d.shape
    nt = s // ts
    d_att = c2d.shape[-1]
    krow, vrow = kv_rows
    const = lambda *shape: pl.BlockSpec(shape, lambda i, t: (0,) * len(shape))
    tok = lambda w: pl.BlockSpec((ts, w), lambda i, t: (i * nt + t, 0))
    return pl.pallas_call(
        functools.partial(_layer1_kernel, d_att=d_att),
        out_shape=(
            jax.ShapeDtypeStruct((n, d), F32),
            jax.ShapeDtypeStruct((n, d), F32),
            jax.ShapeDtypeStruct((n, 2), F32),
            jax.ShapeDtypeStruct((n, 2), I32),
        ),
        grid=(b, nt),
        in_specs=[
            tok(d), tok(d_att), tok(D_MEMQ),
            pl.BlockSpec((1, 1, D_MEMQ, mk_arr.shape[3]), lambda i, t: (l, i, krow, 0)),
            pl.BlockSpec((1, 1, D_MEMQ, mv_arr.shape[3]), lambda i, t: (l, i, vrow, 0)),
            const(d_att + D_MEMQ, d), const(1, d), const(d, ROUTER_LANES), const(1, ROUTER_LANES),
        ],
        out_specs=(tok(d), tok(d), tok(2), tok(2)),
        scratch_shapes=[pltpu.VMEM((ts, d_att + D_MEMQ), BF16)],
        compiler_params=pltpu.CompilerParams(
            dimension_semantics=("arbitrary", "arbitrary"), vmem_limit_bytes=VMEM_LIMIT),
        name="layer1",
    )(x2d, c2d, qm2d, mk_arr, mv_arr, P["w_out_b_bf"], P["g_ffn"][l].reshape(1, d), P["wr"][l], P["br"][l])


def _plan_kernel(eid_ref, dest_ref, counts_ref, cnt, base, pstart, *, ta):
    ph = pl.program_id(0)
    t = pl.program_id(1)
    lane = lax.broadcasted_iota(I32, (ta, LANE), 1)
    onehot = eid_ref[...] == lane
    ohf = jnp.where(onehot, 1.0, 0.0)
    tile_counts = jnp.sum(ohf, axis=0, keepdims=True)

    @pl.when(ph == 0)
    def _():
        @pl.when(t == 0)
        def _():
            cnt[...] = jnp.zeros((1, LANE), F32)
        cnt[...] = cnt[...] + tile_counts

    @pl.when(ph == 1)
    def _():
        @pl.when(t == 0)
        def _():
            ci = cnt[...].astype(I32)
            counts_ref[...] = ci
            pc = ((ci + (MOE_BLK - 1)) // MOE_BLK) * MOE_BLK
            lane1 = lax.broadcasted_iota(I32, (1, LANE), 1)
            incl = pc
            sh = 1
            while sh < LANE:
                incl = incl + jnp.where(lane1 >= sh, pltpu.roll(incl, sh, 1), 0)
                sh *= 2
            pstart[...] = (incl - pc).astype(F32)
            base[...] = jnp.zeros((1, LANE), F32)

        r = lax.broadcasted_iota(I32, (ta, ta), 0)
        c = lax.broadcasted_iota(I32, (ta, ta), 1)
        tri = jnp.where(r >= c, 1.0, 0.0).astype(BF16)
        prefix = jnp.dot(tri, ohf.astype(BF16), preferred_element_type=F32)
        slot = pstart[...] + base[...] + prefix - 1.0
        dest_ref[...] = jnp.sum(jnp.where(onehot, slot, 0.0), axis=1, keepdims=True).astype(I32)
        base[...] = base[...] + tile_counts


def _plan(eid_col):
    a = eid_col.shape[0]
    ta = PLAN_TILE
    assert a % ta == 0
    return pl.pallas_call(
        functools.partial(_plan_kernel, ta=ta),
        out_shape=(jax.ShapeDtypeStruct((a, 1), I32), jax.ShapeDtypeStruct((1, LANE), I32)),
        grid=(2, a // ta),
        in_specs=[pl.BlockSpec((ta, 1), lambda ph, t: (t, 0))],
        out_specs=(pl.BlockSpec((ta, 1), lambda ph, t: (t * ph, 0)),
                   pl.BlockSpec((1, LANE), lambda ph, t: (0, 0))),
        scratch_shapes=[pltpu.VMEM((1, LANE), F32), pltpu.VMEM((1, LANE), F32), pltpu.VMEM((1, LANE), F32)],
        compiler_params=pltpu.CompilerParams(dimension_semantics=("arbitrary", "arbitrary")),
        name="moe_plan",
    )(eid_col)


def _dest_tiles(dest, tm):
    n = dest.shape[0]
    return dest.reshape(n // tm, tm, 2).transpose(0, 2, 1).reshape(n // tm, 2 * tm)


def _dispatch_kernel(dest_hbm, hf_ref, xs_in, xs_out, idx_smem, sem_idx, sem_rows, *, tm):
    del xs_in
    i = pl.program_id(0)
    n = pl.num_programs(0)
    slot = lax.rem(i, 2)

    def idx_copy(k, sl):
        return pltpu.make_async_copy(dest_hbm.at[k], idx_smem.at[sl], sem_idx.at[sl])

    @pl.when(i == 0)
    def _():
        idx_copy(0, 0).start()

    @pl.when(i + 1 < n)
    def _():
        idx_copy(i + 1, 1 - slot).start()

    idx_copy(i, slot).wait()

    def body(g, carry):
        for u in range(GATHER_UNROLL):
            r = g * GATHER_UNROLL + u
            for k in range(2):
                row = idx_smem[slot, k * tm + r]
                pltpu.make_async_copy(hf_ref.at[pl.ds(r, 1)], xs_out.at[pl.ds(row, 1)], sem_rows).start()
        return carry
    lax.fori_loop(0, tm // GATHER_UNROLL, body, 0)
    for _ in range(2):
        pltpu.make_async_copy(hf_ref, xs_out.at[pl.ds(0, tm)], sem_rows).wait()


def _dispatch(hf, dest, xs, tm):
    n, d = hf.shape
    return pl.pallas_call(
        functools.partial(_dispatch_kernel, tm=tm),
        out_shape=jax.ShapeDtypeStruct(xs.shape, xs.dtype),
        grid=(n // tm,),
        in_specs=[
            pl.BlockSpec(memory_space=pl.ANY),
            pl.BlockSpec((tm, d), lambda i: (i, 0)),
            pl.BlockSpec(memory_space=pl.ANY),
        ],
        out_specs=pl.BlockSpec(memory_space=pl.ANY),
        scratch_shapes=[
            pltpu.SMEM((2, 2 * tm), I32),
            pltpu.SemaphoreType.DMA((2,)),
            pltpu.SemaphoreType.DMA,
        ],
        input_output_aliases={2: 0},
        compiler_params=pltpu.CompilerParams(
            dimension_semantics=("arbitrary",), vmem_limit_bytes=VMEM_LIMIT),
        name="moe_dispatch",
    )(_dest_tiles(dest, tm), hf, xs)


def _experts_kernel(bexp_ref, nused_ref, xs_ref, w1_ref, w3_ref, w2_ref, out_ref, w13, w2s, *, d_exp):
    i = pl.program_id(0)
    nused = nused_ref[0]

    @pl.when(i < nused)
    def _():
        changed = jnp.logical_or(i == 0, bexp_ref[i] != bexp_ref[jnp.maximum(i - 1, 0)])

        @pl.when(changed)
        def _():
            w13[:, :d_exp] = w1_ref[0, 0].astype(BF16)
            w13[:, d_exp:] = w3_ref[0, 0].astype(BF16)
            w2s[...] = w2_ref[0, 0].astype(BF16)

        hcat = jnp.dot(xs_ref[...].astype(BF16), w13[...], preferred_element_type=F32)
        a = hcat[:, :d_exp]
        act = (a * jax.nn.sigmoid(a)) * hcat[:, d_exp:]
        out_ref[...] = jnp.dot(act.astype(BF16), w2s[...], preferred_element_type=F32)

    @pl.when(i >= nused)
    def _():
        out_ref[...] = jnp.zeros(out_ref.shape, F32)


def _experts(xs, blk_exp, nused, l, w1, w3, w2):
    p_rows, d = xs.shape
    d_exp = w1.shape[-1]
    n_blk = p_rows // MOE_BLK
    wspec = lambda r, c: pl.BlockSpec((1, 1, r, c), lambda i, be, nu: (l, be[jnp.minimum(i, nu[0] - 1)], 0, 0))
    return pl.pallas_call(
        functools.partial(_experts_kernel, d_exp=d_exp),
        out_shape=jax.ShapeDtypeStruct((p_rows, d), F32),
        grid_spec=pltpu.PrefetchScalarGridSpec(
            num_scalar_prefetch=2,
            grid=(n_blk,),
            in_specs=[
                pl.BlockSpec((MOE_BLK, d), lambda i, be, nu: (jnp.minimum(i, nu[0] - 1), 0)),
                wspec(d, d_exp), wspec(d, d_exp), wspec(d_exp, d),
            ],
            out_specs=pl.BlockSpec((MOE_BLK, d), lambda i, be, nu: (i, 0)),
            scratch_shapes=[
                pltpu.VMEM((d, 2 * d_exp), BF16),
                pltpu.VMEM((d_exp, d), BF16),
            ],
        ),
        compiler_params=pltpu.CompilerParams(
            dimension_semantics=("arbitrary",), vmem_limit_bytes=VMEM_LIMIT),
        name="moe_experts",
    )(blk_exp, nused, xs, w1, w3, w2)


def _moe_layer(l, groups, P):
    d = groups[0][0].shape[1]
    eid_all = jnp.concatenate([g[1] for g in groups], axis=0)
    n_all = eid_all.shape[0]
    a = 2 * n_all
    assert a % MOE_BLK == 0
    n_blk = a // MOE_BLK + N_EXPERTS
    dest_col, counts = _plan(eid_all.reshape(a, 1))
    counts = counts[0, :N_EXPERTS]
    pends = jnp.cumsum((counts + MOE_BLK - 1) // MOE_BLK * MOE_BLK)
    blk_start = jnp.arange(n_blk, dtype=I32) * MOE_BLK
    blk_exp = jnp.minimum(jnp.sum((pends[None, :] <= blk_start[:, None]).astype(I32), axis=1), N_EXPERTS - 1)
    nused = (pends[-1] // MOE_BLK).astype(I32).reshape(1)
    dest = dest_col.reshape(n_all, 2)
    xs = jnp.zeros((n_blk * MOE_BLK, d), F32)
    dests = []
    off = 0
    for hf, _, tm in groups:
        n = hf.shape[0]
        dests.append(dest[off:off + n])
        xs = _dispatch(hf, dests[-1], xs, tm)
        off += n
    yr = _experts(xs, blk_exp.astype(I32), nused, l, P["w1"], P["w3"], P["w2"])
    return yr, dests


def _gather_step(step, nsteps, idx_hbm, src_hbm, idx_smem, buf, sem_idx, sem_rows, nrows):
    slot = lax.rem(step, 2)
    nslot = 1 - slot

    def idx_copy(k, sl):
        return pltpu.make_async_copy(idx_hbm.at[k], idx_smem.at[sl], sem_idx.at[sl])

    def issue(sl):
        def body(g, carry):
            for u in range(GATHER_UNROLL):
                r = g * GATHER_UNROLL + u
                pltpu.make_async_copy(src_hbm.at[pl.ds(idx_smem[sl, r], 1)], buf.at[sl, pl.ds(r, 1)],
                                      sem_rows.at[sl]).start()
            return carry
        lax.fori_loop(0, nrows // GATHER_UNROLL, body, 0)

    @pl.when(step == 0)
    def _():
        first = idx_copy(0, 0)
        first.start()
        first.wait()
        issue(0)

        @pl.when(nsteps > 1)
        def _():
            idx_copy(1, 1).start()

    @pl.when(step + 1 < nsteps)
    def _():
        idx_copy(step + 1, nslot).wait()
        issue(nslot)

        @pl.when(step + 2 < nsteps)
        def _():
            idx_copy(step + 2, slot).start()

    pltpu.make_async_copy(src_hbm.at[pl.ds(0, nrows)], buf.at[slot], sem_rows.at[slot]).wait()
    return slot


def _combine(step, nsteps, dest_hbm, yr_hbm, x1_ref, gate_ref, idx_smem, ybuf, sem_idx, sem_rows, tm):
    slot = _gather_step(step, nsteps, dest_hbm, yr_hbm, idx_smem, ybuf, sem_idx, sem_rows, 2 * tm)
    g = gate_ref[...]
    y = g[:, 0:1] * ybuf[slot, 0:tm, :] + g[:, 1:2] * ybuf[slot, tm:2 * tm, :]
    return x1_ref[...] + y


def _mid_kernel(dest_hbm, yr_hbm, x1_ref, gate_ref, cos_ref, sin_ref,
                gkv_ref, wkvt_ref, gmix_ref, wqt_ref, wqm_ref,
                x2_ref, kt_ref, vt_ref, qt_ref, qm_ref,
                idx_smem, ybuf, sem_idx, sem_rows, *, tm, d_att):
    i = pl.program_id(0)
    x2 = _combine(i, pl.num_programs(0), dest_hbm, yr_hbm, x1_ref, gate_ref,
                  idx_smem, ybuf, sem_idx, sem_rows, tm)
    x2_ref[...] = x2
    n_heads = d_att // HEAD_DIM
    cos_t = cos_ref[...]
    sin_t = sin_ref[...]
    hk = _rms(x2, gkv_ref[...]).astype(BF16)
    kvt = lax.dot_general(wkvt_ref[...], hk, NT_DIMS, preferred_element_type=F32)
    vt_ref[0] = kvt[d_att:, :]
    kt_ref[0] = kvt[:d_att, :]
    _rot_t(kt_ref, cos_t, sin_t, n_heads)
    hq = _rms(x2, gmix_ref[...]).astype(BF16)
    qt_ref[0] = lax.dot_general(wqt_ref[...], hq, NT_DIMS, preferred_element_type=F32)
    _rot_t(qt_ref, cos_t, sin_t, n_heads)
    qm_ref[...] = jnp.dot(hq, wqm_ref[...], preferred_element_type=F32)


def _mid(x1, gate, yr, dest, pos, n_seq, P, tm):
    n, d = x1.shape
    d_att = P["w_kv_t_bf"].shape[0] // 2
    nsteps = n // tm
    s_out = n // n_seq
    nt = s_out // tm
    half = ROT_DIM // 2
    reps = s_out // pos.shape[0]
    inv = ROPE_THETA ** (-jnp.arange(half, dtype=F32) * (2.0 / ROT_DIM))
    ang_t = jnp.tile(inv[:, None] * pos.astype(F32)[None, :], (1, reps))
    const = lambda *shape: pl.BlockSpec(shape, lambda i: (0,) * len(shape))
    tok = lambda w: pl.BlockSpec((tm, w), lambda i: (i, 0))
    tbl_t = pl.BlockSpec((half, tm), lambda i: (0, i % nt))
    feat_t = pl.BlockSpec((1, d_att, tm), lambda i: (i // nt, 0, i % nt))
    feat_shape = jax.ShapeDtypeStruct((n_seq, d_att, s_out), F32)
    return pl.pallas_call(
        functools.partial(_mid_kernel, tm=tm, d_att=d_att),
        out_shape=(jax.ShapeDtypeStruct((n, d), F32), feat_shape, feat_shape, feat_shape,
                   jax.ShapeDtypeStruct((n, D_MEMQ), F32)),
        grid=(nsteps,),
        in_specs=[
            pl.BlockSpec(memory_space=pl.ANY), pl.BlockSpec(memory_space=pl.ANY),
            tok(d), tok(2), tbl_t, tbl_t,
            const(1, d), const(2 * d_att, d), const(1, d), const(d_att, d), const(d, D_MEMQ),
        ],
        out_specs=(tok(d), feat_t, feat_t, feat_t, tok(D_MEMQ)),
        scratch_shapes=[
            pltpu.SMEM((2, 2 * tm), I32),
            pltpu.VMEM((2, 2 * tm, d), F32),
            pltpu.SemaphoreType.DMA((2,)),
            pltpu.SemaphoreType.DMA((2,)),
        ],
        compiler_params=pltpu.CompilerParams(
            dimension_semantics=("arbitrary",), vmem_limit_bytes=VMEM_LIMIT),
        name="mid",
    )(_dest_tiles(dest, tm), yr, x1, gate, jnp.cos(ang_t), jnp.sin(ang_t),
      P["g_kv"].reshape(1, d), P["w_kv_t_bf"], P["g_mix"][1].reshape(1, d), P["w_q_t_bf"], P["w_qm_bf"])


def _final_kernel(dest_hbm, yr_hbm, x1_ref, gate_ref, g_ref, y_ref, idx_smem, ybuf, sem_idx, sem_rows, *, tm):
    i = pl.program_id(0)
    x = _combine(i, pl.num_programs(0), dest_hbm, yr_hbm, x1_ref, gate_ref,
                 idx_smem, ybuf, sem_idx, sem_rows, tm)
    y_ref[...] = _rms(x, g_ref[...])


def _final(x1, gate, yr, dest, g_final, tm):
    n, d = x1.shape
    tok = lambda w: pl.BlockSpec((tm, w), lambda i: (i, 0))
    return pl.pallas_call(
        functools.partial(_final_kernel, tm=tm),
        out_shape=jax.ShapeDtypeStruct((n, d), F32),
        grid=(n // tm,),
        in_specs=[
            pl.BlockSpec(memory_space=pl.ANY), pl.BlockSpec(memory_space=pl.ANY),
            tok(d), tok(2), pl.BlockSpec((1, d), lambda i: (0, 0)),
        ],
        out_specs=tok(d),
        scratch_shapes=[
            pltpu.SMEM((2, 2 * tm), I32),
            pltpu.VMEM((2, 2 * tm, d), F32),
            pltpu.SemaphoreType.DMA((2,)),
            pltpu.SemaphoreType.DMA((2,)),
        ],
        compiler_params=pltpu.CompilerParams(
            dimension_semantics=("arbitrary",), vmem_limit_bytes=VMEM_LIMIT),
        name="final",
    )(_dest_tiles(dest, tm), yr, x1, gate, g_final.reshape(1, d))


def _moba_prompt_kernel(qt_ref, kt_ref, vt_ref, o_ref, kmean, krows, vones, sel_sc, *, s_len):
    qb_i = pl.program_id(2)
    nb = s_len // MOBA_BLOCK
    n_h = LANE // HEAD_DIM
    tq = MOBA_BLOCK
    scale = HEAD_DIM ** -0.5
    feat = lax.broadcasted_iota(I32, (LANE, MOBA_BLOCK), 0) // HEAD_DIM

    @pl.when(qb_i == 0)
    def _():
        kmean[...] = jnp.zeros(kmean.shape, F32)
        for jb in range(nb):
            cols = slice(jb * MOBA_BLOCK, (jb + 1) * MOBA_BLOCK)
            kr = kt_ref[0, :, cols].T
            kmean[jb:jb + 1, :] = jnp.sum(kr, axis=0, keepdims=True) * (1.0 / MOBA_BLOCK)
            krows[jb] = kr.astype(BF16)
            vblk = vt_ref[0, :, cols]
            for hh in range(n_h):
                vones[hh, jb] = jnp.where(feat == hh, vblk, 1.0).astype(BF16)

    qt = qt_ref[0]
    qstack = jnp.concatenate([jnp.where(feat == hh, qt, 0.0) for hh in range(n_h)], axis=1)
    gate = jnp.dot(kmean[...], qstack, preferred_element_type=F32, precision=HIGHEST)
    brow = lax.broadcasted_iota(I32, gate.shape, 0)
    gate = jnp.where(brow < qb_i, gate, NEG_INF)
    sel = _top_k_mask(gate, brow, 0)
    for jb in range(nb):
        sel_sc[jb] = sel[jb:jb + 1, :]
    qs = (qstack * scale).astype(BF16)

    key = lax.broadcasted_iota(I32, (MOBA_BLOCK, n_h * tq), 0)
    qcol = lax.broadcasted_iota(I32, (MOBA_BLOCK, n_h * tq), 1) % tq
    s = jnp.dot(krows[qb_i], qs, preferred_element_type=F32)
    s = jnp.where(key <= qcol, s, NEG_INF)
    m0 = jnp.max(s, axis=0, keepdims=True)
    p = jnp.exp(s - m0).astype(BF16)
    acc0 = tuple(jnp.dot(vones[hh, qb_i], p[:, hh * tq:(hh + 1) * tq], preferred_element_type=F32)
                 for hh in range(n_h))

    def body(j, carry):
        m, accs = carry
        sj = jnp.dot(krows[j], qs, preferred_element_type=F32)
        sj = jnp.where(sel_sc[j] > 0.0, sj, NEG_INF)
        m_new = jnp.maximum(m, jnp.max(sj, axis=0, keepdims=True))
        alpha = jnp.exp(m - m_new)
        pj = jnp.exp(sj - m_new).astype(BF16)
        accs = tuple(alpha[:, hh * tq:(hh + 1) * tq] * accs[hh]
                     + jnp.dot(vones[hh, j], pj[:, hh * tq:(hh + 1) * tq], preferred_element_type=F32)
                     for hh in range(n_h))
        return m_new, accs

    _, accs = lax.fori_loop(0, qb_i, body, (m0, acc0))
    out_t = jnp.zeros((LANE, tq), F32)
    for hh in range(n_h):
        other = (1 - hh) * HEAD_DIM
        out_t = jnp.where(feat == hh, accs[hh] / accs[hh][other:other + 1, :], out_t)
    o_ref[...] = out_t.T


def _moba_prompt(qt, kt, vt, b, s):
    d_att = qt.shape[1]
    nb = s // MOBA_BLOCK
    assert s % MOBA_BLOCK == 0 and LANE // HEAD_DIM == 2
    nb_pad = -(-nb // SUBLANE) * SUBLANE
    kv_spec = pl.BlockSpec((1, LANE, s), lambda i, hp, qb: (i, hp, 0))
    return pl.pallas_call(
        functools.partial(_moba_prompt_kernel, s_len=s),
        out_shape=jax.ShapeDtypeStruct((b * s, d_att), F32),
        grid=(b, d_att // LANE, nb),
        in_specs=[
            pl.BlockSpec((1, LANE, MOBA_BLOCK), lambda i, hp, qb: (i, hp, qb)),
            kv_spec, kv_spec,
        ],
        out_specs=pl.BlockSpec((MOBA_BLOCK, LANE), lambda i, hp, qb: (i * nb + qb, hp)),
        scratch_shapes=[
            pltpu.VMEM((nb_pad, LANE), F32),
            pltpu.VMEM((nb, MOBA_BLOCK, LANE), BF16),
            pltpu.VMEM((2, nb, LANE, MOBA_BLOCK), BF16),
            pltpu.VMEM((nb, 1, 2 * MOBA_BLOCK), F32),
        ],
        compiler_params=pltpu.CompilerParams(
            dimension_semantics=("arbitrary", "arbitrary", "arbitrary"), vmem_limit_bytes=VMEM_LIMIT),
        name="moba_prompt",
    )(qt, kt, vt)


def _head_diag(o, n_heads, ds):
    return jnp.concatenate(
        [o[h * ds:(h + 1) * ds, h * HEAD_DIM:(h + 1) * HEAD_DIM] for h in range(n_heads)], axis=0)


def _moba_sample_kernel(pt_ref, q_ref, kn_ref, vn_ref, *rest, ds, n_heads, nb, bps):
    pages = rest[:4 * bps]
    o_ref, qbd, g_all, m_all, l_all, o_all = rest[4 * bps:]
    jg = pl.program_id(1)
    rows = n_heads * ds
    d_att = n_heads * HEAD_DIM
    scale = HEAD_DIM ** -0.5
    blane = lax.broadcasted_iota(I32, (rows, LANE), 1)

    @pl.when(jg == 0)
    def _():
        q = q_ref[...]
        lane = lax.broadcasted_iota(I32, (ds, d_att), 1)
        for h in range(n_heads):
            qbd[h * ds:(h + 1) * ds, :] = jnp.where(lane // HEAD_DIM == h, q, 0.0)
        g_all[...] = jnp.full((rows, LANE), NEG_INF, F32)
        m_all[...] = jnp.zeros((rows, LANE), F32)
        l_all[...] = jnp.zeros((rows, LANE), F32)

    qb = qbd[...].astype(BF16)
    g_new, m_new, l_new = g_all[...], m_all[...], l_all[...]
    for bi in range(bps):
        k0_ref, k1_ref = pages[2 * bi], pages[2 * bi + 1]
        v0_ref, v1_ref = pages[2 * bps + 2 * bi], pages[2 * bps + 2 * bi + 1]
        j = jg * bps + bi
        s0 = jnp.dot(qb, k0_ref[0].astype(BF16), preferred_element_type=F32)
        s1 = jnp.dot(qb, k1_ref[0].astype(BF16), preferred_element_type=F32)
        g = (jnp.sum(s0, axis=-1, keepdims=True) + jnp.sum(s1, axis=-1, keepdims=True)) * (1.0 / MOBA_BLOCK)
        s0 = s0 * scale
        s1 = s1 * scale
        m = jnp.maximum(jnp.max(s0, axis=-1, keepdims=True), jnp.max(s1, axis=-1, keepdims=True))
        p0 = jnp.exp(s0 - m)
        p1 = jnp.exp(s1 - m)
        l = jnp.sum(p0, axis=-1, keepdims=True) + jnp.sum(p1, axis=-1, keepdims=True)
        o = (lax.dot_general(p0.astype(BF16), v0_ref[0].astype(BF16), NT_DIMS, preferred_element_type=F32)
             + lax.dot_general(p1.astype(BF16), v1_ref[0].astype(BF16), NT_DIMS, preferred_element_type=F32))
        o_all[j] = _head_diag(o, n_heads, ds)
        here = blane == j
        g_new = jnp.where(here, g, g_new)
        m_new = jnp.where(here, m, m_new)
        l_new = jnp.where(here, l, l_new)
    g_all[...] = g_new
    m_all[...] = m_new
    l_all[...] = l_new

    @pl.when(jg == nb // bps - 1)
    def _():
        chosen = _top_k_mask(g_new, blane, -1) > 0.0
        s_own = lax.dot_general(qb, kn_ref[...].astype(BF16), NT_DIMS, preferred_element_type=F32) * scale
        r_i = lax.broadcasted_iota(I32, (rows, ds), 0) % ds
        c_i = lax.broadcasted_iota(I32, (rows, ds), 1)
        s_own = jnp.where(c_i <= r_i, s_own, NEG_INF)
        m_sel = jnp.max(jnp.where(chosen, m_new, NEG_INF), axis=-1, keepdims=True)
        m_tot = jnp.maximum(m_sel, jnp.max(s_own, axis=-1, keepdims=True))
        wgt = jnp.where(chosen, jnp.exp(m_new - m_tot), 0.0)
        p_own = jnp.exp(s_own - m_tot)
        l_tot = jnp.sum(wgt * l_new, axis=-1, keepdims=True) + jnp.sum(p_own, axis=-1, keepdims=True)
        acc = _head_diag(jnp.dot(p_own.astype(BF16), vn_ref[...].astype(BF16), preferred_element_type=F32),
                         n_heads, ds)
        for jj in range(nb):
            acc = acc + wgt[:, jj:jj + 1] * o_all[jj]
        o_ref[0] = acc / l_tot


def _moba_sample(qb, k_new, v_new, cache_k, cache_v, page_table, db, ds, bps=2):
    n_pool, page, n_heads, hd = cache_k.shape
    d_att = n_heads * hd
    n_pages = page_table.shape[1]
    ppb = MOBA_BLOCK // page
    nb = n_pages // ppb
    rows = n_heads * ds
    assert ppb == 2 and n_pages % ppb == 0 and ds <= MOBA_BLOCK and nb <= LANE and nb % bps == 0
    ck = cache_k.transpose(0, 2, 3, 1).reshape(n_pool, d_att, page)
    cv = cache_v.transpose(0, 2, 3, 1).reshape(n_pool, d_att, page)
    seq = pl.BlockSpec((ds, d_att), lambda i, j, pt: (i, 0))
    ppg = ppb * bps
    pg = lambda off: pl.BlockSpec((1, d_att, page), lambda i, j, pt: (pt[i * n_pages + ppg * j + off], 0, 0))
    out = pl.pallas_call(
        functools.partial(_moba_sample_kernel, ds=ds, n_heads=n_heads, nb=nb, bps=bps),
        out_shape=jax.ShapeDtypeStruct((db, rows, hd), F32),
        grid_spec=pltpu.PrefetchScalarGridSpec(
            num_scalar_prefetch=1,
            grid=(db, nb // bps),
            in_specs=[seq, seq, seq] + [pg(o) for o in range(ppg)] * 2,
            out_specs=pl.BlockSpec((1, rows, hd), lambda i, j, pt: (i, 0, 0)),
            scratch_shapes=[
                pltpu.VMEM((rows, d_att), F32),
                pltpu.VMEM((rows, LANE), F32),
                pltpu.VMEM((rows, LANE), F32),
                pltpu.VMEM((rows, LANE), F32),
                pltpu.VMEM((nb, rows, hd), F32),
            ],
        ),
        compiler_params=pltpu.CompilerParams(
            dimension_semantics=("arbitrary", "arbitrary"), vmem_limit_bytes=VMEM_LIMIT),
        name="moba_sample",
    )(page_table.reshape(-1), qb, k_new, v_new, *([ck] * ppg), *([cv] * ppg))
    return out.reshape(db, n_heads, ds, hd).transpose(0, 2, 1, 3).reshape(db * ds, d_att)


def kernel(x_prompt, x_sample, state_conv, cache_k, cache_v, cache_mem_k, cache_mem_v, page_table,
           mem_prompt, g_mix, g_ffn, g_final, g_mem, w_mem_kv, w_in_a, conv_w, conv_b, cln_g, cln_b,
           w_out_a, g_kv, w_kv, w_in_b, w_out_b, w_rg, b_rg, w_re, b_re, w1, w3, w2):
    depth = g_mix.shape[0]
    n_heads = w_kv.shape[1] // (2 * HEAD_DIM)
    d_att = n_heads * HEAD_DIM
    routers = [_router_weights(w_rg[l], b_rg[l], w_re[l], b_re[l]) for l in range(depth)]
    P = {
        "g_mix": g_mix, "g_ffn": g_ffn, "g_final": g_final, "g_kv": g_kv,
        "w_in_a_bf": w_in_a[0].astype(BF16), "w_out_a_bf": w_out_a[0].astype(BF16),
        "w_kv_t_bf": w_kv.T.astype(BF16),
        "w_q_t_bf": w_in_b[0][:, :d_att].T.astype(BF16), "w_qm_bf": w_in_b[0][:, d_att:].astype(BF16),
        "w_out_b_bf": w_out_b[0].astype(BF16),
        "conv_w": conv_w, "conv_b": conv_b, "cln_g": cln_g, "cln_b": cln_b,
        "wr": [r[0] for r in routers], "br": [r[1] for r in routers],
        "w1": w1, "w3": w3, "w2": w2,
    }
    b_p, s_p, d = x_prompt.shape
    db, ds, _ = x_sample.shape
    n_p, n_s = b_p * s_p, db * ds
    mem_len = mem_prompt.shape[1]
    past = page_table.shape[1] * cache_k.shape[1]
    assert past % MOBA_BLOCK == 0
    ts_p, tm_p = 256, 256
    ts_s, tm_s = ds, n_s

    memkv_t = _memkv(mem_prompt, g_mem, w_mem_kv.transpose(0, 2, 1).astype(BF16))
    kv6 = memkv_t.reshape(depth, b_p, 2, MEM_H, HEAD_DIM, mem_len).transpose(2, 0, 1, 5, 3, 4)
    mem_k_p, mem_v_p = kv6[0], kv6[1]
    cmk = cache_mem_k.transpose(0, 1, 3, 4, 2).reshape(depth, db, D_MEMQ, mem_len)
    cmv = cache_mem_v.transpose(0, 1, 3, 4, 2).reshape(depth, db, D_MEMQ, mem_len)
    mem_p = (memkv_t, memkv_t, (0, 1))
    mem_s = (cmk, cmv, (0, 0))

    ctx_p = jnp.zeros((b_p, CONV_W - 1, state_conv.shape[-1]), F32)
    x1_p, hf_p, gate_p, eid_p, conv_p = _layer0(x_prompt.reshape(n_p, d), b_p, s_p, ctx_p, *mem_p, 0, P, ts_p)
    x1_s, hf_s, gate_s, eid_s, conv_s = _layer0(x_sample.reshape(n_s, d), db, ds, state_conv[0], *mem_s, 0, P, ts_s)
    yr, (dest_p, dest_s) = _moe_layer(0, [(hf_p, eid_p, tm_p), (hf_s, eid_s, tm_s)], P)

    x2_p, kt_p, vt_p, qt_p, qm_p = _mid(x1_p, gate_p, yr, dest_p, jnp.arange(s_p), b_p, P, tm_p)
    x2_s, kt_s, vt_s, qt_s, qm_s = _mid(x1_s, gate_s, yr, dest_s, past + jnp.arange(ds), 1, P, tm_s)
    c1_p = _moba_prompt(qt_p, kt_p, vt_p, b_p, s_p)
    k_new, v_new = kt_s[0].T, vt_s[0].T
    c1_s = _moba_sample(qt_s[0].T, k_new, v_new, cache_k, cache_v, page_table, db, ds)
    x3_p, hf_p, gate_p, eid_p = _layer1(x2_p, c1_p, qm_p, b_p, s_p, *mem_p, 1, P, ts_p)
    x3_s, hf_s, gate_s, eid_s = _layer1(x2_s, c1_s, qm_s, db, ds, *mem_s, 1, P, ts_s)
    yr, (dest_p, dest_s) = _moe_layer(1, [(hf_p, eid_p, tm_p), (hf_s, eid_s, tm_s)], P)
    y_p = _final(x3_p, gate_p, yr, dest_p, g_final, tm_p).reshape(b_p, s_p, d)
    y_s = _final(x3_s, gate_s, yr, dest_s, g_final, tm_s).reshape(db, ds, d)

    k_p = kt_p.reshape(b_p, n_heads, HEAD_DIM, s_p).transpose(0, 3, 1, 2)
    v_p = vt_p.reshape(b_p, n_heads, HEAD_DIM, s_p).transpose(0, 3, 1, 2)
    k_s = k_new.reshape(db, ds, n_heads, HEAD_DIM)
    v_s = v_new.reshape(db, ds, n_heads, HEAD_DIM)
    return (y_p, y_s, conv_p[None], conv_s[None], k_p, v_p, k_s, v_s, mem_k_p, mem_v_p)
```

```python
import functools

import jax
import jax.numpy as jnp
from jax import lax
from jax.experimental import pallas as pl
from jax.experimental.pallas import tpu as pltpu

F32 = jnp.float32
BF16 = jnp.bfloat16
I32 = jnp.int32
HIGHEST = lax.Precision.HIGHEST

EPS = 1e-6
HEAD_DIM = 64
MEM_H = 4
D_MEMQ = MEM_H * HEAD_DIM
CONV_W = 31
CTX_ROWS = 32
CTX_PAD = CTX_ROWS - (CONV_W - 1)
MOBA_BLOCK = 256
MOBA_TOPK = 3
ROT_DIM = HEAD_DIM // 4
ROPE_THETA = 500000.0
N_GROUPS = 4
E_PER_GROUP = 16
N_EXPERTS = N_GROUPS * E_PER_GROUP
MOE_BLK = 256
PLAN_TILE_MAX = 2048
GATHER_UNROLL = 8
ROUTER_LANES = 128
LANE = 128
SUBLANE = 8
VMEM_LIMIT = 56 * 1024 * 1024

NEG_INF = float("-inf")
BIG_IDX = 1 << 20
NT_DIMS = (((1,), (1,)), ((), ()))


def _rms(x, g):
    return x * lax.rsqrt(jnp.mean(x * x, axis=-1, keepdims=True) + EPS) * g


def _first_argmax(vals, idx, axis):
    m = jnp.max(vals, axis=axis, keepdims=True)
    first = jnp.min(jnp.where(vals == m, idx, BIG_IDX), axis=axis, keepdims=True)
    return m, first


def _top_k_mask(gate, idx, axis):
    sel = jnp.zeros(gate.shape, F32)
    for _ in range(MOBA_TOPK):
        m, first = _first_argmax(gate, idx, axis)
        pick = jnp.logical_and(idx == first, m > NEG_INF)
        sel = jnp.where(pick, 1.0, sel)
        gate = jnp.where(pick, NEG_INF, gate)
    return sel


def _rot_t(xt_ref, cos_t, sin_t, n_heads):
    half = ROT_DIM // 2
    for hh in range(n_heads):
        r0 = hh * HEAD_DIM
        xa = xt_ref[0, r0:r0 + half, :]
        xb = xt_ref[0, r0 + half:r0 + ROT_DIM, :]
        xt_ref[0, r0:r0 + half, :] = xa * cos_t - xb * sin_t
        xt_ref[0, r0 + half:r0 + ROT_DIM, :] = xb * cos_t + xa * sin_t


def _memkv_kernel(mem_ref, g_ref, wt_ref, out_ref):
    h = _rms(mem_ref[0], g_ref[0]).astype(BF16)
    out_ref[0, 0] = lax.dot_general(wt_ref[0], h, NT_DIMS, preferred_element_type=F32)


def _memkv(mem, g_mem, w_mem_kv_t_bf):
    b, m, d = mem.shape
    depth = g_mem.shape[0]
    n_out = w_mem_kv_t_bf.shape[1]
    return pl.pallas_call(
        _memkv_kernel,
        out_shape=jax.ShapeDtypeStruct((depth, b, n_out, m), F32),
        grid=(depth, b),
        in_specs=[
            pl.BlockSpec((1, m, d), lambda l, i: (i, 0, 0)),
            pl.BlockSpec((1, 1, d), lambda l, i: (l, 0, 0)),
            pl.BlockSpec((1, n_out, d), lambda l, i: (l, 0, 0)),
        ],
        out_specs=pl.BlockSpec((1, 1, n_out, m), lambda l, i: (l, i, 0, 0)),
        compiler_params=pltpu.CompilerParams(dimension_semantics=("arbitrary", "arbitrary")),
        name="memkv",
    )(mem, g_mem.reshape(depth, 1, d), w_mem_kv_t_bf)


def _mem_attn_into(mix, off, qm, mkt_ref, mvt_ref):
    mkt = mkt_ref[0, 0].astype(BF16)
    mvt = mvt_ref[0, 0].astype(BF16)
    scale = HEAD_DIM ** -0.5
    lane = lax.broadcasted_iota(I32, qm.shape, 1)
    om = jnp.zeros(qm.shape, F32)
    for hh in range(MEM_H):
        in_head = (lane // HEAD_DIM) == hh
        q = jnp.where(in_head, qm, 0.0).astype(BF16)
        s = jnp.dot(q, mkt, preferred_element_type=F32) * scale
        m = jnp.max(s, axis=-1, keepdims=True)
        e = jnp.exp(s - m)
        l = jnp.sum(e, axis=-1, keepdims=True)
        o = lax.dot_general(e.astype(BF16), mvt, NT_DIMS, preferred_element_type=F32) / l
        om = jnp.where(in_head, o, om)
    mix[:, off:off + D_MEMQ] = om.astype(BF16)


def _out_and_route(x, mix, wout_ref, gffn_ref, wr_ref, br_ref, x1_ref, hf_ref, gate_ref, eid_ref):
    x1 = x + jnp.dot(mix[...], wout_ref[...], preferred_element_type=F32)
    x1_ref[...] = x1
    hf = _rms(x1, gffn_ref[...])
    hf_ref[...] = hf
    hf_hi = hf.astype(BF16)
    hf_lo = (hf - hf_hi.astype(F32)).astype(BF16)
    logits = jnp.dot(jnp.concatenate([hf_hi, hf_lo, hf_hi], axis=1), wr_ref[...],
                     preferred_element_type=F32) + br_ref[...]
    lane = lax.broadcasted_iota(I32, logits.shape, 1)
    gl = jnp.where((lane >= N_EXPERTS) & (lane < N_EXPERTS + N_GROUPS), logits, NEG_INF)
    gmax, glane = _first_argmax(gl, lane, -1)
    p_group = 1.0 / jnp.sum(jnp.exp(gl - gmax), axis=-1, keepdims=True)
    e_lo = (glane - N_EXPERTS) * E_PER_GROUP
    el = jnp.where((lane >= e_lo) & (lane < e_lo + E_PER_GROUP), logits, NEG_INF)
    m1, i1 = _first_argmax(el, lane, -1)
    el2 = jnp.where(lane == i1, NEG_INF, el)
    m2, i2 = _first_argmax(el2, lane, -1)
    e2 = jnp.exp(m2 - m1)
    g1 = p_group / (1.0 + e2)
    g2 = g1 * e2
    lane2 = lax.broadcasted_iota(I32, gate_ref.shape, 1)
    gate_ref[...] = jnp.where(lane2 == 0, g1, g2)
    eid_ref[...] = jnp.where(lane2 == 0, i1, i2)


def _router_weights(w_rg, b_rg, w_re, b_re):
    d = w_rg.shape[0]
    pad = ROUTER_LANES - N_EXPERTS - N_GROUPS
    wr = jnp.concatenate([w_re, w_rg, jnp.zeros((d, pad), F32)], axis=1)
    br = jnp.concatenate([b_re, b_rg, jnp.zeros((pad,), F32)]).reshape(1, ROUTER_LANES)
    w_hi = wr.astype(BF16)
    w_lo = (wr - w_hi.astype(F32)).astype(BF16)
    return jnp.concatenate([w_hi, w_hi, w_lo], axis=0), br


def _layer0_kernel(x_ref, ctx_ref, mk_ref, mv_ref, gmix_ref, win_ref, cw_ref, cb_ref, lg_ref, lb_ref,
                   wout_ref, gffn_ref, wr_ref, br_ref,
                   x1_ref, hf_ref, gate_ref, eid_ref, nc_ref,
                   zext, zsh, cbuf, mix, *, ts, rc, d_conv):
    t = pl.program_id(1)
    nt = pl.num_programs(1)

    @pl.when(t == 0)
    def _():
        zext[0:CTX_PAD, :] = jnp.zeros((CTX_PAD, d_conv), F32)
        zext[CTX_PAD:CTX_ROWS, :] = ctx_ref[0]

    @pl.when(t > 0)
    def _():
        zext[0:CTX_ROWS, :] = zext[ts:ts + CTX_ROWS, :]

    x = x_ref[...]
    h = _rms(x, gmix_ref[...]).astype(BF16)
    u = jnp.dot(h, win_ref[...], preferred_element_type=F32)
    z = u[:, :d_conv] * jax.nn.sigmoid(u[:, d_conv:2 * d_conv])
    zext[CTX_ROWS:CTX_ROWS + ts, :] = z
    qm = u[:, 2 * d_conv:]

    @pl.when(t == nt - 1)
    def _():
        nc_ref[0] = zext[ts + CTX_PAD: ts + CTX_ROWS, :]

    for sh in range(1, SUBLANE):
        zsh[sh - 1] = zext[sh:sh + ts + CTX_ROWS - SUBLANE, :]

    def chunk(ci, carry):
        r0 = pl.multiple_of(ci * rc, SUBLANE)
        for cbk in range(d_conv // LANE):
            cs = slice(cbk * LANE, (cbk + 1) * LANE)
            acc = jnp.zeros((rc, LANE), F32) + cb_ref[:, cs]
            for w in range(CONV_W):
                hi, sh = divmod(w + CTX_PAD, SUBLANE)
                if sh == 0:
                    win = zext[pl.ds(r0 + hi * SUBLANE, rc), cs]
                else:
                    win = zsh[sh - 1, pl.ds(r0 + hi * SUBLANE, rc), cs]
                acc = acc + win * cw_ref[w:w + 1, cs]
            cbuf[pl.ds(r0, rc), cs] = acc
        return carry

    lax.fori_loop(0, ts // rc, chunk, 0)

    c = cbuf[...]
    mu = jnp.mean(c, axis=-1, keepdims=True)
    xc = c - mu
    y = xc * lax.rsqrt(jnp.mean(xc * xc, axis=-1, keepdims=True) + EPS) * lg_ref[...] + lb_ref[...]
    mix[:, :d_conv] = (y * jax.nn.sigmoid(y)).astype(BF16)
    _mem_attn_into(mix, d_conv, qm, mk_ref, mv_ref)
    _out_and_route(x, mix, wout_ref, gffn_ref, wr_ref, br_ref, x1_ref, hf_ref, gate_ref, eid_ref)


def _layer0(x2d, b, s, ctx, mk_arr, mv_arr, kv_rows, l, P, ts):
    n, d = x2d.shape
    nt = s // ts
    d_conv = ctx.shape[-1]
    rc = min(32, ts)
    d_in = P["w_in_a_bf"].shape[-1]
    krow, vrow = kv_rows
    const = lambda *shape: pl.BlockSpec(shape, lambda i, t: (0,) * len(shape))
    tok = lambda w: pl.BlockSpec((ts, w), lambda i, t: (i * nt + t, 0))
    kern = functools.partial(_layer0_kernel, ts=ts, rc=rc, d_conv=d_conv)
    return pl.pallas_call(
        kern,
        out_shape=(
            jax.ShapeDtypeStruct((n, d), F32),
            jax.ShapeDtypeStruct((n, d), F32),
            jax.ShapeDtypeStruct((n, 2), F32),
            jax.ShapeDtypeStruct((n, 2), I32),
            jax.ShapeDtypeStruct((b, CONV_W - 1, d_conv), F32),
        ),
        grid=(b, nt),
        in_specs=[
            tok(d),
            pl.BlockSpec((1, CONV_W - 1, d_conv), lambda i, t: (i, 0, 0)),
            pl.BlockSpec((1, 1, D_MEMQ, mk_arr.shape[3]), lambda i, t: (l, i, krow, 0)),
            pl.BlockSpec((1, 1, D_MEMQ, mv_arr.shape[3]), lambda i, t: (l, i, vrow, 0)),
            const(1, d), const(d, d_in), const(CONV_W, d_conv), const(1, d_conv), const(1, d_conv),
            const(1, d_conv), const(d_conv + D_MEMQ, d), const(1, d), const(3 * d, ROUTER_LANES),
            const(1, ROUTER_LANES),
        ],
        out_specs=(tok(d), tok(d), tok(2), tok(2),
                   pl.BlockSpec((1, CONV_W - 1, d_conv), lambda i, t: (i, 0, 0))),
        scratch_shapes=[
            pltpu.VMEM((ts + CTX_ROWS, d_conv), F32),
            pltpu.VMEM((SUBLANE - 1, ts + CTX_ROWS - SUBLANE, d_conv), F32),
            pltpu.VMEM((ts, d_conv), F32),
            pltpu.VMEM((ts, d_conv + D_MEMQ), BF16),
        ],
        compiler_params=pltpu.CompilerParams(
            dimension_semantics=("arbitrary", "arbitrary"), vmem_limit_bytes=VMEM_LIMIT),
        name="layer0",
    )(x2d, ctx, mk_arr, mv_arr, P["g_mix"][l].reshape(1, d), P["w_in_a_bf"], P["conv_w"][0],
      P["conv_b"][0].reshape(1, d_conv), P["cln_g"][0].reshape(1, d_conv), P["cln_b"][0].reshape(1, d_conv),
      P["w_out_a_bf"], P["g_ffn"][l].reshape(1, d), P["wr"][l], P["br"][l])


def _layer1_kernel(x_ref, c_ref, qm_ref, mk_ref, mv_ref, wout_ref, gffn_ref, wr_ref, br_ref,
                   x1_ref, hf_ref, gate_ref, eid_ref, mix, *, d_att):
    mix[:, :d_att] = c_ref[...].astype(BF16)
    _mem_attn_into(mix, d_att, qm_ref[...], mk_ref, mv_ref)
    _out_and_route(x_ref[...], mix, wout_ref, gffn_ref, wr_ref, br_ref, x1_ref, hf_ref, gate_ref, eid_ref)


def _layer1(x2d, c2d, qm2d, b, s, mk_arr, mv_arr, kv_rows, l, P, ts):
    n, d = x2d.shape
    nt = s // ts
    d_att = c2d.shape[-1]
    krow, vrow = kv_rows
    const = lambda *shape: pl.BlockSpec(shape, lambda i, t: (0,) * len(shape))
    tok = lambda w: pl.BlockSpec((ts, w), lambda i, t: (i * nt + t, 0))
    return pl.pallas_call(
        functools.partial(_layer1_kernel, d_att=d_att),
        out_shape=(
            jax.ShapeDtypeStruct((n, d), F32),
            jax.ShapeDtypeStruct((n, d), F32),
            jax.ShapeDtypeStruct((n, 2), F32),
            jax.ShapeDtypeStruct((n, 2), I32),
        ),
        grid=(b, nt),
        in_specs=[
            tok(d), tok(d_att), tok(D_MEMQ),
            pl.BlockSpec((1, 1, D_MEMQ, mk_arr.shape[3]), lambda i, t: (l, i, krow, 0)),
            pl.BlockSpec((1, 1, D_MEMQ, mv_arr.shape[3]), lambda i, t: (l, i, vrow, 0)),
            const(d_att + D_MEMQ, d), const(1, d), const(3 * d, ROUTER_LANES), const(1, ROUTER_LANES),
        ],
        out_specs=(tok(d), tok(d), tok(2), tok(2)),
        scratch_shapes=[pltpu.VMEM((ts, d_att + D_MEMQ), BF16)],
        compiler_params=pltpu.CompilerParams(
            dimension_semantics=("arbitrary", "arbitrary"), vmem_limit_bytes=VMEM_LIMIT),
        name="layer1",
    )(x2d, c2d, qm2d, mk_arr, mv_arr, P["w_out_b_bf"], P["g_ffn"][l].reshape(1, d), P["wr"][l], P["br"][l])


def _plan_kernel(eid_ref, dest_ref, counts_ref, cnt, base, pstart, *, ta):
    ph = pl.program_id(0)
    t = pl.program_id(1)
    lane = lax.broadcasted_iota(I32, (ta, LANE), 1)
    onehot = eid_ref[...] == lane
    ohf = jnp.where(onehot, 1.0, 0.0)
    tile_counts = jnp.sum(ohf, axis=0, keepdims=True)

    @pl.when(ph == 0)
    def _():
        @pl.when(t == 0)
        def _():
            cnt[...] = jnp.zeros((1, LANE), F32)
        cnt[...] = cnt[...] + tile_counts

    @pl.when(ph == 1)
    def _():
        @pl.when(t == 0)
        def _():
            ci = cnt[...].astype(I32)
            counts_ref[...] = ci
            pc = ((ci + (MOE_BLK - 1)) // MOE_BLK) * MOE_BLK
            lane1 = lax.broadcasted_iota(I32, (1, LANE), 1)
            incl = pc
            sh = 1
            while sh < LANE:
                incl = incl + jnp.where(lane1 >= sh, pltpu.roll(incl, sh, 1), 0)
                sh *= 2
            pstart[...] = (incl - pc).astype(F32)
            base[...] = jnp.zeros((1, LANE), F32)

        r = lax.broadcasted_iota(I32, (ta, ta), 0)
        c = lax.broadcasted_iota(I32, (ta, ta), 1)
        tri = jnp.where(r >= c, 1.0, 0.0).astype(BF16)
        prefix = jnp.dot(tri, ohf.astype(BF16), preferred_element_type=F32)
        slot = pstart[...] + base[...] + prefix - 1.0
        dest_ref[...] = jnp.sum(jnp.where(onehot, slot, 0.0), axis=1, keepdims=True).astype(I32)
        base[...] = base[...] + tile_counts


def _plan(eid_col):
    a = eid_col.shape[0]
    ta = max(t for t in range(LANE, PLAN_TILE_MAX + 1, LANE) if a % t == 0)
    return pl.pallas_call(
        functools.partial(_plan_kernel, ta=ta),
        out_shape=(jax.ShapeDtypeStruct((a, 1), I32), jax.ShapeDtypeStruct((1, LANE), I32)),
        grid=(2, a // ta),
        in_specs=[pl.BlockSpec((ta, 1), lambda ph, t: (t, 0))],
        out_specs=(pl.BlockSpec((ta, 1), lambda ph, t: (t * ph, 0)),
                   pl.BlockSpec((1, LANE), lambda ph, t: (0, 0))),
        scratch_shapes=[pltpu.VMEM((1, LANE), F32), pltpu.VMEM((1, LANE), F32), pltpu.VMEM((1, LANE), F32)],
        compiler_params=pltpu.CompilerParams(dimension_semantics=("arbitrary", "arbitrary")),
        name="moe_plan",
    )(eid_col)


def _dest_tiles(dest, tm):
    n = dest.shape[0]
    return dest.reshape(n // tm, tm, 2).transpose(0, 2, 1).reshape(n // tm, 2 * tm)


def _dispatch_kernel(dest_hbm, hf_ref, xs_in, xs_out, idx_smem, sem_idx, sem_rows, *, tm):
    del xs_in
    i = pl.program_id(0)
    n = pl.num_programs(0)
    slot = lax.rem(i, 2)

    def idx_copy(k, sl):
        return pltpu.make_async_copy(dest_hbm.at[k], idx_smem.at[sl], sem_idx.at[sl])

    @pl.when(i == 0)
    def _():
        idx_copy(0, 0).start()

    @pl.when(i + 1 < n)
    def _():
        idx_copy(i + 1, 1 - slot).start()

    idx_copy(i, slot).wait()

    def body(g, carry):
        for u in range(GATHER_UNROLL):
            r = g * GATHER_UNROLL + u
            for k in range(2):
                row = idx_smem[slot, k * tm + r]
                pltpu.make_async_copy(hf_ref.at[pl.ds(r, 1)], xs_out.at[pl.ds(row, 1)], sem_rows).start()
        return carry
    lax.fori_loop(0, tm // GATHER_UNROLL, body, 0)
    for _ in range(2):
        pltpu.make_async_copy(hf_ref, xs_out.at[pl.ds(0, tm)], sem_rows).wait()


def _dispatch(hf, dest, xs, tm):
    n, d = hf.shape
    return pl.pallas_call(
        functools.partial(_dispatch_kernel, tm=tm),
        out_shape=jax.ShapeDtypeStruct(xs.shape, xs.dtype),
        grid=(n // tm,),
        in_specs=[
            pl.BlockSpec(memory_space=pl.ANY),
            pl.BlockSpec((tm, d), lambda i: (i, 0)),
            pl.BlockSpec(memory_space=pl.ANY),
        ],
        out_specs=pl.BlockSpec(memory_space=pl.ANY),
        scratch_shapes=[
            pltpu.SMEM((2, 2 * tm), I32),
            pltpu.SemaphoreType.DMA((2,)),
            pltpu.SemaphoreType.DMA,
        ],
        input_output_aliases={2: 0},
        compiler_params=pltpu.CompilerParams(
            dimension_semantics=("arbitrary",), vmem_limit_bytes=VMEM_LIMIT),
        name="moe_dispatch",
    )(_dest_tiles(dest, tm), hf, xs)


def _experts_kernel(bexp_ref, nused_ref, xs_ref, w1_ref, w3_ref, w2_ref, out_ref, w13, w2s, *, d_exp):
    i = pl.program_id(0)
    nused = nused_ref[0]

    @pl.when(i < nused)
    def _():
        changed = jnp.logical_or(i == 0, bexp_ref[i] != bexp_ref[jnp.maximum(i - 1, 0)])

        @pl.when(changed)
        def _():
            w13[:, :d_exp] = w1_ref[0, 0].astype(BF16)
            w13[:, d_exp:] = w3_ref[0, 0].astype(BF16)
            w2s[...] = w2_ref[0, 0].astype(BF16)

        hcat = jnp.dot(xs_ref[...].astype(BF16), w13[...], preferred_element_type=F32)
        a = hcat[:, :d_exp]
        act = (a * jax.nn.sigmoid(a)) * hcat[:, d_exp:]
        out_ref[...] = jnp.dot(act.astype(BF16), w2s[...], preferred_element_type=F32)

    @pl.when(i >= nused)
    def _():
        out_ref[...] = jnp.zeros(out_ref.shape, F32)


def _experts(xs, blk_exp, nused, l, w1, w3, w2):
    p_rows, d = xs.shape
    d_exp = w1.shape[-1]
    n_blk = p_rows // MOE_BLK
    wspec = lambda r, c: pl.BlockSpec((1, 1, r, c), lambda i, be, nu: (l, be[jnp.minimum(i, nu[0] - 1)], 0, 0))
    return pl.pallas_call(
        functools.partial(_experts_kernel, d_exp=d_exp),
        out_shape=jax.ShapeDtypeStruct((p_rows, d), F32),
        grid_spec=pltpu.PrefetchScalarGridSpec(
            num_scalar_prefetch=2,
            grid=(n_blk,),
            in_specs=[
                pl.BlockSpec((MOE_BLK, d), lambda i, be, nu: (jnp.minimum(i, nu[0] - 1), 0)),
                wspec(d, d_exp), wspec(d, d_exp), wspec(d_exp, d),
            ],
            out_specs=pl.BlockSpec((MOE_BLK, d), lambda i, be, nu: (i, 0)),
            scratch_shapes=[
                pltpu.VMEM((d, 2 * d_exp), BF16),
                pltpu.VMEM((d_exp, d), BF16),
            ],
        ),
        compiler_params=pltpu.CompilerParams(
            dimension_semantics=("arbitrary",), vmem_limit_bytes=VMEM_LIMIT),
        name="moe_experts",
    )(blk_exp, nused, xs, w1, w3, w2)


def _moe_layer(l, groups, P):
    d = groups[0][0].shape[1]
    eid_all = jnp.concatenate([g[1] for g in groups], axis=0)
    n_all = eid_all.shape[0]
    a = 2 * n_all
    assert a % MOE_BLK == 0
    n_blk = a // MOE_BLK + N_EXPERTS
    dest_col, counts = _plan(eid_all.reshape(a, 1))
    counts = counts[0, :N_EXPERTS]
    pends = jnp.cumsum((counts + MOE_BLK - 1) // MOE_BLK * MOE_BLK)
    blk_start = jnp.arange(n_blk, dtype=I32) * MOE_BLK
    blk_exp = jnp.minimum(jnp.sum((pends[None, :] <= blk_start[:, None]).astype(I32), axis=1), N_EXPERTS - 1)
    nused = (pends[-1] // MOE_BLK).astype(I32).reshape(1)
    dest = dest_col.reshape(n_all, 2)
    xs = jnp.zeros((n_blk * MOE_BLK, d), F32)
    dests = []
    off = 0
    for hf, _, tm in groups:
        n = hf.shape[0]
        dests.append(dest[off:off + n])
        xs = _dispatch(hf, dests[-1], xs, tm)
        off += n
    yr = _experts(xs, blk_exp.astype(I32), nused, l, P["w1"], P["w3"], P["w2"])
    return yr, dests


def _gather_step(step, nsteps, idx_hbm, src_hbm, idx_smem, buf, sem_idx, sem_rows, nrows):
    slot = lax.rem(step, 2)
    nslot = 1 - slot

    def idx_copy(k, sl):
        return pltpu.make_async_copy(idx_hbm.at[k], idx_smem.at[sl], sem_idx.at[sl])

    def issue(sl):
        def body(g, carry):
            for u in range(GATHER_UNROLL):
                r = g * GATHER_UNROLL + u
                pltpu.make_async_copy(src_hbm.at[pl.ds(idx_smem[sl, r], 1)], buf.at[sl, pl.ds(r, 1)],
                                      sem_rows.at[sl]).start()
            return carry
        lax.fori_loop(0, nrows // GATHER_UNROLL, body, 0)

    @pl.when(step == 0)
    def _():
        first = idx_copy(0, 0)
        first.start()
        first.wait()
        issue(0)

        @pl.when(nsteps > 1)
        def _():
            idx_copy(1, 1).start()

    @pl.when(step + 1 < nsteps)
    def _():
        idx_copy(step + 1, nslot).wait()
        issue(nslot)

        @pl.when(step + 2 < nsteps)
        def _():
            idx_copy(step + 2, slot).start()

    pltpu.make_async_copy(src_hbm.at[pl.ds(0, nrows)], buf.at[slot], sem_rows.at[slot]).wait()
    return slot


def _combine(step, nsteps, dest_hbm, yr_hbm, x1_ref, gate_ref, idx_smem, ybuf, sem_idx, sem_rows, tm):
    slot = _gather_step(step, nsteps, dest_hbm, yr_hbm, idx_smem, ybuf, sem_idx, sem_rows, 2 * tm)
    g = gate_ref[...]
    y = g[:, 0:1] * ybuf[slot, 0:tm, :] + g[:, 1:2] * ybuf[slot, tm:2 * tm, :]
    return x1_ref[...] + y


def _mid_kernel(dest_hbm, yr_hbm, x1_ref, gate_ref, cos_ref, sin_ref,
                gkv_ref, wkvt_ref, gmix_ref, wqt_ref, wqm_ref,
                x2_ref, kt_ref, vt_ref, qt_ref, qm_ref,
                idx_smem, ybuf, sem_idx, sem_rows, *, tm, d_att):
    i = pl.program_id(0)
    x2 = _combine(i, pl.num_programs(0), dest_hbm, yr_hbm, x1_ref, gate_ref,
                  idx_smem, ybuf, sem_idx, sem_rows, tm)
    x2_ref[...] = x2
    n_heads = d_att // HEAD_DIM
    cos_t = cos_ref[...]
    sin_t = sin_ref[...]
    hk = _rms(x2, gkv_ref[...]).astype(BF16)
    kvt = lax.dot_general(wkvt_ref[...], hk, NT_DIMS, preferred_element_type=F32)
    vt_ref[0] = kvt[d_att:, :]
    kt_ref[0] = kvt[:d_att, :]
    _rot_t(kt_ref, cos_t, sin_t, n_heads)
    hq = _rms(x2, gmix_ref[...]).astype(BF16)
    qt_ref[0] = lax.dot_general(wqt_ref[...], hq, NT_DIMS, preferred_element_type=F32)
    _rot_t(qt_ref, cos_t, sin_t, n_heads)
    qm_ref[...] = jnp.dot(hq, wqm_ref[...], preferred_element_type=F32)


def _mid(x1, gate, yr, dest, pos, n_seq, P, tm):
    n, d = x1.shape
    d_att = P["w_kv_t_bf"].shape[0] // 2
    nsteps = n // tm
    s_out = n // n_seq
    nt = s_out // tm
    half = ROT_DIM // 2
    reps = s_out // pos.shape[0]
    inv = ROPE_THETA ** (-jnp.arange(half, dtype=F32) * (2.0 / ROT_DIM))
    ang_t = jnp.tile(inv[:, None] * pos.astype(F32)[None, :], (1, reps))
    const = lambda *shape: pl.BlockSpec(shape, lambda i: (0,) * len(shape))
    tok = lambda w: pl.BlockSpec((tm, w), lambda i: (i, 0))
    tbl_t = pl.BlockSpec((half, tm), lambda i: (0, i % nt))
    feat_t = pl.BlockSpec((1, d_att, tm), lambda i: (i // nt, 0, i % nt))
    feat_shape = jax.ShapeDtypeStruct((n_seq, d_att, s_out), F32)
    return pl.pallas_call(
        functools.partial(_mid_kernel, tm=tm, d_att=d_att),
        out_shape=(jax.ShapeDtypeStruct((n, d), F32), feat_shape, feat_shape, feat_shape,
                   jax.ShapeDtypeStruct((n, D_MEMQ), F32)),
        grid=(nsteps,),
        in_specs=[
            pl.BlockSpec(memory_space=pl.ANY), pl.BlockSpec(memory_space=pl.ANY),
            tok(d), tok(2), tbl_t, tbl_t,
            const(1, d), const(2 * d_att, d), const(1, d), const(d_att, d), const(d, D_MEMQ),
        ],
        out_specs=(tok(d), feat_t, feat_t, feat_t, tok(D_MEMQ)),
        scratch_shapes=[
            pltpu.SMEM((2, 2 * tm), I32),
            pltpu.VMEM((2, 2 * tm, d), F32),
            pltpu.SemaphoreType.DMA((2,)),
            pltpu.SemaphoreType.DMA((2,)),
        ],
        compiler_params=pltpu.CompilerParams(
            dimension_semantics=("arbitrary",), vmem_limit_bytes=VMEM_LIMIT),
        name="mid",
    )(_dest_tiles(dest, tm), yr, x1, gate, jnp.cos(ang_t), jnp.sin(ang_t),
      P["g_kv"].reshape(1, d), P["w_kv_t_bf"], P["g_mix"][1].reshape(1, d), P["w_q_t_bf"], P["w_qm_bf"])


def _final_kernel(dest_hbm, yr_hbm, x1_ref, gate_ref, g_ref, y_ref, idx_smem, ybuf, sem_idx, sem_rows, *, tm):
    i = pl.program_id(0)
    x = _combine(i, pl.num_programs(0), dest_hbm, yr_hbm, x1_ref, gate_ref,
                 idx_smem, ybuf, sem_idx, sem_rows, tm)
    y_ref[...] = _rms(x, g_ref[...])


def _final(x1, gate, yr, dest, g_final, tm):
    n, d = x1.shape
    tok = lambda w: pl.BlockSpec((tm, w), lambda i: (i, 0))
    return pl.pallas_call(
        functools.partial(_final_kernel, tm=tm),
        out_shape=jax.ShapeDtypeStruct((n, d), F32),
        grid=(n // tm,),
        in_specs=[
            pl.BlockSpec(memory_space=pl.ANY), pl.BlockSpec(memory_space=pl.ANY),
            tok(d), tok(2), pl.BlockSpec((1, d), lambda i: (0, 0)),
        ],
        out_specs=tok(d),
        scratch_shapes=[
            pltpu.SMEM((2, 2 * tm), I32),
            pltpu.VMEM((2, 2 * tm, d), F32),
            pltpu.SemaphoreType.DMA((2,)),
            pltpu.SemaphoreType.DMA((2,)),
        ],
        compiler_params=pltpu.CompilerParams(
            dimension_semantics=("arbitrary",), vmem_limit_bytes=VMEM_LIMIT),
        name="final",
    )(_dest_tiles(dest, tm), yr, x1, gate, g_final.reshape(1, d))


def _moba_prompt_kernel(qt_ref, kt_ref, vt_ref, o_ref, kmean, krows, vones, sel_sc, s_buf, p_buf, acc,
                        *, s_len, n_grp):
    qb_i = pl.program_id(2)
    nb = s_len // MOBA_BLOCK
    n_h = LANE // HEAD_DIM
    tq = MOBA_BLOCK
    nq = n_h * tq
    scale = HEAD_DIM ** -0.5
    feat = lax.broadcasted_iota(I32, (LANE, MOBA_BLOCK), 0) // HEAD_DIM
    groups = range(n_grp)

    @pl.when(qb_i == 0)
    def _():
        kmean[...] = jnp.zeros(kmean.shape, F32)
        for g in groups:
            rows = slice(g * LANE, (g + 1) * LANE)
            for jb in range(nb):
                cols = slice(jb * MOBA_BLOCK, (jb + 1) * MOBA_BLOCK)
                kr = kt_ref[0, rows, cols].T
                kmean[g, jb:jb + 1, :] = jnp.sum(kr, axis=0, keepdims=True) * (1.0 / MOBA_BLOCK)
                krows[g, jb] = kr.astype(BF16)
                vblk = vt_ref[0, rows, cols]
                for hh in range(n_h):
                    vones[g, hh, jb] = jnp.where(feat == hh, vblk, 1.0).astype(BF16)

    key = lax.broadcasted_iota(I32, (MOBA_BLOCK, nq), 0)
    qcol = lax.broadcasted_iota(I32, (MOBA_BLOCK, nq), 1) % tq
    qs, m0, p_own = [], [], []
    for g in groups:
        qt = qt_ref[0, g * LANE:(g + 1) * LANE, :]
        qstack = jnp.concatenate([jnp.where(feat == hh, qt, 0.0) for hh in range(n_h)], axis=1)
        gate = jnp.dot(kmean[g], qstack, preferred_element_type=F32, precision=HIGHEST)
        brow = lax.broadcasted_iota(I32, gate.shape, 0)
        gate = jnp.where(brow < qb_i, gate, NEG_INF)
        sel = _top_k_mask(gate, brow, 0)
        for jb in range(nb):
            sel_sc[g, jb] = sel[jb:jb + 1, :]
        qs.append((qstack * scale).astype(BF16))
        s_own = jnp.where(key <= qcol, jnp.dot(krows[g, qb_i], qs[g], preferred_element_type=F32), NEG_INF)
        m0.append(jnp.max(s_own, axis=0, keepdims=True))
        p_own.append(jnp.exp(s_own - m0[g]).astype(BF16))

    def put_scores(g, j, slot):
        s_buf[g, slot] = jnp.dot(krows[g, j], qs[g], preferred_element_type=F32)

    def add_block(g, alpha, j, slot):
        for hh in range(n_h):
            acc[g, hh] = (alpha[:, hh * tq:(hh + 1) * tq] * acc[g, hh]
                          + jnp.dot(vones[g, hh, j], p_buf[g, slot, :, hh * tq:(hh + 1) * tq],
                                    preferred_element_type=F32))

    for g in groups:
        acc[g] = jnp.zeros((n_h, LANE, tq), F32)
        p_buf[g, 1] = p_own[g]
        put_scores(g, 0, 0)

    def half_trip(j, slot, carry):
        out = []
        for g in groups:
            m, alpha_prev = carry[g]
            add_block(g, alpha_prev, jnp.where(j == 0, qb_i, j - 1), 1 - slot)
            put_scores(g, jnp.minimum(j + 1, nb - 1), 1 - slot)
            chosen = sel_sc[g, j] > 0.0
            m_new = jnp.maximum(m, jnp.where(chosen, jnp.max(s_buf[g, slot], axis=0, keepdims=True), NEG_INF))
            p_buf[g, slot] = jnp.exp(s_buf[g, slot] - jnp.where(chosen, m_new, jnp.inf)).astype(BF16)
            out.append((m_new, jnp.exp(m - m_new)))
        return tuple(out)

    def body(t, carry):
        return half_trip(2 * t + 1, 1, half_trip(2 * t, 0, carry))

    n_trips = (qb_i + 1) // 2
    one = jnp.ones((1, nq), F32)
    state = lax.fori_loop(0, n_trips, body, tuple((m0[g], one) for g in groups))
    last = jnp.where(n_trips == 0, qb_i, 2 * n_trips - 1)
    for g in groups:
        add_block(g, state[g][1], last, 1)
        out_t = jnp.zeros((LANE, tq), F32)
        for hh in range(n_h):
            other = (1 - hh) * HEAD_DIM
            a_h = acc[g, hh]
            out_t = jnp.where(feat == hh, a_h / a_h[other:other + 1, :], out_t)
        o_ref[:, g * LANE:(g + 1) * LANE] = out_t.T


def _moba_prompt(qt, kt, vt, b, s, n_grp=2):
    d_att = qt.shape[1]
    nb = s // MOBA_BLOCK
    fw = n_grp * LANE
    assert s % MOBA_BLOCK == 0 and LANE // HEAD_DIM == 2 and d_att % fw == 0
    nb_pad = -(-nb // SUBLANE) * SUBLANE
    kv_spec = pl.BlockSpec((1, fw, s), lambda i, hp, qb: (i, hp, 0))
    return pl.pallas_call(
        functools.partial(_moba_prompt_kernel, s_len=s, n_grp=n_grp),
        out_shape=jax.ShapeDtypeStruct((b * s, d_att), F32),
        grid=(b, d_att // fw, nb),
        in_specs=[
            pl.BlockSpec((1, fw, MOBA_BLOCK), lambda i, hp, qb: (i, hp, qb)),
            kv_spec, kv_spec,
        ],
        out_specs=pl.BlockSpec((MOBA_BLOCK, fw), lambda i, hp, qb: (i * nb + qb, hp)),
        scratch_shapes=[
            pltpu.VMEM((n_grp, nb_pad, LANE), F32),
            pltpu.VMEM((n_grp, nb, MOBA_BLOCK, LANE), BF16),
            pltpu.VMEM((n_grp, 2, nb, LANE, MOBA_BLOCK), BF16),
            pltpu.VMEM((n_grp, nb, 1, 2 * MOBA_BLOCK), F32),
            pltpu.VMEM((n_grp, 2, MOBA_BLOCK, 2 * MOBA_BLOCK), F32),
            pltpu.VMEM((n_grp, 2, MOBA_BLOCK, 2 * MOBA_BLOCK), BF16),
            pltpu.VMEM((n_grp, 2, LANE, MOBA_BLOCK), F32),
        ],
        compiler_params=pltpu.CompilerParams(
            dimension_semantics=("arbitrary", "arbitrary", "arbitrary"), vmem_limit_bytes=VMEM_LIMIT),
        name="moba_prompt",
    )(qt, kt, vt)


def _head_diag(o, n_heads, ds):
    return jnp.concatenate(
        [o[h * ds:(h + 1) * ds, h * HEAD_DIM:(h + 1) * HEAD_DIM] for h in range(n_heads)], axis=0)


def _moba_sample_kernel(pt_ref, q_ref, kn_ref, vn_ref, *rest, ds, n_heads, nb, bps):
    pages = rest[:4 * bps]
    o_ref, qbd, g_all, m_all, l_all, o_all = rest[4 * bps:]
    jg = pl.program_id(1)
    rows = n_heads * ds
    d_att = n_heads * HEAD_DIM
    scale = HEAD_DIM ** -0.5
    blane = lax.broadcasted_iota(I32, (rows, LANE), 1)

    @pl.when(jg == 0)
    def _():
        q = q_ref[...]
        lane = lax.broadcasted_iota(I32, (ds, d_att), 1)
        for h in range(n_heads):
            qbd[h * ds:(h + 1) * ds, :] = jnp.where(lane // HEAD_DIM == h, q, 0.0)
        g_all[...] = jnp.full((rows, LANE), NEG_INF, F32)
        m_all[...] = jnp.zeros((rows, LANE), F32)
        l_all[...] = jnp.zeros((rows, LANE), F32)

    qb = qbd[...].astype(BF16)
    g_new, m_new, l_new = g_all[...], m_all[...], l_all[...]
    for bi in range(bps):
        k0_ref, k1_ref = pages[2 * bi], pages[2 * bi + 1]
        v0_ref, v1_ref = pages[2 * bps + 2 * bi], pages[2 * bps + 2 * bi + 1]
        j = jg * bps + bi
        s0 = jnp.dot(qb, k0_ref[0].astype(BF16), preferred_element_type=F32)
        s1 = jnp.dot(qb, k1_ref[0].astype(BF16), preferred_element_type=F32)
        g = jnp.sum(s0 + s1, axis=-1, keepdims=True) * (1.0 / MOBA_BLOCK)
        m = jnp.max(jnp.maximum(s0, s1), axis=-1, keepdims=True) * scale
        p0 = jnp.exp(s0 * scale - m)
        p1 = jnp.exp(s1 * scale - m)
        l = jnp.sum(p0 + p1, axis=-1, keepdims=True)
        o = (lax.dot_general(p0.astype(BF16), v0_ref[0].astype(BF16), NT_DIMS, preferred_element_type=F32)
             + lax.dot_general(p1.astype(BF16), v1_ref[0].astype(BF16), NT_DIMS, preferred_element_type=F32))
        o_all[j] = _head_diag(o, n_heads, ds)
        here = blane == j
        g_new = jnp.where(here, g, g_new)
        m_new = jnp.where(here, m, m_new)
        l_new = jnp.where(here, l, l_new)
    g_all[...] = g_new
    m_all[...] = m_new
    l_all[...] = l_new

    @pl.when(jg == nb // bps - 1)
    def _():
        chosen = _top_k_mask(g_new, blane, -1) > 0.0
        s_own = lax.dot_general(qb, kn_ref[...].astype(BF16), NT_DIMS, preferred_element_type=F32) * scale
        r_i = lax.broadcasted_iota(I32, (rows, ds), 0) % ds
        c_i = lax.broadcasted_iota(I32, (rows, ds), 1)
        s_own = jnp.where(c_i <= r_i, s_own, NEG_INF)
        m_sel = jnp.max(jnp.where(chosen, m_new, NEG_INF), axis=-1, keepdims=True)
        m_tot = jnp.maximum(m_sel, jnp.max(s_own, axis=-1, keepdims=True))
        wgt = jnp.where(chosen, jnp.exp(m_new - m_tot), 0.0)
        p_own = jnp.exp(s_own - m_tot)
        l_tot = jnp.sum(wgt * l_new, axis=-1, keepdims=True) + jnp.sum(p_own, axis=-1, keepdims=True)
        acc = _head_diag(jnp.dot(p_own.astype(BF16), vn_ref[...].astype(BF16), preferred_element_type=F32),
                         n_heads, ds)
        for jj in range(nb):
            acc = acc + wgt[:, jj:jj + 1] * o_all[jj]
        o_ref[0] = acc / l_tot


def _moba_sample(qb, k_new, v_new, cache_k, cache_v, page_table, db, ds, bps=4):
    n_pool, page, n_heads, hd = cache_k.shape
    d_att = n_heads * hd
    n_pages = page_table.shape[1]
    ppb = MOBA_BLOCK // page
    nb = n_pages // ppb
    rows = n_heads * ds
    assert ppb == 2 and n_pages % ppb == 0 and ds <= MOBA_BLOCK and nb <= LANE and nb % bps == 0
    ck = cache_k.transpose(0, 2, 3, 1).reshape(n_pool, d_att, page)
    cv = cache_v.transpose(0, 2, 3, 1).reshape(n_pool, d_att, page)
    seq = pl.BlockSpec((ds, d_att), lambda i, j, pt: (i, 0))
    ppg = ppb * bps
    pg = lambda off: pl.BlockSpec((1, d_att, page), lambda i, j, pt: (pt[i * n_pages + ppg * j + off], 0, 0))
    out = pl.pallas_call(
        functools.partial(_moba_sample_kernel, ds=ds, n_heads=n_heads, nb=nb, bps=bps),
        out_shape=jax.ShapeDtypeStruct((db, rows, hd), F32),
        grid_spec=pltpu.PrefetchScalarGridSpec(
            num_scalar_prefetch=1,
            grid=(db, nb // bps),
            in_specs=[seq, seq, seq] + [pg(o) for o in range(ppg)] * 2,
            out_specs=pl.BlockSpec((1, rows, hd), lambda i, j, pt: (i, 0, 0)),
            scratch_shapes=[
                pltpu.VMEM((rows, d_att), F32),
                pltpu.VMEM((rows, LANE), F32),
                pltpu.VMEM((rows, LANE), F32),
                pltpu.VMEM((rows, LANE), F32),
                pltpu.VMEM((nb, rows, hd), F32),
            ],
        ),
        compiler_params=pltpu.CompilerParams(
            dimension_semantics=("arbitrary", "arbitrary"), vmem_limit_bytes=VMEM_LIMIT),
        name="moba_sample",
    )(page_table.reshape(-1), qb, k_new, v_new, *([ck] * ppg), *([cv] * ppg))
    return out.reshape(db, n_heads, ds, hd).transpose(0, 2, 1, 3).reshape(db * ds, d_att)


def kernel(x_prompt, x_sample, state_conv, cache_k, cache_v, cache_mem_k, cache_mem_v, page_table,
           mem_prompt, g_mix, g_ffn, g_final, g_mem, w_mem_kv, w_in_a, conv_w, conv_b, cln_g, cln_b,
           w_out_a, g_kv, w_kv, w_in_b, w_out_b, w_rg, b_rg, w_re, b_re, w1, w3, w2):
    depth = g_mix.shape[0]
    n_heads = w_kv.shape[1] // (2 * HEAD_DIM)
    d_att = n_heads * HEAD_DIM
    routers = [_router_weights(w_rg[l], b_rg[l], w_re[l], b_re[l]) for l in range(depth)]
    P = {
        "g_mix": g_mix, "g_ffn": g_ffn, "g_final": g_final, "g_kv": g_kv,
        "w_in_a_bf": w_in_a[0].astype(BF16), "w_out_a_bf": w_out_a[0].astype(BF16),
        "w_kv_t_bf": w_kv.T.astype(BF16),
        "w_q_t_bf": w_in_b[0][:, :d_att].T.astype(BF16), "w_qm_bf": w_in_b[0][:, d_att:].astype(BF16),
        "w_out_b_bf": w_out_b[0].astype(BF16),
        "conv_w": conv_w, "conv_b": conv_b, "cln_g": cln_g, "cln_b": cln_b,
        "wr": [r[0] for r in routers], "br": [r[1] for r in routers],
        "w1": w1, "w3": w3, "w2": w2,
    }
    b_p, s_p, d = x_prompt.shape
    db, ds, _ = x_sample.shape
    n_p, n_s = b_p * s_p, db * ds
    mem_len = mem_prompt.shape[1]
    past = page_table.shape[1] * cache_k.shape[1]
    assert past % MOBA_BLOCK == 0
    ts_p, tm_p = 256, 256
    ts_s, tm_s = ds, n_s

    memkv_t = _memkv(mem_prompt, g_mem, w_mem_kv.transpose(0, 2, 1).astype(BF16))
    kv6 = memkv_t.reshape(depth, b_p, 2, MEM_H, HEAD_DIM, mem_len).transpose(2, 0, 1, 5, 3, 4)
    mem_k_p, mem_v_p = kv6[0], kv6[1]
    cmk = cache_mem_k.transpose(0, 1, 3, 4, 2).reshape(depth, db, D_MEMQ, mem_len)
    cmv = cache_mem_v.transpose(0, 1, 3, 4, 2).reshape(depth, db, D_MEMQ, mem_len)
    mem_p = (memkv_t, memkv_t, (0, 1))
    mem_s = (cmk, cmv, (0, 0))

    ctx_p = jnp.zeros((b_p, CONV_W - 1, state_conv.shape[-1]), F32)
    x1_p, hf_p, gate_p, eid_p, conv_p = _layer0(x_prompt.reshape(n_p, d), b_p, s_p, ctx_p, *mem_p, 0, P, ts_p)
    x1_s, hf_s, gate_s, eid_s, conv_s = _layer0(x_sample.reshape(n_s, d), db, ds, state_conv[0], *mem_s, 0, P, ts_s)
    yr, (dest_p, dest_s) = _moe_layer(0, [(hf_p, eid_p, tm_p), (hf_s, eid_s, tm_s)], P)

    x2_p, kt_p, vt_p, qt_p, qm_p = _mid(x1_p, gate_p, yr, dest_p, jnp.arange(s_p), b_p, P, tm_p)
    x2_s, kt_s, vt_s, qt_s, qm_s = _mid(x1_s, gate_s, yr, dest_s, past + jnp.arange(ds), 1, P, tm_s)
    c1_p = _moba_prompt(qt_p, kt_p, vt_p, b_p, s_p)
    k_new, v_new = kt_s[0].T, vt_s[0].T
    c1_s = _moba_sample(qt_s[0].T, k_new, v_new, cache_k, cache_v, page_table, db, ds)
    x3_p, hf_p, gate_p, eid_p = _layer1(x2_p, c1_p, qm_p, b_p, s_p, *mem_p, 1, P, ts_p)
    x3_s, hf_s, gate_s, eid_s = _layer1(x2_s, c1_s, qm_s, db, ds, *mem_s, 1, P, ts_s)
    yr, (dest_p, dest_s) = _moe_layer(1, [(hf_p, eid_p, tm_p), (hf_s, eid_s, tm_s)], P)
    y_p = _final(x3_p, gate_p, yr, dest_p, g_final, tm_p).reshape(b_p, s_p, d)
    y_s = _final(x3_s, gate_s, yr, dest_s, g_final, tm_s).reshape(db, ds, d)

    k_p = kt_p.reshape(b_p, n_heads, HEAD_DIM, s_p).transpose(0, 3, 1, 2)
    v_p = vt_p.reshape(b_p, n_heads, HEAD_DIM, s_p).transpose(0, 3, 1, 2)
    k_s = k_new.reshape(db, ds, n_heads, HEAD_DIM)
    v_s = v_new.reshape(db, ds, n_heads, HEAD_DIM)
    return (y_p, y_s, conv_p[None], conv_s[None], k_p, v_p, k_s, v_s, mem_k_p, mem_v_p)
```

```python
import functools

import jax
import jax.numpy as jnp
from jax import lax
from jax.experimental import pallas as pl
from jax.experimental.pallas import tpu as pltpu

F32 = jnp.float32
BF16 = jnp.bfloat16
I32 = jnp.int32
U32 = jnp.uint32
HIGHEST = lax.Precision.HIGHEST

EPS = 1e-6
HEAD_DIM = 64
MEM_H = 4
D_MEMQ = MEM_H * HEAD_DIM
CONV_W = 31
CTX_ROWS = 32
CTX_PAD = CTX_ROWS - (CONV_W - 1)
MOBA_BLOCK = 256
MOBA_TOPK = 3
ROT_DIM = HEAD_DIM // 4
ROPE_THETA = 500000.0
N_GROUPS = 4
E_PER_GROUP = 16
N_EXPERTS = N_GROUPS * E_PER_GROUP
MOE_BLK = 256
PLAN_TILE_MAX = 2048
GATHER_UNROLL = 8
ROUTER_LANES = 128
LANE = 128
SUBLANE = 8
VMEM_LIMIT = 56 * 1024 * 1024

NEG_INF = float("-inf")
BIG_IDX = 1 << 20
NT_DIMS = (((1,), (1,)), ((), ()))


def _rms(x, g):
    return x * lax.rsqrt(jnp.mean(x * x, axis=-1, keepdims=True) + EPS) * g


def _pack_bf16_pairs(x):
    half = x.shape[1] // 2
    hi = lax.bitcast_convert_type(x[:, :half].astype(BF16).astype(F32), U32)
    lo = lax.bitcast_convert_type(x[:, half:].astype(BF16).astype(F32), U32)
    return hi | (lo >> 16)


def _unpack_bf16_pairs(u):
    hi = lax.bitcast_convert_type(u & jnp.uint32(0xFFFF0000), F32)
    lo = lax.bitcast_convert_type(u << 16, F32)
    return jnp.concatenate([hi, lo], axis=1).astype(BF16)


def _first_argmax(vals, idx, axis):
    m = jnp.max(vals, axis=axis, keepdims=True)
    first = jnp.min(jnp.where(vals == m, idx, BIG_IDX), axis=axis, keepdims=True)
    return m, first


def _top_k_mask(gate, idx, axis):
    sel = jnp.zeros(gate.shape, F32)
    for _ in range(MOBA_TOPK):
        m, first = _first_argmax(gate, idx, axis)
        pick = jnp.logical_and(idx == first, m > NEG_INF)
        sel = jnp.where(pick, 1.0, sel)
        gate = jnp.where(pick, NEG_INF, gate)
    return sel


def _rot_t(xt_ref, cos_t, sin_t, n_heads):
    half = ROT_DIM // 2
    for hh in range(n_heads):
        r0 = hh * HEAD_DIM
        xa = xt_ref[0, r0:r0 + half, :]
        xb = xt_ref[0, r0 + half:r0 + ROT_DIM, :]
        xt_ref[0, r0:r0 + half, :] = xa * cos_t - xb * sin_t
        xt_ref[0, r0 + half:r0 + ROT_DIM, :] = xb * cos_t + xa * sin_t


def _memkv_kernel(mem_ref, g_ref, wt_ref, out_ref):
    h = _rms(mem_ref[0], g_ref[0]).astype(BF16)
    out_ref[0, 0] = lax.dot_general(wt_ref[0], h, NT_DIMS, preferred_element_type=F32)


def _memkv(mem, g_mem, w_mem_kv_t_bf):
    b, m, d = mem.shape
    depth = g_mem.shape[0]
    n_out = w_mem_kv_t_bf.shape[1]
    return pl.pallas_call(
        _memkv_kernel,
        out_shape=jax.ShapeDtypeStruct((depth, b, n_out, m), F32),
        grid=(depth, b),
        in_specs=[
            pl.BlockSpec((1, m, d), lambda l, i: (i, 0, 0)),
            pl.BlockSpec((1, 1, d), lambda l, i: (l, 0, 0)),
            pl.BlockSpec((1, n_out, d), lambda l, i: (l, 0, 0)),
        ],
        out_specs=pl.BlockSpec((1, 1, n_out, m), lambda l, i: (l, i, 0, 0)),
        compiler_params=pltpu.CompilerParams(dimension_semantics=("arbitrary", "arbitrary")),
        name="memkv",
    )(mem, g_mem.reshape(depth, 1, d), w_mem_kv_t_bf)


def _mem_attn_into(mix, off, qm, mkt_ref, mvt_ref):
    mkt = mkt_ref[0, 0].astype(BF16)
    mvt = mvt_ref[0, 0].astype(BF16)
    scale = HEAD_DIM ** -0.5
    lane = lax.broadcasted_iota(I32, qm.shape, 1)
    om = jnp.zeros(qm.shape, F32)
    for hh in range(MEM_H):
        in_head = (lane // HEAD_DIM) == hh
        q = jnp.where(in_head, qm, 0.0).astype(BF16)
        s = jnp.dot(q, mkt, preferred_element_type=F32) * scale
        m = jnp.max(s, axis=-1, keepdims=True)
        e = jnp.exp(s - m)
        l = jnp.sum(e, axis=-1, keepdims=True)
        o = lax.dot_general(e.astype(BF16), mvt, NT_DIMS, preferred_element_type=F32) / l
        om = jnp.where(in_head, o, om)
    mix[:, off:off + D_MEMQ] = om.astype(BF16)


def _out_and_route(x, mix, wout_ref, gffn_ref, wr_ref, br_ref, x1_ref, hf_ref, gate_ref, eid_ref):
    x1 = x + jnp.dot(mix[...], wout_ref[...], preferred_element_type=F32)
    x1_ref[...] = x1
    hf = _rms(x1, gffn_ref[...])
    hf_ref[...] = _pack_bf16_pairs(hf)
    hf_hi = hf.astype(BF16)
    hf_lo = (hf - hf_hi.astype(F32)).astype(BF16)
    logits = jnp.dot(jnp.concatenate([hf_hi, hf_lo, hf_hi], axis=1), wr_ref[...],
                     preferred_element_type=F32) + br_ref[...]
    lane = lax.broadcasted_iota(I32, logits.shape, 1)
    gl = jnp.where((lane >= N_EXPERTS) & (lane < N_EXPERTS + N_GROUPS), logits, NEG_INF)
    gmax, glane = _first_argmax(gl, lane, -1)
    p_group = 1.0 / jnp.sum(jnp.exp(gl - gmax), axis=-1, keepdims=True)
    e_lo = (glane - N_EXPERTS) * E_PER_GROUP
    el = jnp.where((lane >= e_lo) & (lane < e_lo + E_PER_GROUP), logits, NEG_INF)
    m1, i1 = _first_argmax(el, lane, -1)
    el2 = jnp.where(lane == i1, NEG_INF, el)
    m2, i2 = _first_argmax(el2, lane, -1)
    e2 = jnp.exp(m2 - m1)
    g1 = p_group / (1.0 + e2)
    g2 = g1 * e2
    lane2 = lax.broadcasted_iota(I32, gate_ref.shape, 1)
    gate_ref[...] = jnp.where(lane2 == 0, g1, g2)
    eid_ref[...] = jnp.where(lane2 == 0, i1, i2)


def _router_weights(w_rg, b_rg, w_re, b_re):
    d = w_rg.shape[0]
    pad = ROUTER_LANES - N_EXPERTS - N_GROUPS
    wr = jnp.concatenate([w_re, w_rg, jnp.zeros((d, pad), F32)], axis=1)
    br = jnp.concatenate([b_re, b_rg, jnp.zeros((pad,), F32)]).reshape(1, ROUTER_LANES)
    w_hi = wr.astype(BF16)
    w_lo = (wr - w_hi.astype(F32)).astype(BF16)
    return jnp.concatenate([w_hi, w_hi, w_lo], axis=0), br


def _layer0_kernel(x_ref, ctx_ref, mk_ref, mv_ref, gmix_ref, win_ref, cw_ref, cb_ref, lg_ref, lb_ref,
                   wout_ref, gffn_ref, wr_ref, br_ref,
                   x1_ref, hf_ref, gate_ref, eid_ref, nc_ref,
                   zext, zsh, cbuf, mix, *, ts, rc, d_conv):
    t = pl.program_id(1)
    nt = pl.num_programs(1)

    @pl.when(t == 0)
    def _():
        zext[0:CTX_PAD, :] = jnp.zeros((CTX_PAD, d_conv), F32)
        zext[CTX_PAD:CTX_ROWS, :] = ctx_ref[0]

    @pl.when(t > 0)
    def _():
        zext[0:CTX_ROWS, :] = zext[ts:ts + CTX_ROWS, :]

    x = x_ref[...]
    h = _rms(x, gmix_ref[...]).astype(BF16)
    u = jnp.dot(h, win_ref[...], preferred_element_type=F32)
    z = u[:, :d_conv] * jax.nn.sigmoid(u[:, d_conv:2 * d_conv])
    zext[CTX_ROWS:CTX_ROWS + ts, :] = z
    qm = u[:, 2 * d_conv:]

    @pl.when(t == nt - 1)
    def _():
        nc_ref[0] = zext[ts + CTX_PAD: ts + CTX_ROWS, :]

    for sh in range(1, SUBLANE):
        zsh[sh - 1] = zext[sh:sh + ts + CTX_ROWS - SUBLANE, :]

    def chunk(ci, carry):
        r0 = pl.multiple_of(ci * rc, SUBLANE)
        for cbk in range(d_conv // LANE):
            cs = slice(cbk * LANE, (cbk + 1) * LANE)
            acc = jnp.zeros((rc, LANE), F32) + cb_ref[:, cs]
            for w in range(CONV_W):
                hi, sh = divmod(w + CTX_PAD, SUBLANE)
                if sh == 0:
                    win = zext[pl.ds(r0 + hi * SUBLANE, rc), cs]
                else:
                    win = zsh[sh - 1, pl.ds(r0 + hi * SUBLANE, rc), cs]
                acc = acc + win * cw_ref[w:w + 1, cs]
            cbuf[pl.ds(r0, rc), cs] = acc
        return carry

    lax.fori_loop(0, ts // rc, chunk, 0)

    c = cbuf[...]
    mu = jnp.mean(c, axis=-1, keepdims=True)
    xc = c - mu
    y = xc * lax.rsqrt(jnp.mean(xc * xc, axis=-1, keepdims=True) + EPS) * lg_ref[...] + lb_ref[...]
    mix[:, :d_conv] = (y * jax.nn.sigmoid(y)).astype(BF16)
    _mem_attn_into(mix, d_conv, qm, mk_ref, mv_ref)
    _out_and_route(x, mix, wout_ref, gffn_ref, wr_ref, br_ref, x1_ref, hf_ref, gate_ref, eid_ref)


def _layer0(x2d, b, s, ctx, mk_arr, mv_arr, kv_rows, l, P, ts):
    n, d = x2d.shape
    nt = s // ts
    d_conv = ctx.shape[-1]
    rc = min(32, ts)
    d_in = P["w_in_a_bf"].shape[-1]
    krow, vrow = kv_rows
    const = lambda *shape: pl.BlockSpec(shape, lambda i, t: (0,) * len(shape))
    tok = lambda w: pl.BlockSpec((ts, w), lambda i, t: (i * nt + t, 0))
    kern = functools.partial(_layer0_kernel, ts=ts, rc=rc, d_conv=d_conv)
    return pl.pallas_call(
        kern,
        out_shape=(
            jax.ShapeDtypeStruct((n, d), F32),
            jax.ShapeDtypeStruct((n, d // 2), U32),
            jax.ShapeDtypeStruct((n, 2), F32),
            jax.ShapeDtypeStruct((n, 2), I32),
            jax.ShapeDtypeStruct((b, CONV_W - 1, d_conv), F32),
        ),
        grid=(b, nt),
        in_specs=[
            tok(d),
            pl.BlockSpec((1, CONV_W - 1, d_conv), lambda i, t: (i, 0, 0)),
            pl.BlockSpec((1, 1, D_MEMQ, mk_arr.shape[3]), lambda i, t: (l, i, krow, 0)),
            pl.BlockSpec((1, 1, D_MEMQ, mv_arr.shape[3]), lambda i, t: (l, i, vrow, 0)),
            const(1, d), const(d, d_in), const(CONV_W, d_conv), const(1, d_conv), const(1, d_conv),
            const(1, d_conv), const(d_conv + D_MEMQ, d), const(1, d), const(3 * d, ROUTER_LANES),
            const(1, ROUTER_LANES),
        ],
        out_specs=(tok(d), tok(d // 2), tok(2), tok(2),
                   pl.BlockSpec((1, CONV_W - 1, d_conv), lambda i, t: (i, 0, 0))),
        scratch_shapes=[
            pltpu.VMEM((ts + CTX_ROWS, d_conv), F32),
            pltpu.VMEM((SUBLANE - 1, ts + CTX_ROWS - SUBLANE, d_conv), F32),
            pltpu.VMEM((ts, d_conv), F32),
            pltpu.VMEM((ts, d_conv + D_MEMQ), BF16),
        ],
        compiler_params=pltpu.CompilerParams(
            dimension_semantics=("arbitrary", "arbitrary"), vmem_limit_bytes=VMEM_LIMIT),
        name="layer0",
    )(x2d, ctx, mk_arr, mv_arr, P["g_mix"][l].reshape(1, d), P["w_in_a_bf"], P["conv_w"][0],
      P["conv_b"][0].reshape(1, d_conv), P["cln_g"][0].reshape(1, d_conv), P["cln_b"][0].reshape(1, d_conv),
      P["w_out_a_bf"], P["g_ffn"][l].reshape(1, d), P["wr"][l], P["br"][l])


def _layer1_kernel(x_ref, c_ref, qm_ref, mk_ref, mv_ref, wout_ref, gffn_ref, wr_ref, br_ref,
                   x1_ref, hf_ref, gate_ref, eid_ref, mix, *, d_att):
    mix[:, :d_att] = c_ref[...].astype(BF16)
    _mem_attn_into(mix, d_att, qm_ref[...], mk_ref, mv_ref)
    _out_and_route(x_ref[...], mix, wout_ref, gffn_ref, wr_ref, br_ref, x1_ref, hf_ref, gate_ref, eid_ref)


def _layer1(x2d, c2d, qm2d, b, s, mk_arr, mv_arr, kv_rows, l, P, ts):
    n, d = x2d.shape
    nt = s // ts
    d_att = c2d.shape[-1]
    krow, vrow = kv_rows
    const = lambda *shape: pl.BlockSpec(shape, lambda i, t: (0,) * len(shape))
    tok = lambda w: pl.BlockSpec((ts, w), lambda i, t: (i * nt + t, 0))
    return pl.pallas_call(
        functools.partial(_layer1_kernel, d_att=d_att),
        out_shape=(
            jax.ShapeDtypeStruct((n, d), F32),
            jax.ShapeDtypeStruct((n, d // 2), U32),
            jax.ShapeDtypeStruct((n, 2), F32),
            jax.ShapeDtypeStruct((n, 2), I32),
        ),
        grid=(b, nt),
        in_specs=[
            tok(d), tok(d_att), tok(D_MEMQ),
            pl.BlockSpec((1, 1, D_MEMQ, mk_arr.shape[3]), lambda i, t: (l, i, krow, 0)),
            pl.BlockSpec((1, 1, D_MEMQ, mv_arr.shape[3]), lambda i, t: (l, i, vrow, 0)),
            const(d_att + D_MEMQ, d), const(1, d), const(3 * d, ROUTER_LANES), const(1, ROUTER_LANES),
        ],
        out_specs=(tok(d), tok(d // 2), tok(2), tok(2)),
        scratch_shapes=[pltpu.VMEM((ts, d_att + D_MEMQ), BF16)],
        compiler_params=pltpu.CompilerParams(
            dimension_semantics=("arbitrary", "arbitrary"), vmem_limit_bytes=VMEM_LIMIT),
        name="layer1",
    )(x2d, c2d, qm2d, mk_arr, mv_arr, P["w_out_b_bf"], P["g_ffn"][l].reshape(1, d), P["wr"][l], P["br"][l])


def _plan_kernel(eid_ref, dest_ref, counts_ref, cnt, base, pstart, *, ta):
    ph = pl.program_id(0)
    t = pl.program_id(1)
    lane = lax.broadcasted_iota(I32, (ta, LANE), 1)
    onehot = eid_ref[...] == lane
    ohf = jnp.where(onehot, 1.0, 0.0)
    tile_counts = jnp.sum(ohf, axis=0, keepdims=True)

    @pl.when(ph == 0)
    def _():
        @pl.when(t == 0)
        def _():
            cnt[...] = jnp.zeros((1, LANE), F32)
        cnt[...] = cnt[...] + tile_counts

    @pl.when(ph == 1)
    def _():
        @pl.when(t == 0)
        def _():
            ci = cnt[...].astype(I32)
            counts_ref[...] = ci
            pc = ((ci + (MOE_BLK - 1)) // MOE_BLK) * MOE_BLK
            lane1 = lax.broadcasted_iota(I32, (1, LANE), 1)
            incl = pc
            sh = 1
            while sh < LANE:
                incl = incl + jnp.where(lane1 >= sh, pltpu.roll(incl, sh, 1), 0)
                sh *= 2
            pstart[...] = (incl - pc).astype(F32)
            base[...] = jnp.zeros((1, LANE), F32)

        r = lax.broadcasted_iota(I32, (ta, ta), 0)
        c = lax.broadcasted_iota(I32, (ta, ta), 1)
        tri = jnp.where(r >= c, 1.0, 0.0).astype(BF16)
        prefix = jnp.dot(tri, ohf.astype(BF16), preferred_element_type=F32)
        slot = pstart[...] + base[...] + prefix - 1.0
        dest_ref[...] = jnp.sum(jnp.where(onehot, slot, 0.0), axis=1, keepdims=True).astype(I32)
        base[...] = base[...] + tile_counts


def _plan(eid_col):
    a = eid_col.shape[0]
    ta = max(t for t in range(LANE, PLAN_TILE_MAX + 1, LANE) if a % t == 0)
    return pl.pallas_call(
        functools.partial(_plan_kernel, ta=ta),
        out_shape=(jax.ShapeDtypeStruct((a, 1), I32), jax.ShapeDtypeStruct((1, LANE), I32)),
        grid=(2, a // ta),
        in_specs=[pl.BlockSpec((ta, 1), lambda ph, t: (t, 0))],
        out_specs=(pl.BlockSpec((ta, 1), lambda ph, t: (t * ph, 0)),
                   pl.BlockSpec((1, LANE), lambda ph, t: (0, 0))),
        scratch_shapes=[pltpu.VMEM((1, LANE), F32), pltpu.VMEM((1, LANE), F32), pltpu.VMEM((1, LANE), F32)],
        compiler_params=pltpu.CompilerParams(dimension_semantics=("arbitrary", "arbitrary")),
        name="moe_plan",
    )(eid_col)


def _dest_tiles(dest, tm):
    n = dest.shape[0]
    return dest.reshape(n // tm, tm, 2).transpose(0, 2, 1).reshape(n // tm, 2 * tm)


def _dispatch_kernel(pend_ref, nused_ref, dest_hbm, *rest, tm, starts, n_blk):
    n_src = len(starts) - 1
    hf_refs = rest[:n_src]
    xs_out, idx_smem, sem_idx, sem_rows, zbuf, sem_zero = rest[n_src:]
    i = pl.program_id(0)
    n = pl.num_programs(0)
    slot = lax.rem(i, 2)

    def idx_copy(k, sl):
        return pltpu.make_async_copy(dest_hbm.at[k], idx_smem.at[sl], sem_idx.at[sl])

    @pl.when(i == 0)
    def _():
        idx_copy(0, 0).start()
        zbuf[...] = jnp.zeros(zbuf.shape, zbuf.dtype)

        def zero_block(row0):
            return pltpu.make_async_copy(zbuf, xs_out.at[pl.ds(pl.multiple_of(row0, MOE_BLK), MOE_BLK)], sem_zero)

        def nonempty(e):
            return pend_ref[e] > jnp.where(e == 0, 0, pend_ref[jnp.maximum(e - 1, 0)])

        def visit(fn):
            def expert(e, carry):
                @pl.when(nonempty(e))
                def _():
                    fn(zero_block(pend_ref[e] - MOE_BLK))
                return carry

            def unused(b, carry):
                fn(zero_block(b * MOE_BLK))
                return carry

            lax.fori_loop(0, N_EXPERTS, expert, 0)
            lax.fori_loop(nused_ref[0], n_blk, unused, 0)

        visit(lambda cp: cp.start())
        visit(lambda cp: cp.wait())

    @pl.when(i + 1 < n)
    def _():
        idx_copy(i + 1, 1 - slot).start()

    idx_copy(i, slot).wait()

    for g in range(n_src):
        hf_ref = hf_refs[g]

        @pl.when(jnp.logical_and(i >= starts[g], i < starts[g + 1]))
        def _():
            def body(c, carry):
                for u in range(GATHER_UNROLL):
                    r = c * GATHER_UNROLL + u
                    for k in range(2):
                        row = idx_smem[slot, k * tm + r]
                        pltpu.make_async_copy(hf_ref.at[pl.ds(r, 1)], xs_out.at[pl.ds(row, 1)], sem_rows).start()
                return carry
            lax.fori_loop(0, tm // GATHER_UNROLL, body, 0)
            for _ in range(2):
                pltpu.make_async_copy(hf_ref, xs_out.at[pl.ds(0, tm)], sem_rows).wait()


def _dispatch(hfs, dests, pends, nused, n_blk, tm):
    w = hfs[0].shape[1]
    tiles = [hf.shape[0] // tm for hf in hfs]
    starts = [sum(tiles[:g]) for g in range(len(hfs) + 1)]

    def src_spec(g):
        return pl.BlockSpec((tm, w), lambda i, pe, nu: (jnp.clip(i - starts[g], 0, tiles[g] - 1), 0))

    return pl.pallas_call(
        functools.partial(_dispatch_kernel, tm=tm, starts=tuple(starts), n_blk=n_blk),
        out_shape=jax.ShapeDtypeStruct((n_blk * MOE_BLK, w), hfs[0].dtype),
        grid_spec=pltpu.PrefetchScalarGridSpec(
            num_scalar_prefetch=2,
            grid=(starts[-1],),
            in_specs=[pl.BlockSpec(memory_space=pl.ANY)] + [src_spec(g) for g in range(len(hfs))],
            out_specs=pl.BlockSpec(memory_space=pl.ANY),
            scratch_shapes=[
                pltpu.SMEM((2, 2 * tm), I32),
                pltpu.SemaphoreType.DMA((2,)),
                pltpu.SemaphoreType.DMA,
                pltpu.VMEM((MOE_BLK, w), hfs[0].dtype),
                pltpu.SemaphoreType.DMA,
            ],
        ),
        compiler_params=pltpu.CompilerParams(
            dimension_semantics=("arbitrary",), vmem_limit_bytes=VMEM_LIMIT),
        name="moe_dispatch",
    )(pends, nused, jnp.concatenate([_dest_tiles(d, tm) for d in dests], axis=0), *hfs)


def _experts_kernel(bexp_ref, nused_ref, xs_ref, w1_ref, w3_ref, w2_ref, out_ref, w13, w2s, *, d_exp):
    i = pl.program_id(0)
    nused = nused_ref[0]

    @pl.when(i < nused)
    def _():
        changed = jnp.logical_or(i == 0, bexp_ref[i] != bexp_ref[jnp.maximum(i - 1, 0)])

        @pl.when(changed)
        def _():
            w13[:, :d_exp] = w1_ref[0, 0].astype(BF16)
            w13[:, d_exp:] = w3_ref[0, 0].astype(BF16)
            w2s[...] = w2_ref[0, 0].astype(BF16)

        hcat = jnp.dot(_unpack_bf16_pairs(xs_ref[...]), w13[...], preferred_element_type=F32)
        a = hcat[:, :d_exp]
        act = (a * jax.nn.sigmoid(a)) * hcat[:, d_exp:]
        out_ref[...] = jnp.dot(act.astype(BF16), w2s[...], preferred_element_type=F32)

    @pl.when(i >= nused)
    def _():
        out_ref[...] = jnp.zeros(out_ref.shape, F32)


def _experts(xs, blk_exp, nused, l, w1, w3, w2):
    p_rows, dw = xs.shape
    d, d_exp = w1.shape[-2:]
    assert dw * 2 == d
    n_blk = p_rows // MOE_BLK
    wspec = lambda r, c: pl.BlockSpec((1, 1, r, c), lambda i, be, nu: (l, be[jnp.minimum(i, nu[0] - 1)], 0, 0))
    return pl.pallas_call(
        functools.partial(_experts_kernel, d_exp=d_exp),
        out_shape=jax.ShapeDtypeStruct((p_rows, d), F32),
        grid_spec=pltpu.PrefetchScalarGridSpec(
            num_scalar_prefetch=2,
            grid=(n_blk,),
            in_specs=[
                pl.BlockSpec((MOE_BLK, dw), lambda i, be, nu: (jnp.minimum(i, nu[0] - 1), 0)),
                wspec(d, d_exp), wspec(d, d_exp), wspec(d_exp, d),
            ],
            out_specs=pl.BlockSpec((MOE_BLK, d), lambda i, be, nu: (i, 0)),
            scratch_shapes=[
                pltpu.VMEM((d, 2 * d_exp), BF16),
                pltpu.VMEM((d_exp, d), BF16),
            ],
        ),
        compiler_params=pltpu.CompilerParams(
            dimension_semantics=("arbitrary",), vmem_limit_bytes=VMEM_LIMIT),
        name="moe_experts",
    )(blk_exp, nused, xs, w1, w3, w2)


def _moe_layer(l, groups, P):
    eid_all = jnp.concatenate([g[1] for g in groups], axis=0)
    n_all = eid_all.shape[0]
    a = 2 * n_all
    assert a % MOE_BLK == 0
    n_blk = a // MOE_BLK + N_EXPERTS
    dest_col, counts = _plan(eid_all.reshape(a, 1))
    counts = counts[0, :N_EXPERTS]
    pends = jnp.cumsum((counts + MOE_BLK - 1) // MOE_BLK * MOE_BLK)
    blk_start = jnp.arange(n_blk, dtype=I32) * MOE_BLK
    blk_exp = jnp.minimum(jnp.sum((pends[None, :] <= blk_start[:, None]).astype(I32), axis=1), N_EXPERTS - 1)
    nused = (pends[-1] // MOE_BLK).astype(I32).reshape(1)
    dest = dest_col.reshape(n_all, 2)
    dests = []
    off = 0
    for hf, _, _ in groups:
        dests.append(dest[off:off + hf.shape[0]])
        off += hf.shape[0]
    tm = groups[0][2]
    assert all(g[2] == tm for g in groups)
    xs = _dispatch([g[0] for g in groups], dests, pends.astype(I32), nused, n_blk, tm)
    yr = _experts(xs, blk_exp.astype(I32), nused, l, P["w1"], P["w3"], P["w2"])
    return yr, dests


def _gather_step(step, nsteps, idx_hbm, src_hbm, idx_smem, buf, sem_idx, sem_rows, nrows):
    slot = lax.rem(step, 2)
    nslot = 1 - slot

    def idx_copy(k, sl):
        return pltpu.make_async_copy(idx_hbm.at[k], idx_smem.at[sl], sem_idx.at[sl])

    def issue(sl):
        def body(g, carry):
            for u in range(GATHER_UNROLL):
                r = g * GATHER_UNROLL + u
                pltpu.make_async_copy(src_hbm.at[pl.ds(idx_smem[sl, r], 1)], buf.at[sl, pl.ds(r, 1)],
                                      sem_rows.at[sl]).start()
            return carry
        lax.fori_loop(0, nrows // GATHER_UNROLL, body, 0)

    @pl.when(step == 0)
    def _():
        first = idx_copy(0, 0)
        first.start()
        first.wait()
        issue(0)

        @pl.when(nsteps > 1)
        def _():
            idx_copy(1, 1).start()

    @pl.when(step + 1 < nsteps)
    def _():
        idx_copy(step + 1, nslot).wait()
        issue(nslot)

        @pl.when(step + 2 < nsteps)
        def _():
            idx_copy(step + 2, slot).start()

    pltpu.make_async_copy(src_hbm.at[pl.ds(0, nrows)], buf.at[slot], sem_rows.at[slot]).wait()
    return slot


def _combine(step, nsteps, dest_hbm, yr_hbm, x1_ref, gate_ref, idx_smem, ybuf, sem_idx, sem_rows, tm):
    slot = _gather_step(step, nsteps, dest_hbm, yr_hbm, idx_smem, ybuf, sem_idx, sem_rows, 2 * tm)
    g = gate_ref[...]
    y = g[:, 0:1] * ybuf[slot, 0:tm, :] + g[:, 1:2] * ybuf[slot, tm:2 * tm, :]
    return x1_ref[...] + y


def _mid_kernel(dest_hbm, yr_hbm, x1_ref, gate_ref, cos_ref, sin_ref,
                gkv_ref, wkvt_ref, gmix_ref, wqt_ref, wqm_ref,
                x2_ref, kt_ref, vt_ref, qt_ref, qm_ref,
                idx_smem, ybuf, sem_idx, sem_rows, *, tm, d_att):
    i = pl.program_id(0)
    x2 = _combine(i, pl.num_programs(0), dest_hbm, yr_hbm, x1_ref, gate_ref,
                  idx_smem, ybuf, sem_idx, sem_rows, tm)
    x2_ref[...] = x2
    n_heads = d_att // HEAD_DIM
    cos_t = cos_ref[...]
    sin_t = sin_ref[...]
    hk = _rms(x2, gkv_ref[...]).astype(BF16)
    kvt = lax.dot_general(wkvt_ref[...], hk, NT_DIMS, preferred_element_type=F32)
    vt_ref[0] = kvt[d_att:, :]
    kt_ref[0] = kvt[:d_att, :]
    _rot_t(kt_ref, cos_t, sin_t, n_heads)
    hq = _rms(x2, gmix_ref[...]).astype(BF16)
    qt_ref[0] = lax.dot_general(wqt_ref[...], hq, NT_DIMS, preferred_element_type=F32)
    _rot_t(qt_ref, cos_t, sin_t, n_heads)
    qm_ref[...] = jnp.dot(hq, wqm_ref[...], preferred_element_type=F32)


def _mid(x1, gate, yr, dest, pos, n_seq, P, tm):
    n, d = x1.shape
    d_att = P["w_kv_t_bf"].shape[0] // 2
    nsteps = n // tm
    s_out = n // n_seq
    nt = s_out // tm
    half = ROT_DIM // 2
    reps = s_out // pos.shape[0]
    inv = ROPE_THETA ** (-jnp.arange(half, dtype=F32) * (2.0 / ROT_DIM))
    ang_t = jnp.tile(inv[:, None] * pos.astype(F32)[None, :], (1, reps))
    const = lambda *shape: pl.BlockSpec(shape, lambda i: (0,) * len(shape))
    tok = lambda w: pl.BlockSpec((tm, w), lambda i: (i, 0))
    tbl_t = pl.BlockSpec((half, tm), lambda i: (0, i % nt))
    feat_t = pl.BlockSpec((1, d_att, tm), lambda i: (i // nt, 0, i % nt))
    feat_shape = jax.ShapeDtypeStruct((n_seq, d_att, s_out), F32)
    return pl.pallas_call(
        functools.partial(_mid_kernel, tm=tm, d_att=d_att),
        out_shape=(jax.ShapeDtypeStruct((n, d), F32), feat_shape, feat_shape, feat_shape,
                   jax.ShapeDtypeStruct((n, D_MEMQ), F32)),
        grid=(nsteps,),
        in_specs=[
            pl.BlockSpec(memory_space=pl.ANY), pl.BlockSpec(memory_space=pl.ANY),
            tok(d), tok(2), tbl_t, tbl_t,
            const(1, d), const(2 * d_att, d), const(1, d), const(d_att, d), const(d, D_MEMQ),
        ],
        out_specs=(tok(d), feat_t, feat_t, feat_t, tok(D_MEMQ)),
        scratch_shapes=[
            pltpu.SMEM((2, 2 * tm), I32),
            pltpu.VMEM((2, 2 * tm, d), F32),
            pltpu.SemaphoreType.DMA((2,)),
            pltpu.SemaphoreType.DMA((2,)),
        ],
        compiler_params=pltpu.CompilerParams(
            dimension_semantics=("arbitrary",), vmem_limit_bytes=VMEM_LIMIT),
        name="mid",
    )(_dest_tiles(dest, tm), yr, x1, gate, jnp.cos(ang_t), jnp.sin(ang_t),
      P["g_kv"].reshape(1, d), P["w_kv_t_bf"], P["g_mix"][1].reshape(1, d), P["w_q_t_bf"], P["w_qm_bf"])


def _final_kernel(dest_hbm, yr_hbm, x1_ref, gate_ref, g_ref, y_ref, idx_smem, ybuf, sem_idx, sem_rows, *, tm):
    i = pl.program_id(0)
    x = _combine(i, pl.num_programs(0), dest_hbm, yr_hbm, x1_ref, gate_ref,
                 idx_smem, ybuf, sem_idx, sem_rows, tm)
    y_ref[...] = _rms(x, g_ref[...])


def _final(x1, gate, yr, dest, g_final, tm):
    n, d = x1.shape
    tok = lambda w: pl.BlockSpec((tm, w), lambda i: (i, 0))
    return pl.pallas_call(
        functools.partial(_final_kernel, tm=tm),
        out_shape=jax.ShapeDtypeStruct((n, d), F32),
        grid=(n // tm,),
        in_specs=[
            pl.BlockSpec(memory_space=pl.ANY), pl.BlockSpec(memory_space=pl.ANY),
            tok(d), tok(2), pl.BlockSpec((1, d), lambda i: (0, 0)),
        ],
        out_specs=tok(d),
        scratch_shapes=[
            pltpu.SMEM((2, 2 * tm), I32),
            pltpu.VMEM((2, 2 * tm, d), F32),
            pltpu.SemaphoreType.DMA((2,)),
            pltpu.SemaphoreType.DMA((2,)),
        ],
        compiler_params=pltpu.CompilerParams(
            dimension_semantics=("arbitrary",), vmem_limit_bytes=VMEM_LIMIT),
        name="final",
    )(_dest_tiles(dest, tm), yr, x1, gate, g_final.reshape(1, d))


def _moba_prompt_kernel(qt_ref, kt_ref, vt_ref, o_ref, kmean, krows, vones, sel_sc, s_buf, p_buf, acc,
                        *, s_len, n_grp):
    qb_i = pl.program_id(2)
    nb = s_len // MOBA_BLOCK
    n_h = LANE // HEAD_DIM
    tq = MOBA_BLOCK
    nq = n_h * tq
    scale = HEAD_DIM ** -0.5
    feat = lax.broadcasted_iota(I32, (LANE, MOBA_BLOCK), 0) // HEAD_DIM
    groups = range(n_grp)

    @pl.when(qb_i == 0)
    def _():
        kmean[...] = jnp.zeros(kmean.shape, F32)
        for g in groups:
            rows = slice(g * LANE, (g + 1) * LANE)
            for jb in range(nb):
                cols = slice(jb * MOBA_BLOCK, (jb + 1) * MOBA_BLOCK)
                kr = kt_ref[0, rows, cols].T
                kmean[g, jb:jb + 1, :] = jnp.sum(kr, axis=0, keepdims=True) * (1.0 / MOBA_BLOCK)
                krows[g, jb] = kr.astype(BF16)
                vblk = vt_ref[0, rows, cols]
                for hh in range(n_h):
                    vones[g, hh, jb] = jnp.where(feat == hh, vblk, 1.0).astype(BF16)

    key = lax.broadcasted_iota(I32, (MOBA_BLOCK, nq), 0)
    qcol = lax.broadcasted_iota(I32, (MOBA_BLOCK, nq), 1) % tq
    qs, m0, p_own = [], [], []
    for g in groups:
        qt = qt_ref[0, g * LANE:(g + 1) * LANE, :]
        qstack = jnp.concatenate([jnp.where(feat == hh, qt, 0.0) for hh in range(n_h)], axis=1)
        gate = jnp.dot(kmean[g], qstack, preferred_element_type=F32, precision=HIGHEST)
        brow = lax.broadcasted_iota(I32, gate.shape, 0)
        gate = jnp.where(brow < qb_i, gate, NEG_INF)
        sel = _top_k_mask(gate, brow, 0)
        for jb in range(nb):
            sel_sc[g, jb] = sel[jb:jb + 1, :]
        qs.append((qstack * scale).astype(BF16))
        s_own = jnp.where(key <= qcol, jnp.dot(krows[g, qb_i], qs[g], preferred_element_type=F32), NEG_INF)
        m0.append(jnp.max(s_own, axis=0, keepdims=True))
        p_own.append(jnp.exp(s_own - m0[g]).astype(BF16))

    def put_scores(g, j, slot):
        s_buf[g, slot] = jnp.dot(krows[g, j], qs[g], preferred_element_type=F32)

    def add_block(g, alpha, j, slot):
        for hh in range(n_h):
            acc[g, hh] = (alpha[:, hh * tq:(hh + 1) * tq] * acc[g, hh]
                          + jnp.dot(vones[g, hh, j], p_buf[g, slot, :, hh * tq:(hh + 1) * tq],
                                    preferred_element_type=F32))

    for g in groups:
        acc[g] = jnp.zeros((n_h, LANE, tq), F32)
        p_buf[g, 1] = p_own[g]
        put_scores(g, 0, 0)

    def half_trip(j, slot, carry):
        out = []
        for g in groups:
            m, alpha_prev = carry[g]
            add_block(g, alpha_prev, jnp.where(j == 0, qb_i, j - 1), 1 - slot)
            put_scores(g, jnp.minimum(j + 1, nb - 1), 1 - slot)
            chosen = sel_sc[g, j] > 0.0
            m_new = jnp.maximum(m, jnp.where(chosen, jnp.max(s_buf[g, slot], axis=0, keepdims=True), NEG_INF))
            p_buf[g, slot] = jnp.exp(s_buf[g, slot] - jnp.where(chosen, m_new, jnp.inf)).astype(BF16)
            out.append((m_new, jnp.exp(m - m_new)))
        return tuple(out)

    def body(t, carry):
        return half_trip(2 * t + 1, 1, half_trip(2 * t, 0, carry))

    n_trips = (qb_i + 1) // 2
    one = jnp.ones((1, nq), F32)
    state = lax.fori_loop(0, n_trips, body, tuple((m0[g], one) for g in groups))
    last = jnp.where(n_trips == 0, qb_i, 2 * n_trips - 1)
    for g in groups:
        add_block(g, state[g][1], last, 1)
        out_t = jnp.zeros((LANE, tq), F32)
        for hh in range(n_h):
            other = (1 - hh) * HEAD_DIM
            a_h = acc[g, hh]
            out_t = jnp.where(feat == hh, a_h / a_h[other:other + 1, :], out_t)
        o_ref[:, g * LANE:(g + 1) * LANE] = out_t.T


def _moba_prompt(qt, kt, vt, b, s, n_grp=2):
    d_att = qt.shape[1]
    nb = s // MOBA_BLOCK
    fw = n_grp * LANE
    assert s % MOBA_BLOCK == 0 and LANE // HEAD_DIM == 2 and d_att % fw == 0
    nb_pad = -(-nb // SUBLANE) * SUBLANE
    kv_spec = pl.BlockSpec((1, fw, s), lambda i, hp, qb: (i, hp, 0))
    return pl.pallas_call(
        functools.partial(_moba_prompt_kernel, s_len=s, n_grp=n_grp),
        out_shape=jax.ShapeDtypeStruct((b * s, d_att), F32),
        grid=(b, d_att // fw, nb),
        in_specs=[
            pl.BlockSpec((1, fw, MOBA_BLOCK), lambda i, hp, qb: (i, hp, qb)),
            kv_spec, kv_spec,
        ],
        out_specs=pl.BlockSpec((MOBA_BLOCK, fw), lambda i, hp, qb: (i * nb + qb, hp)),
        scratch_shapes=[
            pltpu.VMEM((n_grp, nb_pad, LANE), F32),
            pltpu.VMEM((n_grp, nb, MOBA_BLOCK, LANE), BF16),
            pltpu.VMEM((n_grp, 2, nb, LANE, MOBA_BLOCK), BF16),
            pltpu.VMEM((n_grp, nb, 1, 2 * MOBA_BLOCK), F32),
            pltpu.VMEM((n_grp, 2, MOBA_BLOCK, 2 * MOBA_BLOCK), F32),
            pltpu.VMEM((n_grp, 2, MOBA_BLOCK, 2 * MOBA_BLOCK), BF16),
            pltpu.VMEM((n_grp, 2, LANE, MOBA_BLOCK), F32),
        ],
        compiler_params=pltpu.CompilerParams(
            dimension_semantics=("arbitrary", "arbitrary", "arbitrary"), vmem_limit_bytes=VMEM_LIMIT),
        name="moba_prompt",
    )(qt, kt, vt)


def _head_diag(o, n_heads, ds):
    return jnp.concatenate(
        [o[h * ds:(h + 1) * ds, h * HEAD_DIM:(h + 1) * HEAD_DIM] for h in range(n_heads)], axis=0)


def _moba_sample_kernel(pt_ref, q_ref, kn_ref, vn_ref, *rest, ds, n_heads, nb, bps):
    pages = rest[:4 * bps]
    o_ref, qbd, g_all, m_all, l_all, o_all = rest[4 * bps:]
    jg = pl.program_id(1)
    rows = n_heads * ds
    d_att = n_heads * HEAD_DIM
    scale = HEAD_DIM ** -0.5
    blane = lax.broadcasted_iota(I32, (rows, LANE), 1)

    @pl.when(jg == 0)
    def _():
        q = q_ref[...]
        lane = lax.broadcasted_iota(I32, (ds, d_att), 1)
        for h in range(n_heads):
            qbd[h * ds:(h + 1) * ds, :] = jnp.where(lane // HEAD_DIM == h, q, 0.0)
        g_all[...] = jnp.full((rows, LANE), NEG_INF, F32)
        m_all[...] = jnp.zeros((rows, LANE), F32)
        l_all[...] = jnp.zeros((rows, LANE), F32)

    qb = qbd[...].astype(BF16)
    g_new, m_new, l_new = g_all[...], m_all[...], l_all[...]
    for bi in range(bps):
        k0_ref, k1_ref = pages[2 * bi], pages[2 * bi + 1]
        v0_ref, v1_ref = pages[2 * bps + 2 * bi], pages[2 * bps + 2 * bi + 1]
        j = jg * bps + bi
        kt = jnp.concatenate([k0_ref[0].astype(BF16), k1_ref[0].astype(BF16)], axis=1)
        vt = jnp.concatenate([v0_ref[0].astype(BF16), v1_ref[0].astype(BF16)], axis=1)
        s = jnp.dot(qb, kt, preferred_element_type=F32)
        g = jnp.sum(s, axis=-1, keepdims=True) * (1.0 / MOBA_BLOCK)
        m = jnp.max(s, axis=-1, keepdims=True) * scale
        p = jnp.exp(s * scale - m)
        l = jnp.sum(p, axis=-1, keepdims=True)
        o = lax.dot_general(p.astype(BF16), vt, NT_DIMS, preferred_element_type=F32)
        o_all[j] = _head_diag(o, n_heads, ds)
        here = blane == j
        g_new = jnp.where(here, g, g_new)
        m_new = jnp.where(here, m, m_new)
        l_new = jnp.where(here, l, l_new)
    g_all[...] = g_new
    m_all[...] = m_new
    l_all[...] = l_new

    @pl.when(jg == nb // bps - 1)
    def _():
        chosen = _top_k_mask(g_new, blane, -1) > 0.0
        s_own = lax.dot_general(qb, kn_ref[...].astype(BF16), NT_DIMS, preferred_element_type=F32) * scale
        r_i = lax.broadcasted_iota(I32, (rows, ds), 0) % ds
        c_i = lax.broadcasted_iota(I32, (rows, ds), 1)
        s_own = jnp.where(c_i <= r_i, s_own, NEG_INF)
        m_sel = jnp.max(jnp.where(chosen, m_new, NEG_INF), axis=-1, keepdims=True)
        m_tot = jnp.maximum(m_sel, jnp.max(s_own, axis=-1, keepdims=True))
        wgt = jnp.where(chosen, jnp.exp(m_new - m_tot), 0.0)
        p_own = jnp.exp(s_own - m_tot)
        l_tot = jnp.sum(wgt * l_new, axis=-1, keepdims=True) + jnp.sum(p_own, axis=-1, keepdims=True)
        acc = _head_diag(jnp.dot(p_own.astype(BF16), vn_ref[...].astype(BF16), preferred_element_type=F32),
                         n_heads, ds)
        for jj in range(nb):
            acc = acc + wgt[:, jj:jj + 1] * o_all[jj]
        o_ref[0] = acc / l_tot


def _moba_sample(qb, k_new, v_new, cache_k, cache_v, page_table, db, ds, bps=4):
    n_pool, page, n_heads, hd = cache_k.shape
    d_att = n_heads * hd
    n_pages = page_table.shape[1]
    ppb = MOBA_BLOCK // page
    nb = n_pages // ppb
    rows = n_heads * ds
    assert ppb == 2 and n_pages % ppb == 0 and ds <= MOBA_BLOCK and nb <= LANE and nb % bps == 0
    ck = cache_k.transpose(0, 2, 3, 1).reshape(n_pool, d_att, page)
    cv = cache_v.transpose(0, 2, 3, 1).reshape(n_pool, d_att, page)
    seq = pl.BlockSpec((ds, d_att), lambda i, j, pt: (i, 0))
    ppg = ppb * bps
    pg = lambda off: pl.BlockSpec((1, d_att, page), lambda i, j, pt: (pt[i * n_pages + ppg * j + off], 0, 0))
    out = pl.pallas_call(
        functools.partial(_moba_sample_kernel, ds=ds, n_heads=n_heads, nb=nb, bps=bps),
        out_shape=jax.ShapeDtypeStruct((db, rows, hd), F32),
        grid_spec=pltpu.PrefetchScalarGridSpec(
            num_scalar_prefetch=1,
            grid=(db, nb // bps),
            in_specs=[seq, seq, seq] + [pg(o) for o in range(ppg)] * 2,
            out_specs=pl.BlockSpec((1, rows, hd), lambda i, j, pt: (i, 0, 0)),
            scratch_shapes=[
                pltpu.VMEM((rows, d_att), F32),
                pltpu.VMEM((rows, LANE), F32),
                pltpu.VMEM((rows, LANE), F32),
                pltpu.VMEM((rows, LANE), F32),
                pltpu.VMEM((nb, rows, hd), F32),
            ],
        ),
        compiler_params=pltpu.CompilerParams(
            dimension_semantics=("arbitrary", "arbitrary"), vmem_limit_bytes=VMEM_LIMIT),
        name="moba_sample",
    )(page_table.reshape(-1), qb, k_new, v_new, *([ck] * ppg), *([cv] * ppg))
    return out.reshape(db, n_heads, ds, hd).transpose(0, 2, 1, 3).reshape(db * ds, d_att)


def kernel(x_prompt, x_sample, state_conv, cache_k, cache_v, cache_mem_k, cache_mem_v, page_table,
           mem_prompt, g_mix, g_ffn, g_final, g_mem, w_mem_kv, w_in_a, conv_w, conv_b, cln_g, cln_b,
           w_out_a, g_kv, w_kv, w_in_b, w_out_b, w_rg, b_rg, w_re, b_re, w1, w3, w2):
    depth = g_mix.shape[0]
    n_heads = w_kv.shape[1] // (2 * HEAD_DIM)
    d_att = n_heads * HEAD_DIM
    routers = [_router_weights(w_rg[l], b_rg[l], w_re[l], b_re[l]) for l in range(depth)]
    P = {
        "g_mix": g_mix, "g_ffn": g_ffn, "g_final": g_final, "g_kv": g_kv,
        "w_in_a_bf": w_in_a[0].astype(BF16), "w_out_a_bf": w_out_a[0].astype(BF16),
        "w_kv_t_bf": w_kv.T.astype(BF16),
        "w_q_t_bf": w_in_b[0][:, :d_att].T.astype(BF16), "w_qm_bf": w_in_b[0][:, d_att:].astype(BF16),
        "w_out_b_bf": w_out_b[0].astype(BF16),
        "conv_w": conv_w, "conv_b": conv_b, "cln_g": cln_g, "cln_b": cln_b,
        "wr": [r[0] for r in routers], "br": [r[1] for r in routers],
        "w1": w1, "w3": w3, "w2": w2,
    }
    b_p, s_p, d = x_prompt.shape
    db, ds, _ = x_sample.shape
    n_p, n_s = b_p * s_p, db * ds
    mem_len = mem_prompt.shape[1]
    past = page_table.shape[1] * cache_k.shape[1]
    assert past % MOBA_BLOCK == 0
    ts_p, tm_p = 256, 256
    ts_s, tm_s = ds, n_s

    memkv_t = _memkv(mem_prompt, g_mem, w_mem_kv.transpose(0, 2, 1).astype(BF16))
    kv6 = memkv_t.reshape(depth, b_p, 2, MEM_H, HEAD_DIM, mem_len).transpose(2, 0, 1, 5, 3, 4)
    mem_k_p, mem_v_p = kv6[0], kv6[1]
    cmk = cache_mem_k.transpose(0, 1, 3, 4, 2).reshape(depth, db, D_MEMQ, mem_len)
    cmv = cache_mem_v.transpose(0, 1, 3, 4, 2).reshape(depth, db, D_MEMQ, mem_len)
    mem_p = (memkv_t, memkv_t, (0, 1))
    mem_s = (cmk, cmv, (0, 0))

    ctx_p = jnp.zeros((b_p, CONV_W - 1, state_conv.shape[-1]), F32)
    x1_p, hf_p, gate_p, eid_p, conv_p = _layer0(x_prompt.reshape(n_p, d), b_p, s_p, ctx_p, *mem_p, 0, P, ts_p)
    x1_s, hf_s, gate_s, eid_s, conv_s = _layer0(x_sample.reshape(n_s, d), db, ds, state_conv[0], *mem_s, 0, P, ts_s)
    yr, (dest_p, dest_s) = _moe_layer(0, [(hf_p, eid_p, tm_p), (hf_s, eid_s, tm_s)], P)

    x2_p, kt_p, vt_p, qt_p, qm_p = _mid(x1_p, gate_p, yr, dest_p, jnp.arange(s_p), b_p, P, tm_p)
    x2_s, kt_s, vt_s, qt_s, qm_s = _mid(x1_s, gate_s, yr, dest_s, past + jnp.arange(ds), 1, P, tm_s)
    c1_p = _moba_prompt(qt_p, kt_p, vt_p, b_p, s_p)
    k_new, v_new = kt_s[0].T, vt_s[0].T
    c1_s = _moba_sample(qt_s[0].T, k_new, v_new, cache_k, cache_v, page_table, db, ds)
    x3_p, hf_p, gate_p, eid_p = _layer1(x2_p, c1_p, qm_p, b_p, s_p, *mem_p, 1, P, ts_p)
    x3_s, hf_s, gate_s, eid_s = _layer1(x2_s, c1_s, qm_s, db, ds, *mem_s, 1, P, ts_s)
    yr, (dest_p, dest_s) = _moe_layer(1, [(hf_p, eid_p, tm_p), (hf_s, eid_s, tm_s)], P)
    y_p = _final(x3_p, gate_p, yr, dest_p, g_final, tm_p).reshape(b_p, s_p, d)
    y_s = _final(x3_s, gate_s, yr, dest_s, g_final, tm_s).reshape(db, ds, d)

    k_p = kt_p.reshape(b_p, n_heads, HEAD_DIM, s_p).transpose(0, 3, 1, 2)
    v_p = vt_p.reshape(b_p, n_heads, HEAD_DIM, s_p).transpose(0, 3, 1, 2)
    k_s = k_new.reshape(db, ds, n_heads, HEAD_DIM)
    v_s = v_new.reshape(db, ds, n_heads, HEAD_DIM)
    return (y_p, y_s, conv_p[None], conv_s[None], k_p, v_p, k_s, v_s, mem_k_p, mem_v_p)
```

```python
import functools

import jax
import jax.numpy as jnp
from jax import lax
from jax.experimental import pallas as pl
from jax.experimental.pallas import tpu as pltpu

F32 = jnp.float32
BF16 = jnp.bfloat16
I32 = jnp.int32
U32 = jnp.uint32
HIGHEST = lax.Precision.HIGHEST

EPS = 1e-6
HEAD_DIM = 64
MEM_H = 4
D_MEMQ = MEM_H * HEAD_DIM
CONV_W = 31
CTX_ROWS = 32
CTX_PAD = CTX_ROWS - (CONV_W - 1)
MOBA_BLOCK = 256
MOBA_TOPK = 3
ROT_DIM = HEAD_DIM // 4
ROPE_THETA = 500000.0
N_GROUPS = 4
E_PER_GROUP = 16
N_EXPERTS = N_GROUPS * E_PER_GROUP
MOE_BLK = 256
PLAN_TILE_MAX = 2048
GATHER_UNROLL = 8
DISPATCH_TILE = 128
ROUTER_LANES = 128
LANE = 128
SUBLANE = 8
VMEM_LIMIT = 56 * 1024 * 1024

NEG_INF = float("-inf")
BIG_IDX = 1 << 20
NT_DIMS = (((1,), (1,)), ((), ()))


def _rms(x, g):
    return x * lax.rsqrt(jnp.mean(x * x, axis=-1, keepdims=True) + EPS) * g


def _pack_bf16_pairs(x):
    half = x.shape[1] // 2
    hi = lax.bitcast_convert_type(x[:, :half].astype(BF16).astype(F32), U32)
    lo = lax.bitcast_convert_type(x[:, half:].astype(BF16).astype(F32), U32)
    return hi | (lo >> 16)


def _unpack_bf16_pairs(u):
    hi = lax.bitcast_convert_type(u & jnp.uint32(0xFFFF0000), F32)
    lo = lax.bitcast_convert_type(u << 16, F32)
    return jnp.concatenate([hi, lo], axis=1).astype(BF16)


def _first_argmax(vals, idx, axis):
    m = jnp.max(vals, axis=axis, keepdims=True)
    first = jnp.min(jnp.where(vals == m, idx, BIG_IDX), axis=axis, keepdims=True)
    return m, first


def _top_k_mask(gate, idx, axis):
    sel = jnp.zeros(gate.shape, F32)
    for _ in range(MOBA_TOPK):
        m, first = _first_argmax(gate, idx, axis)
        pick = jnp.logical_and(idx == first, m > NEG_INF)
        sel = jnp.where(pick, 1.0, sel)
        gate = jnp.where(pick, NEG_INF, gate)
    return sel


def _rot_t(xt_ref, tok, cos_t, sin_t, n_heads):
    half = ROT_DIM // 2
    for hh in range(n_heads):
        r0 = hh * HEAD_DIM
        xa = xt_ref[0, r0:r0 + half, tok]
        xb = xt_ref[0, r0 + half:r0 + ROT_DIM, tok]
        xt_ref[0, r0:r0 + half, tok] = xa * cos_t - xb * sin_t
        xt_ref[0, r0 + half:r0 + ROT_DIM, tok] = xb * cos_t + xa * sin_t


def _memkv_kernel(mem_ref, g_ref, wt_ref, out_ref):
    h = _rms(mem_ref[0], g_ref[0]).astype(BF16)
    out_ref[0, 0] = lax.dot_general(wt_ref[0], h, NT_DIMS, preferred_element_type=F32)


def _memkv(mem, g_mem, w_mem_kv_t_bf):
    b, m, d = mem.shape
    depth = g_mem.shape[0]
    n_out = w_mem_kv_t_bf.shape[1]
    return pl.pallas_call(
        _memkv_kernel,
        out_shape=jax.ShapeDtypeStruct((depth, b, n_out, m), F32),
        grid=(depth, b),
        in_specs=[
            pl.BlockSpec((1, m, d), lambda l, i: (i, 0, 0)),
            pl.BlockSpec((1, 1, d), lambda l, i: (l, 0, 0)),
            pl.BlockSpec((1, n_out, d), lambda l, i: (l, 0, 0)),
        ],
        out_specs=pl.BlockSpec((1, 1, n_out, m), lambda l, i: (l, i, 0, 0)),
        compiler_params=pltpu.CompilerParams(dimension_semantics=("arbitrary", "arbitrary")),
        name="memkv",
    )(mem, g_mem.reshape(depth, 1, d), w_mem_kv_t_bf)


def _mem_attn_into(mix, off, qm, mkt_ref, mvt_ref):
    mkt = mkt_ref[0, 0].astype(BF16)
    mvt = mvt_ref[0, 0].astype(BF16)
    scale = HEAD_DIM ** -0.5
    lane = lax.broadcasted_iota(I32, qm.shape, 1)
    om = jnp.zeros(qm.shape, F32)
    for hh in range(MEM_H):
        in_head = (lane // HEAD_DIM) == hh
        q = jnp.where(in_head, qm, 0.0).astype(BF16)
        s = jnp.dot(q, mkt, preferred_element_type=F32) * scale
        m = jnp.max(s, axis=-1, keepdims=True)
        e = jnp.exp(s - m)
        l = jnp.sum(e, axis=-1, keepdims=True)
        o = lax.dot_general(e.astype(BF16), mvt, NT_DIMS, preferred_element_type=F32) / l
        om = jnp.where(in_head, o, om)
    mix[:, off:off + D_MEMQ] = om.astype(BF16)


def _out_and_route(x, mix, wout_ref, gffn_ref, wr_ref, br_ref, x1_ref, hf_ref, gate_ref, eid_ref):
    x1 = x + jnp.dot(mix[...], wout_ref[...], preferred_element_type=F32)
    x1_ref[...] = x1
    hf = _rms(x1, gffn_ref[...])
    hf_ref[...] = _pack_bf16_pairs(hf)
    hf_hi = hf.astype(BF16)
    hf_lo = (hf - hf_hi.astype(F32)).astype(BF16)
    logits = jnp.dot(jnp.concatenate([hf_hi, hf_lo, hf_hi], axis=1), wr_ref[...],
                     preferred_element_type=F32) + br_ref[...]
    lane = lax.broadcasted_iota(I32, logits.shape, 1)
    gl = jnp.where((lane >= N_EXPERTS) & (lane < N_EXPERTS + N_GROUPS), logits, NEG_INF)
    gmax, glane = _first_argmax(gl, lane, -1)
    p_group = 1.0 / jnp.sum(jnp.exp(gl - gmax), axis=-1, keepdims=True)
    e_lo = (glane - N_EXPERTS) * E_PER_GROUP
    el = jnp.where((lane >= e_lo) & (lane < e_lo + E_PER_GROUP), logits, NEG_INF)
    m1, i1 = _first_argmax(el, lane, -1)
    el2 = jnp.where(lane == i1, NEG_INF, el)
    m2, i2 = _first_argmax(el2, lane, -1)
    e2 = jnp.exp(m2 - m1)
    g1 = p_group / (1.0 + e2)
    g2 = g1 * e2
    lane2 = lax.broadcasted_iota(I32, gate_ref.shape, 1)
    gate_ref[...] = jnp.where(lane2 == 0, g1, g2)
    eid_ref[...] = jnp.where(lane2 == 0, i1, i2)


def _router_weights(w_rg, b_rg, w_re, b_re):
    d = w_rg.shape[0]
    pad = ROUTER_LANES - N_EXPERTS - N_GROUPS
    wr = jnp.concatenate([w_re, w_rg, jnp.zeros((d, pad), F32)], axis=1)
    br = jnp.concatenate([b_re, b_rg, jnp.zeros((pad,), F32)]).reshape(1, ROUTER_LANES)
    w_hi = wr.astype(BF16)
    w_lo = (wr - w_hi.astype(F32)).astype(BF16)
    return jnp.concatenate([w_hi, w_hi, w_lo], axis=0), br


def _layer0_kernel(x_ref, ctx_ref, mk_ref, mv_ref, gmix_ref, win_ref, cw_ref, cb_ref, lg_ref, lb_ref,
                   wout_ref, gffn_ref, wr_ref, br_ref,
                   x1_ref, hf_ref, gate_ref, eid_ref, nc_ref,
                   zext, zsh, cbuf, mix, *, ts, rc, d_conv):
    t = pl.program_id(1)
    nt = pl.num_programs(1)

    @pl.when(t == 0)
    def _():
        zext[0:CTX_PAD, :] = jnp.zeros((CTX_PAD, d_conv), F32)
        zext[CTX_PAD:CTX_ROWS, :] = ctx_ref[0]

    @pl.when(t > 0)
    def _():
        zext[0:CTX_ROWS, :] = zext[ts:ts + CTX_ROWS, :]

    x = x_ref[...]
    h = _rms(x, gmix_ref[...]).astype(BF16)
    u = jnp.dot(h, win_ref[...], preferred_element_type=F32)
    z = u[:, :d_conv] * jax.nn.sigmoid(u[:, d_conv:2 * d_conv])
    zext[CTX_ROWS:CTX_ROWS + ts, :] = z
    qm = u[:, 2 * d_conv:]

    @pl.when(t == nt - 1)
    def _():
        nc_ref[0] = zext[ts + CTX_PAD: ts + CTX_ROWS, :]

    for sh in range(1, SUBLANE):
        zsh[sh - 1] = zext[sh:sh + ts + CTX_ROWS - SUBLANE, :]

    def chunk(ci, carry):
        r0 = pl.multiple_of(ci * rc, SUBLANE)
        for cbk in range(d_conv // LANE):
            cs = slice(cbk * LANE, (cbk + 1) * LANE)
            acc = jnp.zeros((rc, LANE), F32) + cb_ref[:, cs]
            for w in range(CONV_W):
                hi, sh = divmod(w + CTX_PAD, SUBLANE)
                if sh == 0:
                    win = zext[pl.ds(r0 + hi * SUBLANE, rc), cs]
                else:
                    win = zsh[sh - 1, pl.ds(r0 + hi * SUBLANE, rc), cs]
                acc = acc + win * cw_ref[w:w + 1, cs]
            cbuf[pl.ds(r0, rc), cs] = acc
        return carry

    lax.fori_loop(0, ts // rc, chunk, 0)

    c = cbuf[...]
    mu = jnp.mean(c, axis=-1, keepdims=True)
    xc = c - mu
    y = xc * lax.rsqrt(jnp.mean(xc * xc, axis=-1, keepdims=True) + EPS) * lg_ref[...] + lb_ref[...]
    mix[:, :d_conv] = (y * jax.nn.sigmoid(y)).astype(BF16)
    _mem_attn_into(mix, d_conv, qm, mk_ref, mv_ref)
    _out_and_route(x, mix, wout_ref, gffn_ref, wr_ref, br_ref, x1_ref, hf_ref, gate_ref, eid_ref)


def _layer0(x2d, b, s, ctx, mk_arr, mv_arr, kv_rows, l, P, ts):
    n, d = x2d.shape
    nt = s // ts
    d_conv = ctx.shape[-1]
    rc = min(32, ts)
    d_in = P["w_in_a_bf"].shape[-1]
    krow, vrow = kv_rows
    const = lambda *shape: pl.BlockSpec(shape, lambda i, t: (0,) * len(shape))
    tok = lambda w: pl.BlockSpec((ts, w), lambda i, t: (i * nt + t, 0))
    kern = functools.partial(_layer0_kernel, ts=ts, rc=rc, d_conv=d_conv)
    return pl.pallas_call(
        kern,
        out_shape=(
            jax.ShapeDtypeStruct((n, d), F32),
            jax.ShapeDtypeStruct((n, d // 2), U32),
            jax.ShapeDtypeStruct((n, 2), F32),
            jax.ShapeDtypeStruct((n, 2), I32),
            jax.ShapeDtypeStruct((b, CONV_W - 1, d_conv), F32),
        ),
        grid=(b, nt),
        in_specs=[
            tok(d),
            pl.BlockSpec((1, CONV_W - 1, d_conv), lambda i, t: (i, 0, 0)),
            pl.BlockSpec((1, 1, D_MEMQ, mk_arr.shape[3]), lambda i, t: (l, i, krow, 0)),
            pl.BlockSpec((1, 1, D_MEMQ, mv_arr.shape[3]), lambda i, t: (l, i, vrow, 0)),
            const(1, d), const(d, d_in), const(CONV_W, d_conv), const(1, d_conv), const(1, d_conv),
            const(1, d_conv), const(d_conv + D_MEMQ, d), const(1, d), const(3 * d, ROUTER_LANES),
            const(1, ROUTER_LANES),
        ],
        out_specs=(tok(d), tok(d // 2), tok(2), tok(2),
                   pl.BlockSpec((1, CONV_W - 1, d_conv), lambda i, t: (i, 0, 0))),
        scratch_shapes=[
            pltpu.VMEM((ts + CTX_ROWS, d_conv), F32),
            pltpu.VMEM((SUBLANE - 1, ts + CTX_ROWS - SUBLANE, d_conv), F32),
            pltpu.VMEM((ts, d_conv), F32),
            pltpu.VMEM((ts, d_conv + D_MEMQ), BF16),
        ],
        compiler_params=pltpu.CompilerParams(
            dimension_semantics=("arbitrary", "arbitrary"), vmem_limit_bytes=VMEM_LIMIT),
        name="layer0",
    )(x2d, ctx, mk_arr, mv_arr, P["g_mix"][l].reshape(1, d), P["w_in_a_bf"], P["conv_w"][0],
      P["conv_b"][0].reshape(1, d_conv), P["cln_g"][0].reshape(1, d_conv), P["cln_b"][0].reshape(1, d_conv),
      P["w_out_a_bf"], P["g_ffn"][l].reshape(1, d), P["wr"][l], P["br"][l])


def _layer1_kernel(x_ref, c_ref, qm_ref, mk_ref, mv_ref, wout_ref, gffn_ref, wr_ref, br_ref,
                   x1_ref, hf_ref, gate_ref, eid_ref, mix, *, d_att):
    mix[:, :d_att] = c_ref[...].astype(BF16)
    _mem_attn_into(mix, d_att, qm_ref[...], mk_ref, mv_ref)
    _out_and_route(x_ref[...], mix, wout_ref, gffn_ref, wr_ref, br_ref, x1_ref, hf_ref, gate_ref, eid_ref)


def _layer1(x2d, c2d, qm2d, b, s, mk_arr, mv_arr, kv_rows, l, P, ts):
    n, d = x2d.shape
    nt = s // ts
    d_att = c2d.shape[-1]
    krow, vrow = kv_rows
    const = lambda *shape: pl.BlockSpec(shape, lambda i, t: (0,) * len(shape))
    tok = lambda w: pl.BlockSpec((ts, w), lambda i, t: (i * nt + t, 0))
    return pl.pallas_call(
        functools.partial(_layer1_kernel, d_att=d_att),
        out_shape=(
            jax.ShapeDtypeStruct((n, d), F32),
            jax.ShapeDtypeStruct((n, d // 2), U32),
            jax.ShapeDtypeStruct((n, 2), F32),
            jax.ShapeDtypeStruct((n, 2), I32),
        ),
        grid=(b, nt),
        in_specs=[
            tok(d), tok(d_att), tok(D_MEMQ),
            pl.BlockSpec((1, 1, D_MEMQ, mk_arr.shape[3]), lambda i, t: (l, i, krow, 0)),
            pl.BlockSpec((1, 1, D_MEMQ, mv_arr.shape[3]), lambda i, t: (l, i, vrow, 0)),
            const(d_att + D_MEMQ, d), const(1, d), const(3 * d, ROUTER_LANES), const(1, ROUTER_LANES),
        ],
        out_specs=(tok(d), tok(d // 2), tok(2), tok(2)),
        scratch_shapes=[pltpu.VMEM((ts, d_att + D_MEMQ), BF16)],
        compiler_params=pltpu.CompilerParams(
            dimension_semantics=("arbitrary", "arbitrary"), vmem_limit_bytes=VMEM_LIMIT),
        name="layer1",
    )(x2d, c2d, qm2d, mk_arr, mv_arr, P["w_out_b_bf"], P["g_ffn"][l].reshape(1, d), P["wr"][l], P["br"][l])


def _plan_kernel(eid_ref, dest_ref, counts_ref, cnt, base, pstart, *, ta):
    ph = pl.program_id(0)
    t = pl.program_id(1)
    lane = lax.broadcasted_iota(I32, (ta, LANE), 1)
    onehot = eid_ref[...] == lane
    ohf = jnp.where(onehot, 1.0, 0.0)
    tile_counts = jnp.sum(ohf, axis=0, keepdims=True)

    @pl.when(ph == 0)
    def _():
        @pl.when(t == 0)
        def _():
            cnt[...] = jnp.zeros((1, LANE), F32)
        cnt[...] = cnt[...] + tile_counts

    @pl.when(ph == 1)
    def _():
        @pl.when(t == 0)
        def _():
            ci = cnt[...].astype(I32)
            counts_ref[...] = ci
            pc = ((ci + (MOE_BLK - 1)) // MOE_BLK) * MOE_BLK
            lane1 = lax.broadcasted_iota(I32, (1, LANE), 1)
            incl = pc
            sh = 1
            while sh < LANE:
                incl = incl + jnp.where(lane1 >= sh, pltpu.roll(incl, sh, 1), 0)
                sh *= 2
            pstart[...] = (incl - pc).astype(F32)
            base[...] = jnp.zeros((1, LANE), F32)

        r = lax.broadcasted_iota(I32, (ta, ta), 0)
        c = lax.broadcasted_iota(I32, (ta, ta), 1)
        tri = jnp.where(r >= c, 1.0, 0.0).astype(BF16)
        prefix = jnp.dot(tri, ohf.astype(BF16), preferred_element_type=F32)
        slot = pstart[...] + base[...] + prefix - 1.0
        dest_ref[...] = jnp.sum(jnp.where(onehot, slot, 0.0), axis=1, keepdims=True).astype(I32)
        base[...] = base[...] + tile_counts


def _plan(eid_col):
    a = eid_col.shape[0]
    ta = max(t for t in range(LANE, PLAN_TILE_MAX + 1, LANE) if a % t == 0)
    return pl.pallas_call(
        functools.partial(_plan_kernel, ta=ta),
        out_shape=(jax.ShapeDtypeStruct((a, 1), I32), jax.ShapeDtypeStruct((1, LANE), I32)),
        grid=(2, a // ta),
        in_specs=[pl.BlockSpec((ta, 1), lambda ph, t: (t, 0))],
        out_specs=(pl.BlockSpec((ta, 1), lambda ph, t: (t * ph, 0)),
                   pl.BlockSpec((1, LANE), lambda ph, t: (0, 0))),
        scratch_shapes=[pltpu.VMEM((1, LANE), F32), pltpu.VMEM((1, LANE), F32), pltpu.VMEM((1, LANE), F32)],
        compiler_params=pltpu.CompilerParams(dimension_semantics=("arbitrary", "arbitrary")),
        name="moe_plan",
    )(eid_col)


def _dest_tiles(dest, tm):
    n = dest.shape[0]
    return dest.reshape(n // tm, tm, 2).transpose(0, 2, 1).reshape(n // tm, 2 * tm)


def _dispatch_kernel(pend_ref, nused_ref, dest_hbm, *rest, tm, starts, n_blk):
    n_src = len(starts) - 1
    hf_refs = rest[:n_src]
    xs_out, idx_smem, sem_idx, sem_rows, zbuf, sem_zero = rest[n_src:]
    i = pl.program_id(0)
    n_tiles = 2 * pl.num_programs(0)

    def idx_copy(k, sl):
        return pltpu.make_async_copy(dest_hbm.at[k], idx_smem.at[sl], sem_idx.at[sl])

    @pl.when(i == 0)
    def _():
        idx_copy(0, 0).start()
        zbuf[...] = jnp.zeros(zbuf.shape, zbuf.dtype)

        def zero_block(row0):
            return pltpu.make_async_copy(zbuf, xs_out.at[pl.ds(pl.multiple_of(row0, MOE_BLK), MOE_BLK)], sem_zero)

        def nonempty(e):
            return pend_ref[e] > jnp.where(e == 0, 0, pend_ref[jnp.maximum(e - 1, 0)])

        def visit(fn):
            def expert(e, carry):
                @pl.when(nonempty(e))
                def _():
                    fn(zero_block(pend_ref[e] - MOE_BLK))
                return carry

            def unused(b, carry):
                fn(zero_block(b * MOE_BLK))
                return carry

            lax.fori_loop(0, N_EXPERTS, expert, 0)
            lax.fori_loop(nused_ref[0], n_blk, unused, 0)

        visit(lambda cp: cp.start())
        visit(lambda cp: cp.wait())

    for g in range(n_src):
        hf_ref = hf_refs[g]

        @pl.when(jnp.logical_and(i >= starts[g], i < starts[g + 1]))
        def _():
            for par in range(2):
                t = 2 * i + par
                idx_copy(t, par).wait()

                @pl.when(t + 1 < n_tiles)
                def _():
                    idx_copy(t + 1, 1 - par).start()

                for r in range(tm):
                    for k in range(2):
                        row = idx_smem[par, k * tm + r]
                        pltpu.make_async_copy(hf_ref.at[pl.ds(par * tm + r, 1)], xs_out.at[pl.ds(row, 1)],
                                              sem_rows).start()
            for _ in range(2):
                pltpu.make_async_copy(hf_ref, xs_out.at[pl.ds(0, 2 * tm)], sem_rows).wait()


def _dispatch(hfs, dests, pends, nused, n_blk, tm):
    w = hfs[0].shape[1]
    assert all(hf.shape[0] % (2 * tm) == 0 for hf in hfs)
    tiles = [hf.shape[0] // (2 * tm) for hf in hfs]
    starts = [sum(tiles[:g]) for g in range(len(hfs) + 1)]

    def src_spec(g):
        return pl.BlockSpec((2 * tm, w), lambda i, pe, nu: (jnp.clip(i - starts[g], 0, tiles[g] - 1), 0))

    return pl.pallas_call(
        functools.partial(_dispatch_kernel, tm=tm, starts=tuple(starts), n_blk=n_blk),
        out_shape=jax.ShapeDtypeStruct((n_blk * MOE_BLK, w), hfs[0].dtype),
        grid_spec=pltpu.PrefetchScalarGridSpec(
            num_scalar_prefetch=2,
            grid=(starts[-1],),
            in_specs=[pl.BlockSpec(memory_space=pl.ANY)] + [src_spec(g) for g in range(len(hfs))],
            out_specs=pl.BlockSpec(memory_space=pl.ANY),
            scratch_shapes=[
                pltpu.SMEM((2, 2 * tm), I32),
                pltpu.SemaphoreType.DMA((2,)),
                pltpu.SemaphoreType.DMA,
                pltpu.VMEM((MOE_BLK, w), hfs[0].dtype),
                pltpu.SemaphoreType.DMA,
            ],
        ),
        compiler_params=pltpu.CompilerParams(
            dimension_semantics=("arbitrary",), vmem_limit_bytes=VMEM_LIMIT),
        name="moe_dispatch",
    )(pends, nused, jnp.concatenate([_dest_tiles(d, tm) for d in dests], axis=0), *hfs)


def _experts_kernel(bexp_ref, nused_ref, xs_ref, w1_ref, w3_ref, w2_ref, out_ref, w13, w2s, *, d_exp):
    i = pl.program_id(0)
    nused = nused_ref[0]

    @pl.when(i < nused)
    def _():
        changed = jnp.logical_or(i == 0, bexp_ref[i] != bexp_ref[jnp.maximum(i - 1, 0)])

        @pl.when(changed)
        def _():
            w13[:, :d_exp] = w1_ref[0, 0].astype(BF16)
            w13[:, d_exp:] = w3_ref[0, 0].astype(BF16)
            w2s[...] = w2_ref[0, 0].astype(BF16)

        hcat = jnp.dot(_unpack_bf16_pairs(xs_ref[...]), w13[...], preferred_element_type=F32)
        a = hcat[:, :d_exp]
        act = (a * jax.nn.sigmoid(a)) * hcat[:, d_exp:]
        out_ref[...] = jnp.dot(act.astype(BF16), w2s[...], preferred_element_type=F32)

    @pl.when(i >= nused)
    def _():
        out_ref[...] = jnp.zeros(out_ref.shape, F32)


def _experts(xs, blk_exp, nused, l, w1, w3, w2):
    p_rows, dw = xs.shape
    d, d_exp = w1.shape[-2:]
    assert dw * 2 == d
    n_blk = p_rows // MOE_BLK
    wspec = lambda r, c: pl.BlockSpec((1, 1, r, c), lambda i, be, nu: (l, be[jnp.minimum(i, nu[0] - 1)], 0, 0))
    return pl.pallas_call(
        functools.partial(_experts_kernel, d_exp=d_exp),
        out_shape=jax.ShapeDtypeStruct((p_rows, d), F32),
        grid_spec=pltpu.PrefetchScalarGridSpec(
            num_scalar_prefetch=2,
            grid=(n_blk,),
            in_specs=[
                pl.BlockSpec((MOE_BLK, dw), lambda i, be, nu: (jnp.minimum(i, nu[0] - 1), 0)),
                wspec(d, d_exp), wspec(d, d_exp), wspec(d_exp, d),
            ],
            out_specs=pl.BlockSpec((MOE_BLK, d), lambda i, be, nu: (i, 0)),
            scratch_shapes=[
                pltpu.VMEM((d, 2 * d_exp), BF16),
                pltpu.VMEM((d_exp, d), BF16),
            ],
        ),
        compiler_params=pltpu.CompilerParams(
            dimension_semantics=("arbitrary",), vmem_limit_bytes=VMEM_LIMIT),
        name="moe_experts",
    )(blk_exp, nused, xs, w1, w3, w2)


def _moe_layer(l, groups, P):
    eid_all = jnp.concatenate([g[1] for g in groups], axis=0)
    n_all = eid_all.shape[0]
    a = 2 * n_all
    assert a % MOE_BLK == 0
    n_blk = a // MOE_BLK + N_EXPERTS
    dest_col, counts = _plan(eid_all.reshape(a, 1))
    counts = counts[0, :N_EXPERTS]
    pends = jnp.cumsum((counts + MOE_BLK - 1) // MOE_BLK * MOE_BLK)
    blk_start = jnp.arange(n_blk, dtype=I32) * MOE_BLK
    blk_exp = jnp.minimum(jnp.sum((pends[None, :] <= blk_start[:, None]).astype(I32), axis=1), N_EXPERTS - 1)
    nused = (pends[-1] // MOE_BLK).astype(I32).reshape(1)
    dest = dest_col.reshape(n_all, 2)
    dests = []
    off = 0
    for hf, _ in groups:
        dests.append(dest[off:off + hf.shape[0]])
        off += hf.shape[0]
    xs = _dispatch([g[0] for g in groups], dests, pends.astype(I32), nused, n_blk, DISPATCH_TILE)
    yr = _experts(xs, blk_exp.astype(I32), nused, l, P["w1"], P["w3"], P["w2"])
    return yr, dests


def _gather_begin(step, slot, nsteps, idx_hbm, src_hbm, idx_smem, buf, sem_idx, sem_rows, nrows, n_chunks):
    nslot = 1 - slot
    last = nsteps - 1
    nxt = jnp.minimum(step + 1, last)
    nxt2 = jnp.minimum(step + 2, last)
    per = nrows // n_chunks

    def idx_copy(k, sl):
        return pltpu.make_async_copy(idx_hbm.at[k], idx_smem.at[sl], sem_idx.at[sl])

    def row_copy(sl, r):
        return pltpu.make_async_copy(src_hbm.at[pl.ds(idx_smem[sl, r], 1)], buf.at[sl, pl.ds(r, 1)],
                                     sem_rows.at[sl])

    def rows_wait(sl):
        pltpu.make_async_copy(src_hbm.at[pl.ds(0, nrows)], buf.at[sl], sem_rows.at[sl]).wait()

    @pl.when(step == 0)
    def _():
        first = idx_copy(0, 0)
        first.start()
        first.wait()

        def body(g, carry):
            for u in range(GATHER_UNROLL):
                row_copy(0, g * GATHER_UNROLL + u).start()
            return carry
        lax.fori_loop(0, nrows // GATHER_UNROLL, body, 0)
        idx_copy(nxt, 1).start()

    idx_copy(nxt, nslot).wait()
    rows_wait(slot)
    idx_copy(nxt2, slot).start()

    def issue_chunk(c):
        for r in range(c * per, (c + 1) * per):
            row_copy(nslot, r).start()

    def finish():
        @pl.when(step == last)
        def _():
            rows_wait(nslot)
            idx_copy(nxt2, slot).wait()

    return issue_chunk, finish


def _combine(slot, x1_ref, gate_ref, ybuf, tm):
    g = gate_ref[...]
    y = g[:, 0:1] * ybuf[slot, 0:tm, :] + g[:, 1:2] * ybuf[slot, tm:2 * tm, :]
    return x1_ref[...] + y


def _mid_kernel(dest_hbm, yr_hbm, x1_ref, gate_ref, cos_ref, sin_ref,
                gkv_ref, wkvt_ref, gmix_ref, wqt_ref, wqm_ref,
                x2_ref, kt_ref, vt_ref, qt_ref, qm_ref,
                idx_smem, ybuf, sem_idx, sem_rows, *, tm, d_att):
    n_heads = d_att // HEAD_DIM
    n_chunks = 8
    for par in range(2):
        tok = slice(par * tm, (par + 1) * tm)
        issue_chunk, finish = _gather_begin(2 * pl.program_id(0) + par, par, 2 * pl.num_programs(0),
                                            dest_hbm, yr_hbm, idx_smem, ybuf, sem_idx, sem_rows,
                                            2 * tm, n_chunks)
        chunks = iter(range(n_chunks))
        x2 = _combine(par, x1_ref.at[tok], gate_ref.at[tok], ybuf, tm)
        x2_ref[tok, :] = x2
        issue_chunk(next(chunks))
        cos_t = cos_ref[:, tok]
        sin_t = sin_ref[:, tok]
        hk = _rms(x2, gkv_ref[...]).astype(BF16)
        issue_chunk(next(chunks))
        vt_ref[0, :, tok] = lax.dot_general(wkvt_ref[d_att:, :], hk, NT_DIMS, preferred_element_type=F32)
        issue_chunk(next(chunks))
        kt_ref[0, :, tok] = lax.dot_general(wkvt_ref[:d_att, :], hk, NT_DIMS, preferred_element_type=F32)
        issue_chunk(next(chunks))
        _rot_t(kt_ref, tok, cos_t, sin_t, n_heads)
        hq = _rms(x2, gmix_ref[...]).astype(BF16)
        issue_chunk(next(chunks))
        qt_ref[0, :, tok] = lax.dot_general(wqt_ref[...], hq, NT_DIMS, preferred_element_type=F32)
        issue_chunk(next(chunks))
        _rot_t(qt_ref, tok, cos_t, sin_t, n_heads)
        issue_chunk(next(chunks))
        qm_ref[tok, :] = jnp.dot(hq, wqm_ref[...], preferred_element_type=F32)
        for c in chunks:
            issue_chunk(c)
        finish()


def _mid(x1, gate, yr, dest, pos, n_seq, P, tm):
    n, d = x1.shape
    d_att = P["w_kv_t_bf"].shape[0] // 2
    tg = 2 * tm
    s_out = n // n_seq
    assert s_out % tg == 0
    nt = s_out // tg
    half = ROT_DIM // 2
    reps = s_out // pos.shape[0]
    inv = ROPE_THETA ** (-jnp.arange(half, dtype=F32) * (2.0 / ROT_DIM))
    ang_t = jnp.tile(inv[:, None] * pos.astype(F32)[None, :], (1, reps))
    const = lambda *shape: pl.BlockSpec(shape, lambda i: (0,) * len(shape))
    tok = lambda w: pl.BlockSpec((tg, w), lambda i: (i, 0))
    tbl_t = pl.BlockSpec((half, tg), lambda i: (0, i % nt))
    feat_t = pl.BlockSpec((1, d_att, tg), lambda i: (i // nt, 0, i % nt))
    feat_shape = jax.ShapeDtypeStruct((n_seq, d_att, s_out), F32)
    return pl.pallas_call(
        functools.partial(_mid_kernel, tm=tm, d_att=d_att),
        out_shape=(jax.ShapeDtypeStruct((n, d), F32), feat_shape, feat_shape, feat_shape,
                   jax.ShapeDtypeStruct((n, D_MEMQ), F32)),
        grid=(n // tg,),
        in_specs=[
            pl.BlockSpec(memory_space=pl.ANY), pl.BlockSpec(memory_space=pl.ANY),
            tok(d), tok(2), tbl_t, tbl_t,
            const(1, d), const(2 * d_att, d), const(1, d), const(d_att, d), const(d, D_MEMQ),
        ],
        out_specs=(tok(d), feat_t, feat_t, feat_t, tok(D_MEMQ)),
        scratch_shapes=[
            pltpu.SMEM((2, 2 * tm), I32),
            pltpu.VMEM((2, 2 * tm, d), F32),
            pltpu.SemaphoreType.DMA((2,)),
            pltpu.SemaphoreType.DMA((2,)),
        ],
        compiler_params=pltpu.CompilerParams(
            dimension_semantics=("arbitrary",), vmem_limit_bytes=VMEM_LIMIT),
        name="mid",
    )(_dest_tiles(dest, tm), yr, x1, gate, jnp.cos(ang_t), jnp.sin(ang_t),
      P["g_kv"].reshape(1, d), P["w_kv_t_bf"], P["g_mix"][1].reshape(1, d), P["w_q_t_bf"], P["w_qm_bf"])


def _final_kernel(dest_hbm, yr_hbm, x1_ref, gate_ref, g_ref, y_ref, idx_smem, ybuf, sem_idx, sem_rows, *, tm):
    for par in range(2):
        tok = slice(par * tm, (par + 1) * tm)
        issue_chunk, finish = _gather_begin(2 * pl.program_id(0) + par, par, 2 * pl.num_programs(0),
                                            dest_hbm, yr_hbm, idx_smem, ybuf, sem_idx, sem_rows, 2 * tm, 1)
        issue_chunk(0)
        y_ref[tok, :] = _rms(_combine(par, x1_ref.at[tok], gate_ref.at[tok], ybuf, tm), g_ref[...])
        finish()


def _final(x1, gate, yr, dest, g_final, tm):
    n, d = x1.shape
    assert n % (2 * tm) == 0
    tok = lambda w: pl.BlockSpec((2 * tm, w), lambda i: (i, 0))
    return pl.pallas_call(
        functools.partial(_final_kernel, tm=tm),
        out_shape=jax.ShapeDtypeStruct((n, d), F32),
        grid=(n // (2 * tm),),
        in_specs=[
            pl.BlockSpec(memory_space=pl.ANY), pl.BlockSpec(memory_space=pl.ANY),
            tok(d), tok(2), pl.BlockSpec((1, d), lambda i: (0, 0)),
        ],
        out_specs=tok(d),
        scratch_shapes=[
            pltpu.SMEM((2, 2 * tm), I32),
            pltpu.VMEM((2, 2 * tm, d), F32),
            pltpu.SemaphoreType.DMA((2,)),
            pltpu.SemaphoreType.DMA((2,)),
        ],
        compiler_params=pltpu.CompilerParams(
            dimension_semantics=("arbitrary",), vmem_limit_bytes=VMEM_LIMIT),
        name="final",
    )(_dest_tiles(dest, tm), yr, x1, gate, g_final.reshape(1, d))


def _moba_prompt_kernel(qt_ref, kt_ref, vt_ref, o_ref, kmean, krows, vones, sel_sc, s_buf, p_buf, acc,
                        *, s_len, n_grp):
    qb_i = pl.program_id(2)
    nb = s_len // MOBA_BLOCK
    n_h = LANE // HEAD_DIM
    tq = MOBA_BLOCK
    nq = n_h * tq
    scale = HEAD_DIM ** -0.5
    feat = lax.broadcasted_iota(I32, (LANE, MOBA_BLOCK), 0) // HEAD_DIM
    groups = range(n_grp)

    @pl.when(qb_i == 0)
    def _():
        kmean[...] = jnp.zeros(kmean.shape, F32)
        for g in groups:
            rows = slice(g * LANE, (g + 1) * LANE)
            for jb in range(nb):
                cols = slice(jb * MOBA_BLOCK, (jb + 1) * MOBA_BLOCK)
                kr = kt_ref[0, rows, cols].T
                kmean[g, jb:jb + 1, :] = jnp.sum(kr, axis=0, keepdims=True) * (1.0 / MOBA_BLOCK)
                krows[g, jb] = kr.astype(BF16)
                vblk = vt_ref[0, rows, cols]
                for hh in range(n_h):
                    vones[g, hh, jb] = jnp.where(feat == hh, vblk, 1.0).astype(BF16)

    key = lax.broadcasted_iota(I32, (MOBA_BLOCK, nq), 0)
    qcol = lax.broadcasted_iota(I32, (MOBA_BLOCK, nq), 1) % tq
    qs, m0, p_own = [], [], []
    for g in groups:
        qt = qt_ref[0, g * LANE:(g + 1) * LANE, :]
        qstack = jnp.concatenate([jnp.where(feat == hh, qt, 0.0) for hh in range(n_h)], axis=1)
        gate = jnp.dot(kmean[g], qstack, preferred_element_type=F32, precision=HIGHEST)
        brow = lax.broadcasted_iota(I32, gate.shape, 0)
        gate = jnp.where(brow < qb_i, gate, NEG_INF)
        sel = _top_k_mask(gate, brow, 0)
        for jb in range(nb):
            sel_sc[g, jb] = sel[jb:jb + 1, :]
        qs.append((qstack * scale).astype(BF16))
        s_own = jnp.where(key <= qcol, jnp.dot(krows[g, qb_i], qs[g], preferred_element_type=F32), NEG_INF)
        m0.append(jnp.max(s_own, axis=0, keepdims=True))
        p_own.append(jnp.exp(s_own - m0[g]).astype(BF16))

    def put_scores(g, j, slot):
        s_buf[g, slot] = jnp.dot(krows[g, j], qs[g], preferred_element_type=F32)

    def add_block(g, alpha, j, slot):
        for hh in range(n_h):
            acc[g, hh] = (alpha[:, hh * tq:(hh + 1) * tq] * acc[g, hh]
                          + jnp.dot(vones[g, hh, j], p_buf[g, slot, :, hh * tq:(hh + 1) * tq],
                                    preferred_element_type=F32))

    for g in groups:
        acc[g] = jnp.zeros((n_h, LANE, tq), F32)
        p_buf[g, 1] = p_own[g]
        put_scores(g, 0, 0)

    def half_trip(j, slot, carry):
        out = []
        for g in groups:
            m, alpha_prev = carry[g]
            add_block(g, alpha_prev, jnp.where(j == 0, qb_i, j - 1), 1 - slot)
            put_scores(g, jnp.minimum(j + 1, nb - 1), 1 - slot)
            chosen = sel_sc[g, j] > 0.0
            m_new = jnp.maximum(m, jnp.where(chosen, jnp.max(s_buf[g, slot], axis=0, keepdims=True), NEG_INF))
            p_buf[g, slot] = jnp.exp(s_buf[g, slot] - jnp.where(chosen, m_new, jnp.inf)).astype(BF16)
            out.append((m_new, jnp.exp(m - m_new)))
        return tuple(out)

    def body(t, carry):
        return half_trip(2 * t + 1, 1, half_trip(2 * t, 0, carry))

    n_trips = (qb_i + 1) // 2
    one = jnp.ones((1, nq), F32)
    state = lax.fori_loop(0, n_trips, body, tuple((m0[g], one) for g in groups))
    last = jnp.where(n_trips == 0, qb_i, 2 * n_trips - 1)
    for g in groups:
        add_block(g, state[g][1], last, 1)
        out_t = jnp.zeros((LANE, tq), F32)
        for hh in range(n_h):
            other = (1 - hh) * HEAD_DIM
            a_h = acc[g, hh]
            out_t = jnp.where(feat == hh, a_h / a_h[other:other + 1, :], out_t)
        o_ref[:, g * LANE:(g + 1) * LANE] = out_t.T


def _moba_prompt(qt, kt, vt, b, s, n_grp=2):
    d_att = qt.shape[1]
    nb = s // MOBA_BLOCK
    fw = n_grp * LANE
    assert s % MOBA_BLOCK == 0 and LANE // HEAD_DIM == 2 and d_att % fw == 0
    nb_pad = -(-nb // SUBLANE) * SUBLANE
    kv_spec = pl.BlockSpec((1, fw, s), lambda i, hp, qb: (i, hp, 0))
    return pl.pallas_call(
        functools.partial(_moba_prompt_kernel, s_len=s, n_grp=n_grp),
        out_shape=jax.ShapeDtypeStruct((b * s, d_att), F32),
        grid=(b, d_att // fw, nb),
        in_specs=[
            pl.BlockSpec((1, fw, MOBA_BLOCK), lambda i, hp, qb: (i, hp, qb)),
            kv_spec, kv_spec,
        ],
        out_specs=pl.BlockSpec((MOBA_BLOCK, fw), lambda i, hp, qb: (i * nb + qb, hp)),
        scratch_shapes=[
            pltpu.VMEM((n_grp, nb_pad, LANE), F32),
            pltpu.VMEM((n_grp, nb, MOBA_BLOCK, LANE), BF16),
            pltpu.VMEM((n_grp, 2, nb, LANE, MOBA_BLOCK), BF16),
            pltpu.VMEM((n_grp, nb, 1, 2 * MOBA_BLOCK), F32),
            pltpu.VMEM((n_grp, 2, MOBA_BLOCK, 2 * MOBA_BLOCK), F32),
            pltpu.VMEM((n_grp, 2, MOBA_BLOCK, 2 * MOBA_BLOCK), BF16),
            pltpu.VMEM((n_grp, 2, LANE, MOBA_BLOCK), F32),
        ],
        compiler_params=pltpu.CompilerParams(
            dimension_semantics=("arbitrary", "arbitrary", "arbitrary"), vmem_limit_bytes=VMEM_LIMIT),
        name="moba_prompt",
    )(qt, kt, vt)


def _head_diag(o, n_heads, ds):
    return jnp.concatenate(
        [o[h * ds:(h + 1) * ds, h * HEAD_DIM:(h + 1) * HEAD_DIM] for h in range(n_heads)], axis=0)


def _moba_sample_kernel(pt_ref, q_ref, kn_ref, vn_ref, *rest, ds, n_heads, nb, bps):
    pages = rest[:4 * bps]
    o_ref, qbd, g_all, m_all, l_all, o_all = rest[4 * bps:]
    jg = pl.program_id(1)
    rows = n_heads * ds
    d_att = n_heads * HEAD_DIM
    scale = HEAD_DIM ** -0.5
    blane = lax.broadcasted_iota(I32, (rows, LANE), 1)

    @pl.when(jg == 0)
    def _():
        q = q_ref[...]
        lane = lax.broadcasted_iota(I32, (ds, d_att), 1)
        for h in range(n_heads):
            qbd[h * ds:(h + 1) * ds, :] = jnp.where(lane // HEAD_DIM == h, q, 0.0)
        g_all[...] = jnp.full((rows, LANE), NEG_INF, F32)
        m_all[...] = jnp.zeros((rows, LANE), F32)
        l_all[...] = jnp.zeros((rows, LANE), F32)

    qb = qbd[...].astype(BF16)
    g_new, m_new, l_new = g_all[...], m_all[...], l_all[...]
    for bi in range(bps):
        k0_ref, k1_ref = pages[2 * bi], pages[2 * bi + 1]
        v0_ref, v1_ref = pages[2 * bps + 2 * bi], pages[2 * bps + 2 * bi + 1]
        j = jg * bps + bi
        kt = jnp.concatenate([k0_ref[0].astype(BF16), k1_ref[0].astype(BF16)], axis=1)
        vt = jnp.concatenate([v0_ref[0].astype(BF16), v1_ref[0].astype(BF16)], axis=1)
        s = jnp.dot(qb, kt, preferred_element_type=F32)
        g = jnp.sum(s, axis=-1, keepdims=True) * (1.0 / MOBA_BLOCK)
        m = jnp.max(s, axis=-1, keepdims=True) * scale
        p = jnp.exp(s * scale - m)
        l = jnp.sum(p, axis=-1, keepdims=True)
        o = lax.dot_general(p.astype(BF16), vt, NT_DIMS, preferred_element_type=F32)
        o_all[j] = _head_diag(o, n_heads, ds)
        here = blane == j
        g_new = jnp.where(here, g, g_new)
        m_new = jnp.where(here, m, m_new)
        l_new = jnp.where(here, l, l_new)
    g_all[...] = g_new
    m_all[...] = m_new
    l_all[...] = l_new

    @pl.when(jg == nb // bps - 1)
    def _():
        chosen = _top_k_mask(g_new, blane, -1) > 0.0
        s_own = lax.dot_general(qb, kn_ref[...].astype(BF16), NT_DIMS, preferred_element_type=F32) * scale
        r_i = lax.broadcasted_iota(I32, (rows, ds), 0) % ds
        c_i = lax.broadcasted_iota(I32, (rows, ds), 1)
        s_own = jnp.where(c_i <= r_i, s_own, NEG_INF)
        m_sel = jnp.max(jnp.where(chosen, m_new, NEG_INF), axis=-1, keepdims=True)
        m_tot = jnp.maximum(m_sel, jnp.max(s_own, axis=-1, keepdims=True))
        wgt = jnp.where(chosen, jnp.exp(m_new - m_tot), 0.0)
        p_own = jnp.exp(s_own - m_tot)
        l_tot = jnp.sum(wgt * l_new, axis=-1, keepdims=True) + jnp.sum(p_own, axis=-1, keepdims=True)
        acc = _head_diag(jnp.dot(p_own.astype(BF16), vn_ref[...].astype(BF16), preferred_element_type=F32),
                         n_heads, ds)
        for jj in range(nb):
            acc = acc + wgt[:, jj:jj + 1] * o_all[jj]
        o_ref[0] = acc / l_tot


def _moba_sample(qb, k_new, v_new, cache_k, cache_v, page_table, db, ds, bps=4):
    n_pool, page, n_heads, hd = cache_k.shape
    d_att = n_heads * hd
    n_pages = page_table.shape[1]
    ppb = MOBA_BLOCK // page
    nb = n_pages // ppb
    rows = n_heads * ds
    assert ppb == 2 and n_pages % ppb == 0 and ds <= MOBA_BLOCK and nb <= LANE and nb % bps == 0
    ck = cache_k.transpose(0, 2, 3, 1).reshape(n_pool, d_att, page)
    cv = cache_v.transpose(0, 2, 3, 1).reshape(n_pool, d_att, page)
    seq = pl.BlockSpec((ds, d_att), lambda i, j, pt: (i, 0))
    ppg = ppb * bps
    pg = lambda off: pl.BlockSpec((1, d_att, page), lambda i, j, pt: (pt[i * n_pages + ppg * j + off], 0, 0))
    out = pl.pallas_call(
        functools.partial(_moba_sample_kernel, ds=ds, n_heads=n_heads, nb=nb, bps=bps),
        out_shape=jax.ShapeDtypeStruct((db, rows, hd), F32),
        grid_spec=pltpu.PrefetchScalarGridSpec(
            num_scalar_prefetch=1,
            grid=(db, nb // bps),
            in_specs=[seq, seq, seq] + [pg(o) for o in range(ppg)] * 2,
            out_specs=pl.BlockSpec((1, rows, hd), lambda i, j, pt: (i, 0, 0)),
            scratch_shapes=[
                pltpu.VMEM((rows, d_att), F32),
                pltpu.VMEM((rows, LANE), F32),
                pltpu.VMEM((rows, LANE), F32),
                pltpu.VMEM((rows, LANE), F32),
                pltpu.VMEM((nb, rows, hd), F32),
            ],
        ),
        compiler_params=pltpu.CompilerParams(
            dimension_semantics=("arbitrary", "arbitrary"), vmem_limit_bytes=VMEM_LIMIT),
        name="moba_sample",
    )(page_table.reshape(-1), qb, k_new, v_new, *([ck] * ppg), *([cv] * ppg))
    return out.reshape(db, n_heads, ds, hd).transpose(0, 2, 1, 3).reshape(db * ds, d_att)


def kernel(x_prompt, x_sample, state_conv, cache_k, cache_v, cache_mem_k, cache_mem_v, page_table,
           mem_prompt, g_mix, g_ffn, g_final, g_mem, w_mem_kv, w_in_a, conv_w, conv_b, cln_g, cln_b,
           w_out_a, g_kv, w_kv, w_in_b, w_out_b, w_rg, b_rg, w_re, b_re, w1, w3, w2):
    depth = g_mix.shape[0]
    n_heads = w_kv.shape[1] // (2 * HEAD_DIM)
    d_att = n_heads * HEAD_DIM
    routers = [_router_weights(w_rg[l], b_rg[l], w_re[l], b_re[l]) for l in range(depth)]
    P = {
        "g_mix": g_mix, "g_ffn": g_ffn, "g_final": g_final, "g_kv": g_kv,
        "w_in_a_bf": w_in_a[0].astype(BF16), "w_out_a_bf": w_out_a[0].astype(BF16),
        "w_kv_t_bf": w_kv.T.astype(BF16),
        "w_q_t_bf": w_in_b[0][:, :d_att].T.astype(BF16), "w_qm_bf": w_in_b[0][:, d_att:].astype(BF16),
        "w_out_b_bf": w_out_b[0].astype(BF16),
        "conv_w": conv_w, "conv_b": conv_b, "cln_g": cln_g, "cln_b": cln_b,
        "wr": [r[0] for r in routers], "br": [r[1] for r in routers],
        "w1": w1, "w3": w3, "w2": w2,
    }
    b_p, s_p, d = x_prompt.shape
    db, ds, _ = x_sample.shape
    n_p, n_s = b_p * s_p, db * ds
    mem_len = mem_prompt.shape[1]
    past = page_table.shape[1] * cache_k.shape[1]
    assert past % MOBA_BLOCK == 0
    ts_p, tm_p = 256, 256
    ts_s, tm_s = ds, n_s // 2

    memkv_t = _memkv(mem_prompt, g_mem, w_mem_kv.transpose(0, 2, 1).astype(BF16))
    kv6 = memkv_t.reshape(depth, b_p, 2, MEM_H, HEAD_DIM, mem_len).transpose(2, 0, 1, 5, 3, 4)
    mem_k_p, mem_v_p = kv6[0], kv6[1]
    cmk = cache_mem_k.transpose(0, 1, 3, 4, 2).reshape(depth, db, D_MEMQ, mem_len)
    cmv = cache_mem_v.transpose(0, 1, 3, 4, 2).reshape(depth, db, D_MEMQ, mem_len)
    mem_p = (memkv_t, memkv_t, (0, 1))
    mem_s = (cmk, cmv, (0, 0))

    ctx_p = jnp.zeros((b_p, CONV_W - 1, state_conv.shape[-1]), F32)
    x1_p, hf_p, gate_p, eid_p, conv_p = _layer0(x_prompt.reshape(n_p, d), b_p, s_p, ctx_p, *mem_p, 0, P, ts_p)
    x1_s, hf_s, gate_s, eid_s, conv_s = _layer0(x_sample.reshape(n_s, d), db, ds, state_conv[0], *mem_s, 0, P, ts_s)
    yr, (dest_p, dest_s) = _moe_layer(0, [(hf_p, eid_p), (hf_s, eid_s)], P)

    x2_p, kt_p, vt_p, qt_p, qm_p = _mid(x1_p, gate_p, yr, dest_p, jnp.arange(s_p), b_p, P, tm_p)
    x2_s, kt_s, vt_s, qt_s, qm_s = _mid(x1_s, gate_s, yr, dest_s, past + jnp.arange(ds), 1, P, tm_s)
    c1_p = _moba_prompt(qt_p, kt_p, vt_p, b_p, s_p)
    k_new, v_new = kt_s[0].T, vt_s[0].T
    c1_s = _moba_sample(qt_s[0].T, k_new, v_new, cache_k, cache_v, page_table, db, ds)
    x3_p, hf_p, gate_p, eid_p = _layer1(x2_p, c1_p, qm_p, b_p, s_p, *mem_p, 1, P, ts_p)
    x3_s, hf_s, gate_s, eid_s = _layer1(x2_s, c1_s, qm_s, db, ds, *mem_s, 1, P, ts_s)
    yr, (dest_p, dest_s) = _moe_layer(1, [(hf_p, eid_p), (hf_s, eid_s)], P)
    y_p = _final(x3_p, gate_p, yr, dest_p, g_final, tm_p).reshape(b_p, s_p, d)
    y_s = _final(x3_s, gate_s, yr, dest_s, g_final, tm_s).reshape(db, ds, d)

    k_p = kt_p.reshape(b_p, n_heads, HEAD_DIM, s_p).transpose(0, 3, 1, 2)
    v_p = vt_p.reshape(b_p, n_heads, HEAD_DIM, s_p).transpose(0, 3, 1, 2)
    k_s = k_new.reshape(db, ds, n_heads, HEAD_DIM)
    v_s = v_new.reshape(db, ds, n_heads, HEAD_DIM)
    return (y_p, y_s, conv_p[None], conv_s[None], k_p, v_p, k_s, v_s, mem_k_p, mem_v_p)
```

```python
import functools

import jax
import jax.numpy as jnp
from jax import lax
from jax.experimental import pallas as pl
from jax.experimental.pallas import tpu as pltpu

F32 = jnp.float32
BF16 = jnp.bfloat16
I32 = jnp.int32
U32 = jnp.uint32
HIGHEST = lax.Precision.HIGHEST

EPS = 1e-6
HEAD_DIM = 64
MEM_H = 4
D_MEMQ = MEM_H * HEAD_DIM
CONV_W = 31
CTX_ROWS = 32
CTX_PAD = CTX_ROWS - (CONV_W - 1)
MOBA_BLOCK = 256
MOBA_TOPK = 3
ROT_DIM = HEAD_DIM // 4
ROPE_THETA = 500000.0
N_GROUPS = 4
E_PER_GROUP = 16
N_EXPERTS = N_GROUPS * E_PER_GROUP
MOE_BLK = 256
PLAN_TILE_MAX = 512
GATHER_UNROLL = 8
DISPATCH_TILE = 128
ROUTER_LANES = 128
LANE = 128
SUBLANE = 8
VMEM_LIMIT = 56 * 1024 * 1024

NEG_INF = float("-inf")
BIG_IDX = 1 << 20
NT_DIMS = (((1,), (1,)), ((), ()))


def _rms(x, g):
    return x * lax.rsqrt(jnp.mean(x * x, axis=-1, keepdims=True) + EPS) * g


def _pack_bf16_pairs(x):
    half = x.shape[1] // 2
    hi = lax.bitcast_convert_type(x[:, :half].astype(BF16).astype(F32), U32)
    lo = lax.bitcast_convert_type(x[:, half:].astype(BF16).astype(F32), U32)
    return hi | (lo >> 16)


def _unpack_bf16_pairs(u):
    hi = lax.bitcast_convert_type(u & jnp.uint32(0xFFFF0000), F32)
    lo = lax.bitcast_convert_type(u << 16, F32)
    return jnp.concatenate([hi, lo], axis=1).astype(BF16)


def _first_argmax(vals, idx, axis):
    m = jnp.max(vals, axis=axis, keepdims=True)
    first = jnp.min(jnp.where(vals == m, idx, BIG_IDX), axis=axis, keepdims=True)
    return m, first


def _top_k_mask(gate, idx, axis):
    sel = jnp.zeros(gate.shape, F32)
    for _ in range(MOBA_TOPK):
        m, first = _first_argmax(gate, idx, axis)
        pick = jnp.logical_and(idx == first, m > NEG_INF)
        sel = jnp.where(pick, 1.0, sel)
        gate = jnp.where(pick, NEG_INF, gate)
    return sel


def _rot_t(xt_ref, tok, cos_t, sin_t, n_heads):
    half = ROT_DIM // 2
    for hh in range(n_heads):
        r0 = hh * HEAD_DIM
        xa = xt_ref[0, r0:r0 + half, tok]
        xb = xt_ref[0, r0 + half:r0 + ROT_DIM, tok]
        xt_ref[0, r0:r0 + half, tok] = xa * cos_t - xb * sin_t
        xt_ref[0, r0 + half:r0 + ROT_DIM, tok] = xb * cos_t + xa * sin_t


def _memkv_kernel(mem_ref, g_ref, wt_ref, out_ref):
    h = _rms(mem_ref[0], g_ref[0]).astype(BF16)
    out_ref[0, 0] = lax.dot_general(wt_ref[0], h, NT_DIMS, preferred_element_type=F32)


def _memkv(mem, g_mem, w_mem_kv_t_bf):
    b, m, d = mem.shape
    depth = g_mem.shape[0]
    n_out = w_mem_kv_t_bf.shape[1]
    return pl.pallas_call(
        _memkv_kernel,
        out_shape=jax.ShapeDtypeStruct((depth, b, n_out, m), F32),
        grid=(depth, b),
        in_specs=[
            pl.BlockSpec((1, m, d), lambda l, i: (i, 0, 0)),
            pl.BlockSpec((1, 1, d), lambda l, i: (l, 0, 0)),
            pl.BlockSpec((1, n_out, d), lambda l, i: (l, 0, 0)),
        ],
        out_specs=pl.BlockSpec((1, 1, n_out, m), lambda l, i: (l, i, 0, 0)),
        compiler_params=pltpu.CompilerParams(dimension_semantics=("arbitrary", "arbitrary")),
        name="memkv",
    )(mem, g_mem.reshape(depth, 1, d), w_mem_kv_t_bf)


def _mem_attn_into(mix, off, qm, mkt_ref, mvt_ref):
    mkt = mkt_ref[0, 0].astype(BF16)
    mvt = mvt_ref[0, 0].astype(BF16)
    scale = HEAD_DIM ** -0.5
    lane = lax.broadcasted_iota(I32, qm.shape, 1)
    om = jnp.zeros(qm.shape, F32)
    for hh in range(MEM_H):
        in_head = (lane // HEAD_DIM) == hh
        q = jnp.where(in_head, qm, 0.0).astype(BF16)
        s = jnp.dot(q, mkt, preferred_element_type=F32) * scale
        m = jnp.max(s, axis=-1, keepdims=True)
        e = jnp.exp(s - m)
        l = jnp.sum(e, axis=-1, keepdims=True)
        o = lax.dot_general(e.astype(BF16), mvt, NT_DIMS, preferred_element_type=F32) / l
        om = jnp.where(in_head, o, om)
    mix[:, off:off + D_MEMQ] = om.astype(BF16)


def _out_and_route(x, mix, wout_ref, gffn_ref, wr_ref, br_ref, x1_ref, hf_ref, gate_ref, eid_ref):
    x1 = x + jnp.dot(mix[...], wout_ref[...], preferred_element_type=F32)
    x1_ref[...] = x1
    hf = _rms(x1, gffn_ref[...])
    hf_ref[...] = _pack_bf16_pairs(hf)
    hf_hi = hf.astype(BF16)
    hf_lo = (hf - hf_hi.astype(F32)).astype(BF16)
    logits = jnp.dot(jnp.concatenate([hf_hi, hf_lo, hf_hi], axis=1), wr_ref[...],
                     preferred_element_type=F32) + br_ref[...]
    lane = lax.broadcasted_iota(I32, logits.shape, 1)
    gl = jnp.where((lane >= N_EXPERTS) & (lane < N_EXPERTS + N_GROUPS), logits, NEG_INF)
    gmax, glane = _first_argmax(gl, lane, -1)
    p_group = 1.0 / jnp.sum(jnp.exp(gl - gmax), axis=-1, keepdims=True)
    e_lo = (glane - N_EXPERTS) * E_PER_GROUP
    el = jnp.where((lane >= e_lo) & (lane < e_lo + E_PER_GROUP), logits, NEG_INF)
    m1, i1 = _first_argmax(el, lane, -1)
    el2 = jnp.where(lane == i1, NEG_INF, el)
    m2, i2 = _first_argmax(el2, lane, -1)
    e2 = jnp.exp(m2 - m1)
    g1 = p_group / (1.0 + e2)
    g2 = g1 * e2
    lane2 = lax.broadcasted_iota(I32, gate_ref.shape, 1)
    gate_ref[...] = jnp.where(lane2 == 0, g1, g2)
    if eid_ref.shape[1] == 2:
        eid_ref[...] = jnp.where(lane2 == 0, i1, i2)
    else:
        t = x.shape[0]
        diag = lax.broadcasted_iota(I32, (t, t), 0) == lax.broadcasted_iota(I32, (t, t), 1)
        rows = [jnp.sum(jnp.where(diag, col.astype(F32), 0.0), axis=0, keepdims=True) for col in (i1, i2)]
        eid_ref[...] = jnp.concatenate(rows, axis=0).astype(I32)


def _eid_out(n, ts, nt):
    if ts % LANE == 0:
        return jax.ShapeDtypeStruct((2, n), I32), pl.BlockSpec((2, ts), lambda i, t: (0, i * nt + t))
    return jax.ShapeDtypeStruct((n, 2), I32), pl.BlockSpec((ts, 2), lambda i, t: (i * nt + t, 0))


def _router_weights(w_rg, b_rg, w_re, b_re):
    d = w_rg.shape[0]
    pad = ROUTER_LANES - N_EXPERTS - N_GROUPS
    wr = jnp.concatenate([w_re, w_rg, jnp.zeros((d, pad), F32)], axis=1)
    br = jnp.concatenate([b_re, b_rg, jnp.zeros((pad,), F32)]).reshape(1, ROUTER_LANES)
    w_hi = wr.astype(BF16)
    w_lo = (wr - w_hi.astype(F32)).astype(BF16)
    return jnp.concatenate([w_hi, w_hi, w_lo], axis=0), br


def _layer0_kernel(x_ref, ctx_ref, mk_ref, mv_ref, gmix_ref, win_ref, cw_ref, cb_ref, lg_ref, lb_ref,
                   wout_ref, gffn_ref, wr_ref, br_ref,
                   x1_ref, hf_ref, gate_ref, eid_ref, nc_ref,
                   zext, zsh, cbuf, mix, *, ts, rc, d_conv):
    t = pl.program_id(1)
    nt = pl.num_programs(1)

    @pl.when(t == 0)
    def _():
        zext[0:CTX_PAD, :] = jnp.zeros((CTX_PAD, d_conv), F32)
        zext[CTX_PAD:CTX_ROWS, :] = ctx_ref[0]

    @pl.when(t > 0)
    def _():
        zext[0:CTX_ROWS, :] = zext[ts:ts + CTX_ROWS, :]

    x = x_ref[...]
    h = _rms(x, gmix_ref[...]).astype(BF16)
    u = jnp.dot(h, win_ref[...], preferred_element_type=F32)
    z = u[:, :d_conv] * jax.nn.sigmoid(u[:, d_conv:2 * d_conv])
    zext[CTX_ROWS:CTX_ROWS + ts, :] = z
    qm = u[:, 2 * d_conv:]

    @pl.when(t == nt - 1)
    def _():
        nc_ref[0] = zext[ts + CTX_PAD: ts + CTX_ROWS, :]

    for sh in range(1, SUBLANE):
        zsh[sh - 1] = zext[sh:sh + ts + CTX_ROWS - SUBLANE, :]

    def chunk(ci, carry):
        r0 = pl.multiple_of(ci * rc, SUBLANE)
        for cbk in range(d_conv // LANE):
            cs = slice(cbk * LANE, (cbk + 1) * LANE)
            acc = jnp.zeros((rc, LANE), F32) + cb_ref[:, cs]
            for w in range(CONV_W):
                hi, sh = divmod(w + CTX_PAD, SUBLANE)
                if sh == 0:
                    win = zext[pl.ds(r0 + hi * SUBLANE, rc), cs]
                else:
                    win = zsh[sh - 1, pl.ds(r0 + hi * SUBLANE, rc), cs]
                acc = acc + win * cw_ref[w:w + 1, cs]
            cbuf[pl.ds(r0, rc), cs] = acc
        return carry

    lax.fori_loop(0, ts // rc, chunk, 0)

    c = cbuf[...]
    mu = jnp.mean(c, axis=-1, keepdims=True)
    xc = c - mu
    y = xc * lax.rsqrt(jnp.mean(xc * xc, axis=-1, keepdims=True) + EPS) * lg_ref[...] + lb_ref[...]
    mix[:, :d_conv] = (y * jax.nn.sigmoid(y)).astype(BF16)
    _mem_attn_into(mix, d_conv, qm, mk_ref, mv_ref)
    _out_and_route(x, mix, wout_ref, gffn_ref, wr_ref, br_ref, x1_ref, hf_ref, gate_ref, eid_ref)


def _layer0(x2d, b, s, ctx, mk_arr, mv_arr, kv_rows, l, P, ts):
    n, d = x2d.shape
    nt = s // ts
    d_conv = ctx.shape[-1]
    rc = min(32, ts)
    d_in = P["w_in_a_bf"].shape[-1]
    krow, vrow = kv_rows
    const = lambda *shape: pl.BlockSpec(shape, lambda i, t: (0,) * len(shape))
    tok = lambda w: pl.BlockSpec((ts, w), lambda i, t: (i * nt + t, 0))
    eid_shape, eid_spec = _eid_out(n, ts, nt)
    kern = functools.partial(_layer0_kernel, ts=ts, rc=rc, d_conv=d_conv)
    return pl.pallas_call(
        kern,
        out_shape=(
            jax.ShapeDtypeStruct((n, d), F32),
            jax.ShapeDtypeStruct((n, d // 2), U32),
            jax.ShapeDtypeStruct((n, 2), F32),
            eid_shape,
            jax.ShapeDtypeStruct((b, CONV_W - 1, d_conv), F32),
        ),
        grid=(b, nt),
        in_specs=[
            tok(d),
            pl.BlockSpec((1, CONV_W - 1, d_conv), lambda i, t: (i, 0, 0)),
            pl.BlockSpec((1, 1, D_MEMQ, mk_arr.shape[3]), lambda i, t: (l, i, krow, 0)),
            pl.BlockSpec((1, 1, D_MEMQ, mv_arr.shape[3]), lambda i, t: (l, i, vrow, 0)),
            const(1, d), const(d, d_in), const(CONV_W, d_conv), const(1, d_conv), const(1, d_conv),
            const(1, d_conv), const(d_conv + D_MEMQ, d), const(1, d), const(3 * d, ROUTER_LANES),
            const(1, ROUTER_LANES),
        ],
        out_specs=(tok(d), tok(d // 2), tok(2), eid_spec,
                   pl.BlockSpec((1, CONV_W - 1, d_conv), lambda i, t: (i, 0, 0))),
        scratch_shapes=[
            pltpu.VMEM((ts + CTX_ROWS, d_conv), F32),
            pltpu.VMEM((SUBLANE - 1, ts + CTX_ROWS - SUBLANE, d_conv), F32),
            pltpu.VMEM((ts, d_conv), F32),
            pltpu.VMEM((ts, d_conv + D_MEMQ), BF16),
        ],
        compiler_params=pltpu.CompilerParams(
            dimension_semantics=("arbitrary", "arbitrary"), vmem_limit_bytes=VMEM_LIMIT),
        name="layer0",
    )(x2d, ctx, mk_arr, mv_arr, P["g_mix"][l].reshape(1, d), P["w_in_a_bf"], P["conv_w"][0],
      P["conv_b"][0].reshape(1, d_conv), P["cln_g"][0].reshape(1, d_conv), P["cln_b"][0].reshape(1, d_conv),
      P["w_out_a_bf"], P["g_ffn"][l].reshape(1, d), P["wr"][l], P["br"][l])


def _layer1_kernel(x_ref, c_ref, qm_ref, mk_ref, mv_ref, wout_ref, gffn_ref, wr_ref, br_ref,
                   x1_ref, hf_ref, gate_ref, eid_ref, mix, *, d_att):
    mix[:, :d_att] = c_ref[...].astype(BF16)
    _mem_attn_into(mix, d_att, qm_ref[...], mk_ref, mv_ref)
    _out_and_route(x_ref[...], mix, wout_ref, gffn_ref, wr_ref, br_ref, x1_ref, hf_ref, gate_ref, eid_ref)


def _layer1(x2d, c2d, qm2d, b, s, mk_arr, mv_arr, kv_rows, l, P, ts):
    n, d = x2d.shape
    nt = s // ts
    d_att = c2d.shape[-1]
    krow, vrow = kv_rows
    const = lambda *shape: pl.BlockSpec(shape, lambda i, t: (0,) * len(shape))
    tok = lambda w: pl.BlockSpec((ts, w), lambda i, t: (i * nt + t, 0))
    eid_shape, eid_spec = _eid_out(n, ts, nt)
    return pl.pallas_call(
        functools.partial(_layer1_kernel, d_att=d_att),
        out_shape=(
            jax.ShapeDtypeStruct((n, d), F32),
            jax.ShapeDtypeStruct((n, d // 2), U32),
            jax.ShapeDtypeStruct((n, 2), F32),
            eid_shape,
        ),
        grid=(b, nt),
        in_specs=[
            tok(d), tok(d_att), tok(D_MEMQ),
            pl.BlockSpec((1, 1, D_MEMQ, mk_arr.shape[3]), lambda i, t: (l, i, krow, 0)),
            pl.BlockSpec((1, 1, D_MEMQ, mv_arr.shape[3]), lambda i, t: (l, i, vrow, 0)),
            const(d_att + D_MEMQ, d), const(1, d), const(3 * d, ROUTER_LANES), const(1, ROUTER_LANES),
        ],
        out_specs=(tok(d), tok(d // 2), tok(2), eid_spec),
        scratch_shapes=[pltpu.VMEM((ts, d_att + D_MEMQ), BF16)],
        compiler_params=pltpu.CompilerParams(
            dimension_semantics=("arbitrary", "arbitrary"), vmem_limit_bytes=VMEM_LIMIT),
        name="layer1",
    )(x2d, c2d, qm2d, mk_arr, mv_arr, P["w_out_b_bf"], P["g_ffn"][l].reshape(1, d), P["wr"][l], P["br"][l])


def _plan_kernel(eid_ref, dest_ref, counts_ref, cnt, base, pstart, *, ta):
    ph = pl.program_id(0)
    t = pl.program_id(1)
    expert = lax.broadcasted_iota(I32, (LANE, ta), 0)
    onehot = eid_ref[...] == expert
    ohf = jnp.where(onehot, 1.0, 0.0)
    tile_counts = jnp.sum(ohf, axis=1, keepdims=True)

    @pl.when(ph == 0)
    def _():
        @pl.when(t == 0)
        def _():
            cnt[...] = jnp.zeros(cnt.shape, F32)
        cnt[...] = cnt[...] + tile_counts

    @pl.when(ph == 1)
    def _():
        @pl.when(t == 0)
        def _():
            counts = cnt[...]
            counts_ref[...] = counts.astype(I32)
            pc = (((counts.astype(I32) + (MOE_BLK - 1)) // MOE_BLK) * MOE_BLK).astype(F32)
            r = lax.broadcasted_iota(I32, (LANE, LANE), 0)
            c = lax.broadcasted_iota(I32, (LANE, LANE), 1)
            below = jnp.where(c < r, 1.0, 0.0)
            pstart[...] = jnp.dot(below, pc, preferred_element_type=F32, precision=HIGHEST)
            base[...] = jnp.zeros(base.shape, F32)

        r = lax.broadcasted_iota(I32, (ta, ta), 0)
        c = lax.broadcasted_iota(I32, (ta, ta), 1)
        upto = jnp.where(r <= c, 1.0, 0.0).astype(BF16)
        prefix = jnp.dot(ohf.astype(BF16), upto, preferred_element_type=F32)
        slot = pstart[:, 0:1] + base[:, 0:1] + prefix - 1.0
        dest_ref[...] = jnp.sum(jnp.where(onehot, slot, 0.0), axis=0, keepdims=True).astype(I32)
        base[...] = base[...] + tile_counts


def _plan(eid_row):
    a = eid_row.shape[1]
    ta = max(t for t in range(LANE, PLAN_TILE_MAX + 1, LANE) if a % t == 0)
    return pl.pallas_call(
        functools.partial(_plan_kernel, ta=ta),
        out_shape=(jax.ShapeDtypeStruct((1, a), I32), jax.ShapeDtypeStruct((LANE, LANE), I32)),
        grid=(2, a // ta),
        in_specs=[pl.BlockSpec((1, ta), lambda ph, t: (0, t))],
        out_specs=(pl.BlockSpec((1, ta), lambda ph, t: (0, t * ph)),
                   pl.BlockSpec((LANE, LANE), lambda ph, t: (0, 0))),
        scratch_shapes=[pltpu.VMEM((LANE, LANE), F32)] * 3,
        compiler_params=pltpu.CompilerParams(dimension_semantics=("arbitrary", "arbitrary")),
        name="moe_plan",
    )(eid_row)


def _dest_tiles(dest_t, tm):
    n = dest_t.shape[1]
    return dest_t.reshape(2, n // tm, tm).transpose(1, 0, 2).reshape(n // tm, 2 * tm)


def _dispatch_kernel(pend_ref, nused_ref, dest_hbm, *rest, tm, starts, n_blk):
    n_src = len(starts) - 1
    hf_refs = rest[:n_src]
    xs_out, idx_smem, sem_idx, sem_rows, zbuf, sem_zero = rest[n_src:]
    i = pl.program_id(0)
    n_tiles = 2 * pl.num_programs(0)

    def idx_copy(k, sl):
        return pltpu.make_async_copy(dest_hbm.at[k], idx_smem.at[sl], sem_idx.at[sl])

    @pl.when(i == 0)
    def _():
        idx_copy(0, 0).start()
        zbuf[...] = jnp.zeros(zbuf.shape, zbuf.dtype)

        def zero_block(row0):
            return pltpu.make_async_copy(zbuf, xs_out.at[pl.ds(pl.multiple_of(row0, MOE_BLK), MOE_BLK)], sem_zero)

        def nonempty(e):
            return pend_ref[e] > jnp.where(e == 0, 0, pend_ref[jnp.maximum(e - 1, 0)])

        def visit(fn):
            def expert(e, carry):
                @pl.when(nonempty(e))
                def _():
                    fn(zero_block(pend_ref[e] - MOE_BLK))
                return carry

            def unused(b, carry):
                fn(zero_block(b * MOE_BLK))
                return carry

            lax.fori_loop(0, N_EXPERTS, expert, 0)
            lax.fori_loop(nused_ref[0], n_blk, unused, 0)

        visit(lambda cp: cp.start())
        visit(lambda cp: cp.wait())

    for g in range(n_src):
        hf_ref = hf_refs[g]

        @pl.when(jnp.logical_and(i >= starts[g], i < starts[g + 1]))
        def _():
            for par in range(2):
                t = 2 * i + par
                idx_copy(t, par).wait()

                @pl.when(t + 1 < n_tiles)
                def _():
                    idx_copy(t + 1, 1 - par).start()

                for r in range(tm):
                    for k in range(2):
                        row = idx_smem[par, k * tm + r]
                        pltpu.make_async_copy(hf_ref.at[pl.ds(par * tm + r, 1)], xs_out.at[pl.ds(row, 1)],
                                              sem_rows).start(priority=k)
            for _ in range(2):
                pltpu.make_async_copy(hf_ref, xs_out.at[pl.ds(0, 2 * tm)], sem_rows).wait()


def _dispatch(hfs, dests, pends, nused, n_blk, tm):
    w = hfs[0].shape[1]
    assert all(hf.shape[0] % (2 * tm) == 0 for hf in hfs)
    tiles = [hf.shape[0] // (2 * tm) for hf in hfs]
    starts = [sum(tiles[:g]) for g in range(len(hfs) + 1)]

    def src_spec(g):
        return pl.BlockSpec((2 * tm, w), lambda i, pe, nu: (jnp.clip(i - starts[g], 0, tiles[g] - 1), 0))

    return pl.pallas_call(
        functools.partial(_dispatch_kernel, tm=tm, starts=tuple(starts), n_blk=n_blk),
        out_shape=jax.ShapeDtypeStruct((n_blk * MOE_BLK, w), hfs[0].dtype),
        grid_spec=pltpu.PrefetchScalarGridSpec(
            num_scalar_prefetch=2,
            grid=(starts[-1],),
            in_specs=[pl.BlockSpec(memory_space=pl.ANY)] + [src_spec(g) for g in range(len(hfs))],
            out_specs=pl.BlockSpec(memory_space=pl.ANY),
            scratch_shapes=[
                pltpu.SMEM((2, 2 * tm), I32),
                pltpu.SemaphoreType.DMA((2,)),
                pltpu.SemaphoreType.DMA,
                pltpu.VMEM((MOE_BLK, w), hfs[0].dtype),
                pltpu.SemaphoreType.DMA,
            ],
        ),
        compiler_params=pltpu.CompilerParams(
            dimension_semantics=("arbitrary",), vmem_limit_bytes=VMEM_LIMIT),
        name="moe_dispatch",
    )(pends, nused, jnp.concatenate([_dest_tiles(d, tm) for d in dests], axis=0), *hfs)


def _experts_kernel(bexp_ref, nused_ref, xs_ref, w1_ref, w3_ref, w2_ref, out_ref, w13, w2s, *, d_exp):
    i = pl.program_id(0)
    nused = nused_ref[0]

    @pl.when(i < nused)
    def _():
        changed = jnp.logical_or(i == 0, bexp_ref[i] != bexp_ref[jnp.maximum(i - 1, 0)])

        @pl.when(changed)
        def _():
            w13[:, :d_exp] = w1_ref[0, 0].astype(BF16)
            w13[:, d_exp:] = w3_ref[0, 0].astype(BF16)
            w2s[...] = w2_ref[0, 0].astype(BF16)

        hcat = jnp.dot(_unpack_bf16_pairs(xs_ref[...]), w13[...], preferred_element_type=F32)
        a = hcat[:, :d_exp]
        act = (a * jax.nn.sigmoid(a)) * hcat[:, d_exp:]
        out_ref[...] = jnp.dot(act.astype(BF16), w2s[...], preferred_element_type=F32)

    @pl.when(i >= nused)
    def _():
        out_ref[...] = jnp.zeros(out_ref.shape, F32)


def _experts(xs, blk_exp, nused, l, w1, w3, w2):
    p_rows, dw = xs.shape
    d, d_exp = w1.shape[-2:]
    assert dw * 2 == d
    n_blk = p_rows // MOE_BLK
    wspec = lambda r, c: pl.BlockSpec((1, 1, r, c), lambda i, be, nu: (l, be[jnp.minimum(i, nu[0] - 1)], 0, 0))
    return pl.pallas_call(
        functools.partial(_experts_kernel, d_exp=d_exp),
        out_shape=jax.ShapeDtypeStruct((p_rows, d), F32),
        grid_spec=pltpu.PrefetchScalarGridSpec(
            num_scalar_prefetch=2,
            grid=(n_blk,),
            in_specs=[
                pl.BlockSpec((MOE_BLK, dw), lambda i, be, nu: (jnp.minimum(i, nu[0] - 1), 0)),
                wspec(d, d_exp), wspec(d, d_exp), wspec(d_exp, d),
            ],
            out_specs=pl.BlockSpec((MOE_BLK, d), lambda i, be, nu: (i, 0)),
            scratch_shapes=[
                pltpu.VMEM((d, 2 * d_exp), BF16),
                pltpu.VMEM((d_exp, d), BF16),
            ],
        ),
        compiler_params=pltpu.CompilerParams(
            dimension_semantics=("arbitrary",), vmem_limit_bytes=VMEM_LIMIT),
        name="moe_experts",
    )(blk_exp, nused, xs, w1, w3, w2)


def _moe_layer(l, groups, P):
    eid_all = jnp.concatenate([g[1] for g in groups], axis=1)
    n_all = eid_all.shape[1]
    a = 2 * n_all
    assert a % MOE_BLK == 0
    n_blk = a // MOE_BLK + N_EXPERTS
    dest_row, counts = _plan(eid_all.reshape(1, a))
    counts = counts[:N_EXPERTS, 0]
    pends = jnp.cumsum((counts + MOE_BLK - 1) // MOE_BLK * MOE_BLK)
    blk_start = jnp.arange(n_blk, dtype=I32) * MOE_BLK
    blk_exp = jnp.minimum(jnp.sum((pends[None, :] <= blk_start[:, None]).astype(I32), axis=1), N_EXPERTS - 1)
    nused = (pends[-1] // MOE_BLK).astype(I32).reshape(1)
    dest = dest_row.reshape(2, n_all)
    dests = []
    off = 0
    for hf, _ in groups:
        dests.append(dest[:, off:off + hf.shape[0]])
        off += hf.shape[0]
    xs = _dispatch([g[0] for g in groups], dests, pends.astype(I32), nused, n_blk, DISPATCH_TILE)
    yr = _experts(xs, blk_exp.astype(I32), nused, l, P["w1"], P["w3"], P["w2"])
    return yr, dests


def _gather_begin(step, slot, nsteps, idx_hbm, src_hbm, idx_smem, buf, sem_idx, sem_rows, nrows, n_chunks):
    nslot = 1 - slot
    last = nsteps - 1
    nxt = jnp.minimum(step + 1, last)
    nxt2 = jnp.minimum(step + 2, last)
    per = nrows // n_chunks

    def idx_copy(k, sl):
        return pltpu.make_async_copy(idx_hbm.at[k], idx_smem.at[sl], sem_idx.at[sl])

    def row_copy(sl, r):
        return pltpu.make_async_copy(src_hbm.at[pl.ds(idx_smem[sl, r], 1)], buf.at[sl, pl.ds(r, 1)],
                                     sem_rows.at[sl])

    def rows_wait(sl):
        pltpu.make_async_copy(src_hbm.at[pl.ds(0, nrows)], buf.at[sl], sem_rows.at[sl]).wait()

    @pl.when(step == 0)
    def _():
        first = idx_copy(0, 0)
        first.start()
        first.wait()

        def body(g, carry):
            for u in range(GATHER_UNROLL):
                row_copy(0, g * GATHER_UNROLL + u).start()
            return carry
        lax.fori_loop(0, nrows // GATHER_UNROLL, body, 0)
        idx_copy(nxt, 1).start()

    idx_copy(nxt, nslot).wait()
    rows_wait(slot)
    idx_copy(nxt2, slot).start()

    def issue_chunk(c):
        for r in range(c * per, (c + 1) * per):
            row_copy(nslot, r).start(priority=r % 2)

    def finish():
        @pl.when(step == last)
        def _():
            rows_wait(nslot)
            idx_copy(nxt2, slot).wait()

    return issue_chunk, finish


def _combine(slot, x1_ref, gate_ref, ybuf, tm):
    g = gate_ref[...]
    y = g[:, 0:1] * ybuf[slot, 0:tm, :] + g[:, 1:2] * ybuf[slot, tm:2 * tm, :]
    return x1_ref[...] + y


def _mid_kernel(dest_hbm, yr_hbm, x1_ref, gate_ref, cos_ref, sin_ref,
                gkv_ref, wkvt_ref, gmix_ref, wqt_ref, wqm_ref,
                x2_ref, kt_ref, vt_ref, qt_ref, qm_ref,
                idx_smem, ybuf, sem_idx, sem_rows, *, tm, d_att):
    n_heads = d_att // HEAD_DIM
    n_chunks = 8
    for par in range(2):
        tok = slice(par * tm, (par + 1) * tm)
        issue_chunk, finish = _gather_begin(2 * pl.program_id(0) + par, par, 2 * pl.num_programs(0),
                                            dest_hbm, yr_hbm, idx_smem, ybuf, sem_idx, sem_rows,
                                            2 * tm, n_chunks)
        chunks = iter(range(n_chunks))
        x2 = _combine(par, x1_ref.at[tok], gate_ref.at[tok], ybuf, tm)
        x2_ref[tok, :] = x2
        issue_chunk(next(chunks))
        cos_t = cos_ref[:, tok]
        sin_t = sin_ref[:, tok]
        hk = _rms(x2, gkv_ref[...]).astype(BF16)
        issue_chunk(next(chunks))
        vt_ref[0, :, tok] = lax.dot_general(wkvt_ref[d_att:, :], hk, NT_DIMS, preferred_element_type=F32)
        issue_chunk(next(chunks))
        kt_ref[0, :, tok] = lax.dot_general(wkvt_ref[:d_att, :], hk, NT_DIMS, preferred_element_type=F32)
        issue_chunk(next(chunks))
        _rot_t(kt_ref, tok, cos_t, sin_t, n_heads)
        hq = _rms(x2, gmix_ref[...]).astype(BF16)
        issue_chunk(next(chunks))
        qt_ref[0, :, tok] = lax.dot_general(wqt_ref[...], hq, NT_DIMS, preferred_element_type=F32)
        issue_chunk(next(chunks))
        _rot_t(qt_ref, tok, cos_t, sin_t, n_heads)
        issue_chunk(next(chunks))
        qm_ref[tok, :] = jnp.dot(hq, wqm_ref[...], preferred_element_type=F32)
        for c in chunks:
            issue_chunk(c)
        finish()


def _mid(x1, gate, yr, dest, pos, n_seq, P, tm):
    n, d = x1.shape
    d_att = P["w_kv_t_bf"].shape[0] // 2
    tg = 2 * tm
    s_out = n // n_seq
    assert s_out % tg == 0
    nt = s_out // tg
    half = ROT_DIM // 2
    reps = s_out // pos.shape[0]
    inv = ROPE_THETA ** (-jnp.arange(half, dtype=F32) * (2.0 / ROT_DIM))
    ang_t = jnp.tile(inv[:, None] * pos.astype(F32)[None, :], (1, reps))
    const = lambda *shape: pl.BlockSpec(shape, lambda i: (0,) * len(shape))
    tok = lambda w: pl.BlockSpec((tg, w), lambda i: (i, 0))
    tbl_t = pl.BlockSpec((half, tg), lambda i: (0, i % nt))
    feat_t = pl.BlockSpec((1, d_att, tg), lambda i: (i // nt, 0, i % nt))
    feat_shape = jax.ShapeDtypeStruct((n_seq, d_att, s_out), F32)
    return pl.pallas_call(
        functools.partial(_mid_kernel, tm=tm, d_att=d_att),
        out_shape=(jax.ShapeDtypeStruct((n, d), F32), feat_shape, feat_shape, feat_shape,
                   jax.ShapeDtypeStruct((n, D_MEMQ), F32)),
        grid=(n // tg,),
        in_specs=[
            pl.BlockSpec(memory_space=pl.ANY), pl.BlockSpec(memory_space=pl.ANY),
            tok(d), tok(2), tbl_t, tbl_t,
            const(1, d), const(2 * d_att, d), const(1, d), const(d_att, d), const(d, D_MEMQ),
        ],
        out_specs=(tok(d), feat_t, feat_t, feat_t, tok(D_MEMQ)),
        scratch_shapes=[
            pltpu.SMEM((2, 2 * tm), I32),
            pltpu.VMEM((2, 2 * tm, d), F32),
            pltpu.SemaphoreType.DMA((2,)),
            pltpu.SemaphoreType.DMA((2,)),
        ],
        compiler_params=pltpu.CompilerParams(
            dimension_semantics=("arbitrary",), vmem_limit_bytes=VMEM_LIMIT),
        name="mid",
    )(_dest_tiles(dest, tm), yr, x1, gate, jnp.cos(ang_t), jnp.sin(ang_t),
      P["g_kv"].reshape(1, d), P["w_kv_t_bf"], P["g_mix"][1].reshape(1, d), P["w_q_t_bf"], P["w_qm_bf"])


def _final_kernel(dest_hbm, yr_hbm, x1_ref, gate_ref, g_ref, y_ref, idx_smem, ybuf, sem_idx, sem_rows, *, tm):
    for par in range(2):
        tok = slice(par * tm, (par + 1) * tm)
        issue_chunk, finish = _gather_begin(2 * pl.program_id(0) + par, par, 2 * pl.num_programs(0),
                                            dest_hbm, yr_hbm, idx_smem, ybuf, sem_idx, sem_rows, 2 * tm, 1)
        issue_chunk(0)
        y_ref[tok, :] = _rms(_combine(par, x1_ref.at[tok], gate_ref.at[tok], ybuf, tm), g_ref[...])
        finish()


def _final(x1, gate, yr, dest, g_final, tm):
    n, d = x1.shape
    assert n % (2 * tm) == 0
    tok = lambda w: pl.BlockSpec((2 * tm, w), lambda i: (i, 0))
    return pl.pallas_call(
        functools.partial(_final_kernel, tm=tm),
        out_shape=jax.ShapeDtypeStruct((n, d), F32),
        grid=(n // (2 * tm),),
        in_specs=[
            pl.BlockSpec(memory_space=pl.ANY), pl.BlockSpec(memory_space=pl.ANY),
            tok(d), tok(2), pl.BlockSpec((1, d), lambda i: (0, 0)),
        ],
        out_specs=tok(d),
        scratch_shapes=[
            pltpu.SMEM((2, 2 * tm), I32),
            pltpu.VMEM((2, 2 * tm, d), F32),
            pltpu.SemaphoreType.DMA((2,)),
            pltpu.SemaphoreType.DMA((2,)),
        ],
        compiler_params=pltpu.CompilerParams(
            dimension_semantics=("arbitrary",), vmem_limit_bytes=VMEM_LIMIT),
        name="final",
    )(_dest_tiles(dest, tm), yr, x1, gate, g_final.reshape(1, d))


def _moba_prompt_kernel(qt_ref, kt_ref, vt_ref, o_ref, kmean, krows, vones, sel_sc, s_buf, p_buf, acc,
                        *, s_len, n_grp):
    qb_i = pl.program_id(2)
    nb = s_len // MOBA_BLOCK
    n_h = LANE // HEAD_DIM
    tq = MOBA_BLOCK
    nq = n_h * tq
    scale = HEAD_DIM ** -0.5
    feat = lax.broadcasted_iota(I32, (LANE, MOBA_BLOCK), 0) // HEAD_DIM
    groups = range(n_grp)

    @pl.when(qb_i == 0)
    def _():
        kmean[...] = jnp.zeros(kmean.shape, F32)
        for g in groups:
            rows = slice(g * LANE, (g + 1) * LANE)
            for jb in range(nb):
                cols = slice(jb * MOBA_BLOCK, (jb + 1) * MOBA_BLOCK)
                kr = kt_ref[0, rows, cols].T
                kmean[g, jb:jb + 1, :] = jnp.sum(kr, axis=0, keepdims=True) * (1.0 / MOBA_BLOCK)
                krows[g, jb] = kr.astype(BF16)
                vblk = vt_ref[0, rows, cols]
                for hh in range(n_h):
                    vones[g, hh, jb] = jnp.where(feat == hh, vblk, 1.0).astype(BF16)

    key = lax.broadcasted_iota(I32, (MOBA_BLOCK, nq), 0)
    qcol = lax.broadcasted_iota(I32, (MOBA_BLOCK, nq), 1) % tq
    qs, m0, p_own = [], [], []
    for g in groups:
        qt = qt_ref[0, g * LANE:(g + 1) * LANE, :]
        qstack = jnp.concatenate([jnp.where(feat == hh, qt, 0.0) for hh in range(n_h)], axis=1)
        gate = jnp.dot(kmean[g], qstack, preferred_element_type=F32, precision=HIGHEST)
        brow = lax.broadcasted_iota(I32, gate.shape, 0)
        gate = jnp.where(brow < qb_i, gate, NEG_INF)
        sel = _top_k_mask(gate, brow, 0)
        for jb in range(nb):
            sel_sc[g, jb] = sel[jb:jb + 1, :]
        qs.append((qstack * scale).astype(BF16))
        s_own = jnp.where(key <= qcol, jnp.dot(krows[g, qb_i], qs[g], preferred_element_type=F32), NEG_INF)
        m0.append(jnp.max(s_own, axis=0, keepdims=True))
        p_own.append(jnp.exp(s_own - m0[g]).astype(BF16))

    def put_scores(g, j, slot):
        s_buf[g, slot] = jnp.dot(krows[g, j], qs[g], preferred_element_type=F32)

    def add_block(g, alpha, j, slot):
        for hh in range(n_h):
            acc[g, hh] = (alpha[:, hh * tq:(hh + 1) * tq] * acc[g, hh]
                          + jnp.dot(vones[g, hh, j], p_buf[g, slot, :, hh * tq:(hh + 1) * tq],
                                    preferred_element_type=F32))

    for g in groups:
        acc[g] = jnp.zeros((n_h, LANE, tq), F32)
        p_buf[g, 1] = p_own[g]
        put_scores(g, 0, 0)

    def half_trip(j, slot, carry):
        out = []
        for g in groups:
            m, alpha_prev = carry[g]
            add_block(g, alpha_prev, jnp.where(j == 0, qb_i, j - 1), 1 - slot)
            put_scores(g, jnp.minimum(j + 1, nb - 1), 1 - slot)
            chosen = sel_sc[g, j] > 0.0
            m_new = jnp.maximum(m, jnp.where(chosen, jnp.max(s_buf[g, slot], axis=0, keepdims=True), NEG_INF))
            p_buf[g, slot] = jnp.exp(s_buf[g, slot] - jnp.where(chosen, m_new, jnp.inf)).astype(BF16)
            out.append((m_new, jnp.exp(m - m_new)))
        return tuple(out)

    def body(t, carry):
        return half_trip(2 * t + 1, 1, half_trip(2 * t, 0, carry))

    n_trips = (qb_i + 1) // 2
    one = jnp.ones((1, nq), F32)
    state = lax.fori_loop(0, n_trips, body, tuple((m0[g], one) for g in groups))
    last = jnp.where(n_trips == 0, qb_i, 2 * n_trips - 1)
    for g in groups:
        add_block(g, state[g][1], last, 1)
        out_t = jnp.zeros((LANE, tq), F32)
        for hh in range(n_h):
            other = (1 - hh) * HEAD_DIM
            a_h = acc[g, hh]
            out_t = jnp.where(feat == hh, a_h / a_h[other:other + 1, :], out_t)
        o_ref[:, g * LANE:(g + 1) * LANE] = out_t.T


def _moba_prompt(qt, kt, vt, b, s, n_grp=2):
    d_att = qt.shape[1]
    nb = s // MOBA_BLOCK
    fw = n_grp * LANE
    assert s % MOBA_BLOCK == 0 and LANE // HEAD_DIM == 2 and d_att % fw == 0
    nb_pad = -(-nb // SUBLANE) * SUBLANE
    kv_spec = pl.BlockSpec((1, fw, s), lambda i, hp, qb: (i, hp, 0))
    return pl.pallas_call(
        functools.partial(_moba_prompt_kernel, s_len=s, n_grp=n_grp),
        out_shape=jax.ShapeDtypeStruct((b * s, d_att), F32),
        grid=(b, d_att // fw, nb),
        in_specs=[
            pl.BlockSpec((1, fw, MOBA_BLOCK), lambda i, hp, qb: (i, hp, qb)),
            kv_spec, kv_spec,
        ],
        out_specs=pl.BlockSpec((MOBA_BLOCK, fw), lambda i, hp, qb: (i * nb + qb, hp)),
        scratch_shapes=[
            pltpu.VMEM((n_grp, nb_pad, LANE), F32),
            pltpu.VMEM((n_grp, nb, MOBA_BLOCK, LANE), BF16),
            pltpu.VMEM((n_grp, 2, nb, LANE, MOBA_BLOCK), BF16),
            pltpu.VMEM((n_grp, nb, 1, 2 * MOBA_BLOCK), F32),
            pltpu.VMEM((n_grp, 2, MOBA_BLOCK, 2 * MOBA_BLOCK), F32),
            pltpu.VMEM((n_grp, 2, MOBA_BLOCK, 2 * MOBA_BLOCK), BF16),
            pltpu.VMEM((n_grp, 2, LANE, MOBA_BLOCK), F32),
        ],
        compiler_params=pltpu.CompilerParams(
            dimension_semantics=("arbitrary", "arbitrary", "arbitrary"), vmem_limit_bytes=VMEM_LIMIT),
        name="moba_prompt",
    )(qt, kt, vt)


def _head_diag(o, n_heads, ds):
    return jnp.concatenate(
        [o[h * ds:(h + 1) * ds, h * HEAD_DIM:(h + 1) * HEAD_DIM] for h in range(n_heads)], axis=0)


def _moba_sample_kernel(pt_ref, q_ref, kn_ref, vn_ref, *rest, ds, n_heads, nb, bps):
    pages = rest[:4 * bps]
    o_ref, qbd, g_all, m_all, l_all, o_all = rest[4 * bps:]
    jg = pl.program_id(1)
    rows = n_heads * ds
    d_att = n_heads * HEAD_DIM
    scale = HEAD_DIM ** -0.5
    blane = lax.broadcasted_iota(I32, (rows, LANE), 1)

    @pl.when(jg == 0)
    def _():
        q = q_ref[...]
        lane = lax.broadcasted_iota(I32, (ds, d_att), 1)
        for h in range(n_heads):
            qbd[h * ds:(h + 1) * ds, :] = jnp.where(lane // HEAD_DIM == h, q, 0.0)
        g_all[...] = jnp.full((rows, LANE), NEG_INF, F32)
        m_all[...] = jnp.zeros((rows, LANE), F32)
        l_all[...] = jnp.zeros((rows, LANE), F32)

    qb = qbd[...].astype(BF16)
    g_new, m_new, l_new = g_all[...], m_all[...], l_all[...]
    for bi in range(bps):
        k0_ref, k1_ref = pages[2 * bi], pages[2 * bi + 1]
        v0_ref, v1_ref = pages[2 * bps + 2 * bi], pages[2 * bps + 2 * bi + 1]
        j = jg * bps + bi
        kt = jnp.concatenate([k0_ref[0].astype(BF16), k1_ref[0].astype(BF16)], axis=1)
        vt = jnp.concatenate([v0_ref[0].astype(BF16), v1_ref[0].astype(BF16)], axis=1)
        s = jnp.dot(qb, kt, preferred_element_type=F32)
        g = jnp.sum(s, axis=-1, keepdims=True) * (1.0 / MOBA_BLOCK)
        m = jnp.max(s, axis=-1, keepdims=True) * scale
        p = jnp.exp(s * scale - m)
        l = jnp.sum(p, axis=-1, keepdims=True)
        o = lax.dot_general(p.astype(BF16), vt, NT_DIMS, preferred_element_type=F32)
        o_all[j] = _head_diag(o, n_heads, ds)
        here = blane == j
        g_new = jnp.where(here, g, g_new)
        m_new = jnp.where(here, m, m_new)
        l_new = jnp.where(here, l, l_new)
    g_all[...] = g_new
    m_all[...] = m_new
    l_all[...] = l_new

    @pl.when(jg == nb // bps - 1)
    def _():
        chosen = _top_k_mask(g_new, blane, -1) > 0.0
        s_own = lax.dot_general(qb, kn_ref[...].astype(BF16), NT_DIMS, preferred_element_type=F32) * scale
        r_i = lax.broadcasted_iota(I32, (rows, ds), 0) % ds
        c_i = lax.broadcasted_iota(I32, (rows, ds), 1)
        s_own = jnp.where(c_i <= r_i, s_own, NEG_INF)
        m_sel = jnp.max(jnp.where(chosen, m_new, NEG_INF), axis=-1, keepdims=True)
        m_tot = jnp.maximum(m_sel, jnp.max(s_own, axis=-1, keepdims=True))
        wgt = jnp.where(chosen, jnp.exp(m_new - m_tot), 0.0)
        p_own = jnp.exp(s_own - m_tot)
        l_tot = jnp.sum(wgt * l_new, axis=-1, keepdims=True) + jnp.sum(p_own, axis=-1, keepdims=True)
        acc = _head_diag(jnp.dot(p_own.astype(BF16), vn_ref[...].astype(BF16), preferred_element_type=F32),
                         n_heads, ds)
        for jj in range(nb):
            acc = acc + wgt[:, jj:jj + 1] * o_all[jj]
        o_ref[0] = acc / l_tot


def _moba_sample(qb, k_new, v_new, cache_k, cache_v, page_table, db, ds, bps=4):
    n_pool, page, n_heads, hd = cache_k.shape
    d_att = n_heads * hd
    n_pages = page_table.shape[1]
    ppb = MOBA_BLOCK // page
    nb = n_pages // ppb
    rows = n_heads * ds
    assert ppb == 2 and n_pages % ppb == 0 and ds <= MOBA_BLOCK and nb <= LANE and nb % bps == 0
    ck = cache_k.transpose(0, 2, 3, 1).reshape(n_pool, d_att, page)
    cv = cache_v.transpose(0, 2, 3, 1).reshape(n_pool, d_att, page)
    seq = pl.BlockSpec((ds, d_att), lambda i, j, pt: (i, 0))
    ppg = ppb * bps
    pg = lambda off: pl.BlockSpec((1, d_att, page), lambda i, j, pt: (pt[i * n_pages + ppg * j + off], 0, 0))
    out = pl.pallas_call(
        functools.partial(_moba_sample_kernel, ds=ds, n_heads=n_heads, nb=nb, bps=bps),
        out_shape=jax.ShapeDtypeStruct((db, rows, hd), F32),
        grid_spec=pltpu.PrefetchScalarGridSpec(
            num_scalar_prefetch=1,
            grid=(db, nb // bps),
            in_specs=[seq, seq, seq] + [pg(o) for o in range(ppg)] * 2,
            out_specs=pl.BlockSpec((1, rows, hd), lambda i, j, pt: (i, 0, 0)),
            scratch_shapes=[
                pltpu.VMEM((rows, d_att), F32),
                pltpu.VMEM((rows, LANE), F32),
                pltpu.VMEM((rows, LANE), F32),
                pltpu.VMEM((rows, LANE), F32),
                pltpu.VMEM((nb, rows, hd), F32),
            ],
        ),
        compiler_params=pltpu.CompilerParams(
            dimension_semantics=("arbitrary", "arbitrary"), vmem_limit_bytes=VMEM_LIMIT),
        name="moba_sample",
    )(page_table.reshape(-1), qb, k_new, v_new, *([ck] * ppg), *([cv] * ppg))
    return out.reshape(db, n_heads, ds, hd).transpose(0, 2, 1, 3).reshape(db * ds, d_att)


def kernel(x_prompt, x_sample, state_conv, cache_k, cache_v, cache_mem_k, cache_mem_v, page_table,
           mem_prompt, g_mix, g_ffn, g_final, g_mem, w_mem_kv, w_in_a, conv_w, conv_b, cln_g, cln_b,
           w_out_a, g_kv, w_kv, w_in_b, w_out_b, w_rg, b_rg, w_re, b_re, w1, w3, w2):
    depth = g_mix.shape[0]
    n_heads = w_kv.shape[1] // (2 * HEAD_DIM)
    d_att = n_heads * HEAD_DIM
    routers = [_router_weights(w_rg[l], b_rg[l], w_re[l], b_re[l]) for l in range(depth)]
    P = {
        "g_mix": g_mix, "g_ffn": g_ffn, "g_final": g_final, "g_kv": g_kv,
        "w_in_a_bf": w_in_a[0].astype(BF16), "w_out_a_bf": w_out_a[0].astype(BF16),
        "w_kv_t_bf": w_kv.T.astype(BF16),
        "w_q_t_bf": w_in_b[0][:, :d_att].T.astype(BF16), "w_qm_bf": w_in_b[0][:, d_att:].astype(BF16),
        "w_out_b_bf": w_out_b[0].astype(BF16),
        "conv_w": conv_w, "conv_b": conv_b, "cln_g": cln_g, "cln_b": cln_b,
        "wr": [r[0] for r in routers], "br": [r[1] for r in routers],
        "w1": w1, "w3": w3, "w2": w2,
    }
    b_p, s_p, d = x_prompt.shape
    db, ds, _ = x_sample.shape
    n_p, n_s = b_p * s_p, db * ds
    mem_len = mem_prompt.shape[1]
    past = page_table.shape[1] * cache_k.shape[1]
    assert past % MOBA_BLOCK == 0
    ts_p, tm_p = 256, 256
    ts_s, tm_s = ds, n_s // 2

    memkv_t = _memkv(mem_prompt, g_mem, w_mem_kv.transpose(0, 2, 1).astype(BF16))
    kv6 = memkv_t.reshape(depth, b_p, 2, MEM_H, HEAD_DIM, mem_len).transpose(2, 0, 1, 5, 3, 4)
    mem_k_p, mem_v_p = kv6[0], kv6[1]
    cmk = cache_mem_k.transpose(0, 1, 3, 4, 2).reshape(depth, db, D_MEMQ, mem_len)
    cmv = cache_mem_v.transpose(0, 1, 3, 4, 2).reshape(depth, db, D_MEMQ, mem_len)
    mem_p = (memkv_t, memkv_t, (0, 1))
    mem_s = (cmk, cmv, (0, 0))

    ctx_p = jnp.zeros((b_p, CONV_W - 1, state_conv.shape[-1]), F32)
    x1_p, hf_p, gate_p, eid_p, conv_p = _layer0(x_prompt.reshape(n_p, d), b_p, s_p, ctx_p, *mem_p, 0, P, ts_p)
    x1_s, hf_s, gate_s, eid_s, conv_s = _layer0(x_sample.reshape(n_s, d), db, ds, state_conv[0], *mem_s, 0, P, ts_s)
    yr, (dest_p, dest_s) = _moe_layer(0, [(hf_p, eid_p), (hf_s, eid_s.T)], P)

    x2_p, kt_p, vt_p, qt_p, qm_p = _mid(x1_p, gate_p, yr, dest_p, jnp.arange(s_p), b_p, P, tm_p)
    x2_s, kt_s, vt_s, qt_s, qm_s = _mid(x1_s, gate_s, yr, dest_s, past + jnp.arange(ds), 1, P, tm_s)
    c1_p = _moba_prompt(qt_p, kt_p, vt_p, b_p, s_p)
    k_new, v_new = kt_s[0].T, vt_s[0].T
    c1_s = _moba_sample(qt_s[0].T, k_new, v_new, cache_k, cache_v, page_table, db, ds)
    x3_p, hf_p, gate_p, eid_p = _layer1(x2_p, c1_p, qm_p, b_p, s_p, *mem_p, 1, P, ts_p)
    x3_s, hf_s, gate_s, eid_s = _layer1(x2_s, c1_s, qm_s, db, ds, *mem_s, 1, P, ts_s)
    yr, (dest_p, dest_s) = _moe_layer(1, [(hf_p, eid_p), (hf_s, eid_s.T)], P)
    y_p = _final(x3_p, gate_p, yr, dest_p, g_final, tm_p).reshape(b_p, s_p, d)
    y_s = _final(x3_s, gate_s, yr, dest_s, g_final, tm_s).reshape(db, ds, d)

    k_p = kt_p.reshape(b_p, n_heads, HEAD_DIM, s_p).transpose(0, 3, 1, 2)
    v_p = vt_p.reshape(b_p, n_heads, HEAD_DIM, s_p).transpose(0, 3, 1, 2)
    k_s = k_new.reshape(db, ds, n_heads, HEAD_DIM)
    v_s = v_new.reshape(db, ds, n_heads, HEAD_DIM)
    return (y_p, y_s, conv_p[None], conv_s[None], k_p, v_p, k_s, v_s, mem_k_p, mem_v_p)
```

```python
import functools

import jax
import jax.numpy as jnp
from jax import lax
from jax.experimental import pallas as pl
from jax.experimental.pallas import tpu as pltpu

F32 = jnp.float32
BF16 = jnp.bfloat16
I32 = jnp.int32
U32 = jnp.uint32
HIGHEST = lax.Precision.HIGHEST

EPS = 1e-6
HEAD_DIM = 64
MEM_H = 4
D_MEMQ = MEM_H * HEAD_DIM
CONV_W = 31
CTX_ROWS = 32
CTX_PAD = CTX_ROWS - (CONV_W - 1)
MOBA_BLOCK = 256
MOBA_TOPK = 3
ROT_DIM = HEAD_DIM // 4
ROPE_THETA = 500000.0
N_GROUPS = 4
E_PER_GROUP = 16
N_EXPERTS = N_GROUPS * E_PER_GROUP
MOE_BLK = 256
PLAN_TILE_MAX = 512
GATHER_UNROLL = 8
DISPATCH_TILE = 128
ROUTER_LANES = 128
LANE = 128
SUBLANE = 8
VMEM_LIMIT = 56 * 1024 * 1024

NEG_INF = float("-inf")
BIG_IDX = 1 << 20
NT_DIMS = (((1,), (1,)), ((), ()))


def _rms(x, g):
    return x * lax.rsqrt(jnp.mean(x * x, axis=-1, keepdims=True) + EPS) * g


def _pack_bf16_pairs(x):
    half = x.shape[1] // 2
    hi = lax.bitcast_convert_type(x[:, :half].astype(BF16).astype(F32), U32)
    lo = lax.bitcast_convert_type(x[:, half:].astype(BF16).astype(F32), U32)
    return hi | (lo >> 16)


def _unpack_bf16_pairs(u):
    hi = lax.bitcast_convert_type(u & jnp.uint32(0xFFFF0000), F32)
    lo = lax.bitcast_convert_type(u << 16, F32)
    return jnp.concatenate([hi, lo], axis=1).astype(BF16)


def _first_argmax(vals, idx, axis):
    m = jnp.max(vals, axis=axis, keepdims=True)
    first = jnp.min(jnp.where(vals == m, idx, BIG_IDX), axis=axis, keepdims=True)
    return m, first


def _top_k_mask(gate, idx, axis):
    sel = jnp.zeros(gate.shape, F32)
    for _ in range(MOBA_TOPK):
        m, first = _first_argmax(gate, idx, axis)
        pick = jnp.logical_and(idx == first, m > NEG_INF)
        sel = jnp.where(pick, 1.0, sel)
        gate = jnp.where(pick, NEG_INF, gate)
    return sel


def _rot_t(xt_ref, tok, cos_t, sin_t, n_heads):
    half = ROT_DIM // 2
    for hh in range(n_heads):
        r0 = hh * HEAD_DIM
        xa = xt_ref[0, r0:r0 + half, tok]
        xb = xt_ref[0, r0 + half:r0 + ROT_DIM, tok]
        xt_ref[0, r0:r0 + half, tok] = xa * cos_t - xb * sin_t
        xt_ref[0, r0 + half:r0 + ROT_DIM, tok] = xb * cos_t + xa * sin_t


def _memkv_kernel(mem_ref, g_ref, wt_ref, out_ref):
    h = _rms(mem_ref[0], g_ref[0]).astype(BF16)
    out_ref[0, 0] = lax.dot_general(wt_ref[0], h, NT_DIMS, preferred_element_type=F32)


def _memkv(mem, g_mem, w_mem_kv_t_bf):
    b, m, d = mem.shape
    depth = g_mem.shape[0]
    n_out = w_mem_kv_t_bf.shape[1]
    return pl.pallas_call(
        _memkv_kernel,
        out_shape=jax.ShapeDtypeStruct((depth, b, n_out, m), F32),
        grid=(depth, b),
        in_specs=[
            pl.BlockSpec((1, m, d), lambda l, i: (i, 0, 0)),
            pl.BlockSpec((1, 1, d), lambda l, i: (l, 0, 0)),
            pl.BlockSpec((1, n_out, d), lambda l, i: (l, 0, 0)),
        ],
        out_specs=pl.BlockSpec((1, 1, n_out, m), lambda l, i: (l, i, 0, 0)),
        compiler_params=pltpu.CompilerParams(dimension_semantics=("arbitrary", "arbitrary")),
        name="memkv",
    )(mem, g_mem.reshape(depth, 1, d), w_mem_kv_t_bf)


def _mem_attn_into(mix, off, qm, mkt_ref, mvt_ref):
    mkt = mkt_ref[0, 0].astype(BF16)
    mvt = mvt_ref[0, 0].astype(BF16)
    scale = HEAD_DIM ** -0.5
    lane = lax.broadcasted_iota(I32, qm.shape, 1)
    om = jnp.zeros(qm.shape, F32)
    for hh in range(MEM_H):
        in_head = (lane // HEAD_DIM) == hh
        q = jnp.where(in_head, qm, 0.0).astype(BF16)
        s = jnp.dot(q, mkt, preferred_element_type=F32) * scale
        m = jnp.max(s, axis=-1, keepdims=True)
        e = jnp.exp(s - m)
        l = jnp.sum(e, axis=-1, keepdims=True)
        o = lax.dot_general(e.astype(BF16), mvt, NT_DIMS, preferred_element_type=F32) / l
        om = jnp.where(in_head, o, om)
    mix[:, off:off + D_MEMQ] = om.astype(BF16)


def _out_and_route(x, mix, wout_ref, gffn_ref, wr_ref, br_ref, x1_ref, hf_ref, gate_ref, eid_ref, cnt_ref, first):
    x1 = x + jnp.dot(mix[...], wout_ref[...], preferred_element_type=F32)
    x1_ref[...] = x1
    hf = _rms(x1, gffn_ref[...])
    hf_ref[...] = _pack_bf16_pairs(hf)
    hf_hi = hf.astype(BF16)
    hf_lo = (hf - hf_hi.astype(F32)).astype(BF16)
    logits = jnp.dot(jnp.concatenate([hf_hi, hf_lo, hf_hi], axis=1), wr_ref[...],
                     preferred_element_type=F32) + br_ref[...]
    lane = lax.broadcasted_iota(I32, logits.shape, 1)
    gl = jnp.where((lane >= N_EXPERTS) & (lane < N_EXPERTS + N_GROUPS), logits, NEG_INF)
    gmax, glane = _first_argmax(gl, lane, -1)
    p_group = 1.0 / jnp.sum(jnp.exp(gl - gmax), axis=-1, keepdims=True)
    e_lo = (glane - N_EXPERTS) * E_PER_GROUP
    el = jnp.where((lane >= e_lo) & (lane < e_lo + E_PER_GROUP), logits, NEG_INF)
    m1, i1 = _first_argmax(el, lane, -1)
    el2 = jnp.where(lane == i1, NEG_INF, el)
    m2, i2 = _first_argmax(el2, lane, -1)
    e2 = jnp.exp(m2 - m1)
    g1 = p_group / (1.0 + e2)
    g2 = g1 * e2
    picked = jnp.where(lane == i1, 1.0, 0.0) + jnp.where(lane == i2, 1.0, 0.0)
    tile_counts = jnp.sum(picked, axis=0, keepdims=True)

    @pl.when(first)
    def _():
        cnt_ref[...] = tile_counts

    @pl.when(jnp.logical_not(first))
    def _():
        cnt_ref[...] = cnt_ref[...] + tile_counts

    lane2 = lax.broadcasted_iota(I32, gate_ref.shape, 1)
    gate_ref[...] = jnp.where(lane2 == 0, g1, g2)
    if eid_ref.shape[1] == 2:
        eid_ref[...] = jnp.where(lane2 == 0, i1, i2)
    else:
        t = x.shape[0]
        diag = lax.broadcasted_iota(I32, (t, t), 0) == lax.broadcasted_iota(I32, (t, t), 1)
        rows = [jnp.sum(jnp.where(diag, col.astype(F32), 0.0), axis=0, keepdims=True) for col in (i1, i2)]
        eid_ref[...] = jnp.concatenate(rows, axis=0).astype(I32)


def _eid_out(n, ts, nt):
    if ts % LANE == 0:
        return jax.ShapeDtypeStruct((2, n), I32), pl.BlockSpec((2, ts), lambda i, t: (0, i * nt + t))
    return jax.ShapeDtypeStruct((n, 2), I32), pl.BlockSpec((ts, 2), lambda i, t: (i * nt + t, 0))


def _router_weights(w_rg, b_rg, w_re, b_re):
    d = w_rg.shape[0]
    pad = ROUTER_LANES - N_EXPERTS - N_GROUPS
    wr = jnp.concatenate([w_re, w_rg, jnp.zeros((d, pad), F32)], axis=1)
    br = jnp.concatenate([b_re, b_rg, jnp.zeros((pad,), F32)]).reshape(1, ROUTER_LANES)
    w_hi = wr.astype(BF16)
    w_lo = (wr - w_hi.astype(F32)).astype(BF16)
    return jnp.concatenate([w_hi, w_hi, w_lo], axis=0), br


def _layer0_kernel(x_ref, ctx_ref, mk_ref, mv_ref, gmix_ref, win_ref, cw_ref, cb_ref, lg_ref, lb_ref,
                   wout_ref, gffn_ref, wr_ref, br_ref,
                   x1_ref, hf_ref, gate_ref, eid_ref, cnt_ref, nc_ref,
                   zext, zsh, cbuf, mix, *, ts, rc, d_conv):
    t = pl.program_id(1)
    nt = pl.num_programs(1)

    @pl.when(t == 0)
    def _():
        zext[0:CTX_PAD, :] = jnp.zeros((CTX_PAD, d_conv), F32)
        zext[CTX_PAD:CTX_ROWS, :] = ctx_ref[0]

    @pl.when(t > 0)
    def _():
        zext[0:CTX_ROWS, :] = zext[ts:ts + CTX_ROWS, :]

    x = x_ref[...]
    h = _rms(x, gmix_ref[...]).astype(BF16)
    u = jnp.dot(h, win_ref[...], preferred_element_type=F32)
    z = u[:, :d_conv] * jax.nn.sigmoid(u[:, d_conv:2 * d_conv])
    zext[CTX_ROWS:CTX_ROWS + ts, :] = z
    qm = u[:, 2 * d_conv:]

    @pl.when(t == nt - 1)
    def _():
        nc_ref[0] = zext[ts + CTX_PAD: ts + CTX_ROWS, :]

    for sh in range(1, SUBLANE):
        zsh[sh - 1] = zext[sh:sh + ts + CTX_ROWS - SUBLANE, :]

    def chunk(ci, carry):
        r0 = pl.multiple_of(ci * rc, SUBLANE)
        for cbk in range(d_conv // LANE):
            cs = slice(cbk * LANE, (cbk + 1) * LANE)
            acc = jnp.zeros((rc, LANE), F32) + cb_ref[:, cs]
            for w in range(CONV_W):
                hi, sh = divmod(w + CTX_PAD, SUBLANE)
                if sh == 0:
                    win = zext[pl.ds(r0 + hi * SUBLANE, rc), cs]
                else:
                    win = zsh[sh - 1, pl.ds(r0 + hi * SUBLANE, rc), cs]
                acc = acc + win * cw_ref[w:w + 1, cs]
            cbuf[pl.ds(r0, rc), cs] = acc
        return carry

    lax.fori_loop(0, ts // rc, chunk, 0)

    c = cbuf[...]
    mu = jnp.mean(c, axis=-1, keepdims=True)
    xc = c - mu
    y = xc * lax.rsqrt(jnp.mean(xc * xc, axis=-1, keepdims=True) + EPS) * lg_ref[...] + lb_ref[...]
    mix[:, :d_conv] = (y * jax.nn.sigmoid(y)).astype(BF16)
    _mem_attn_into(mix, d_conv, qm, mk_ref, mv_ref)
    first = jnp.logical_and(pl.program_id(0) == 0, t == 0)
    _out_and_route(x, mix, wout_ref, gffn_ref, wr_ref, br_ref, x1_ref, hf_ref, gate_ref, eid_ref, cnt_ref, first)


def _layer0(x2d, b, s, ctx, mk_arr, mv_arr, kv_rows, l, P, ts):
    n, d = x2d.shape
    nt = s // ts
    d_conv = ctx.shape[-1]
    rc = min(32, ts)
    d_in = P["w_in_a_bf"].shape[-1]
    krow, vrow = kv_rows
    const = lambda *shape: pl.BlockSpec(shape, lambda i, t: (0,) * len(shape))
    tok = lambda w: pl.BlockSpec((ts, w), lambda i, t: (i * nt + t, 0))
    eid_shape, eid_spec = _eid_out(n, ts, nt)
    kern = functools.partial(_layer0_kernel, ts=ts, rc=rc, d_conv=d_conv)
    return pl.pallas_call(
        kern,
        out_shape=(
            jax.ShapeDtypeStruct((n, d), F32),
            jax.ShapeDtypeStruct((n, d // 2), U32),
            jax.ShapeDtypeStruct((n, 2), F32),
            eid_shape,
            jax.ShapeDtypeStruct((1, LANE), F32),
            jax.ShapeDtypeStruct((b, CONV_W - 1, d_conv), F32),
        ),
        grid=(b, nt),
        in_specs=[
            tok(d),
            pl.BlockSpec((1, CONV_W - 1, d_conv), lambda i, t: (i, 0, 0)),
            pl.BlockSpec((1, 1, D_MEMQ, mk_arr.shape[3]), lambda i, t: (l, i, krow, 0)),
            pl.BlockSpec((1, 1, D_MEMQ, mv_arr.shape[3]), lambda i, t: (l, i, vrow, 0)),
            const(1, d), const(d, d_in), const(CONV_W, d_conv), const(1, d_conv), const(1, d_conv),
            const(1, d_conv), const(d_conv + D_MEMQ, d), const(1, d), const(3 * d, ROUTER_LANES),
            const(1, ROUTER_LANES),
        ],
        out_specs=(tok(d), tok(d // 2), tok(2), eid_spec, const(1, LANE),
                   pl.BlockSpec((1, CONV_W - 1, d_conv), lambda i, t: (i, 0, 0))),
        scratch_shapes=[
            pltpu.VMEM((ts + CTX_ROWS, d_conv), F32),
            pltpu.VMEM((SUBLANE - 1, ts + CTX_ROWS - SUBLANE, d_conv), F32),
            pltpu.VMEM((ts, d_conv), F32),
            pltpu.VMEM((ts, d_conv + D_MEMQ), BF16),
        ],
        compiler_params=pltpu.CompilerParams(
            dimension_semantics=("arbitrary", "arbitrary"), vmem_limit_bytes=VMEM_LIMIT),
        name="layer0",
    )(x2d, ctx, mk_arr, mv_arr, P["g_mix"][l].reshape(1, d), P["w_in_a_bf"], P["conv_w"][0],
      P["conv_b"][0].reshape(1, d_conv), P["cln_g"][0].reshape(1, d_conv), P["cln_b"][0].reshape(1, d_conv),
      P["w_out_a_bf"], P["g_ffn"][l].reshape(1, d), P["wr"][l], P["br"][l])


def _layer1_kernel(x_ref, c_ref, qm_ref, mk_ref, mv_ref, wout_ref, gffn_ref, wr_ref, br_ref,
                   x1_ref, hf_ref, gate_ref, eid_ref, cnt_ref, mix, *, d_att):
    mix[:, :d_att] = c_ref[...].astype(BF16)
    _mem_attn_into(mix, d_att, qm_ref[...], mk_ref, mv_ref)
    first = jnp.logical_and(pl.program_id(0) == 0, pl.program_id(1) == 0)
    _out_and_route(x_ref[...], mix, wout_ref, gffn_ref, wr_ref, br_ref, x1_ref, hf_ref, gate_ref, eid_ref,
                   cnt_ref, first)


def _layer1(x2d, c2d, qm2d, b, s, mk_arr, mv_arr, kv_rows, l, P, ts):
    n, d = x2d.shape
    nt = s // ts
    d_att = c2d.shape[-1]
    krow, vrow = kv_rows
    const = lambda *shape: pl.BlockSpec(shape, lambda i, t: (0,) * len(shape))
    tok = lambda w: pl.BlockSpec((ts, w), lambda i, t: (i * nt + t, 0))
    eid_shape, eid_spec = _eid_out(n, ts, nt)
    return pl.pallas_call(
        functools.partial(_layer1_kernel, d_att=d_att),
        out_shape=(
            jax.ShapeDtypeStruct((n, d), F32),
            jax.ShapeDtypeStruct((n, d // 2), U32),
            jax.ShapeDtypeStruct((n, 2), F32),
            eid_shape,
            jax.ShapeDtypeStruct((1, LANE), F32),
        ),
        grid=(b, nt),
        in_specs=[
            tok(d), tok(d_att), tok(D_MEMQ),
            pl.BlockSpec((1, 1, D_MEMQ, mk_arr.shape[3]), lambda i, t: (l, i, krow, 0)),
            pl.BlockSpec((1, 1, D_MEMQ, mv_arr.shape[3]), lambda i, t: (l, i, vrow, 0)),
            const(d_att + D_MEMQ, d), const(1, d), const(3 * d, ROUTER_LANES), const(1, ROUTER_LANES),
        ],
        out_specs=(tok(d), tok(d // 2), tok(2), eid_spec, const(1, LANE)),
        scratch_shapes=[pltpu.VMEM((ts, d_att + D_MEMQ), BF16)],
        compiler_params=pltpu.CompilerParams(
            dimension_semantics=("arbitrary", "arbitrary"), vmem_limit_bytes=VMEM_LIMIT),
        name="layer1",
    )(x2d, c2d, qm2d, mk_arr, mv_arr, P["w_out_b_bf"], P["g_ffn"][l].reshape(1, d), P["wr"][l], P["br"][l])


def _plan_kernel(eid_ref, counts_ref, dest_ref, base, pstart, *, ta):
    t = pl.program_id(0)
    expert = lax.broadcasted_iota(I32, (LANE, ta), 0)
    onehot = eid_ref[...] == expert
    ohf = jnp.where(onehot, 1.0, 0.0)

    @pl.when(t == 0)
    def _():
        counts = counts_ref[...]
        pc = (((counts.astype(I32) + (MOE_BLK - 1)) // MOE_BLK) * MOE_BLK).astype(F32)
        r = lax.broadcasted_iota(I32, (LANE, LANE), 0)
        c = lax.broadcasted_iota(I32, (LANE, LANE), 1)
        below = jnp.where(c < r, 1.0, 0.0)
        pstart[...] = jnp.dot(below, pc, preferred_element_type=F32, precision=HIGHEST)
        base[...] = jnp.zeros(base.shape, F32)

    r = lax.broadcasted_iota(I32, (ta, ta), 0)
    c = lax.broadcasted_iota(I32, (ta, ta), 1)
    upto = jnp.where(r <= c, 1.0, 0.0).astype(BF16)
    prefix = jnp.dot(ohf.astype(BF16), upto, preferred_element_type=F32)
    slot = pstart[:, 0:1] + base[:, 0:1] + prefix - 1.0
    dest_ref[...] = jnp.sum(jnp.where(onehot, slot, 0.0), axis=0, keepdims=True).astype(I32)
    base[...] = base[...] + jnp.sum(ohf, axis=1, keepdims=True)


def _plan(eid_row, counts):
    a = eid_row.shape[1]
    ta = max(t for t in range(LANE, PLAN_TILE_MAX + 1, LANE) if a % t == 0)
    return pl.pallas_call(
        functools.partial(_plan_kernel, ta=ta),
        out_shape=jax.ShapeDtypeStruct((1, a), I32),
        grid=(a // ta,),
        in_specs=[pl.BlockSpec((1, ta), lambda t: (0, t)), pl.BlockSpec((LANE, LANE), lambda t: (0, 0))],
        out_specs=pl.BlockSpec((1, ta), lambda t: (0, t)),
        scratch_shapes=[pltpu.VMEM((LANE, LANE), F32)] * 2,
        compiler_params=pltpu.CompilerParams(dimension_semantics=("arbitrary",)),
        name="moe_plan",
    )(eid_row, jnp.broadcast_to(counts[:, None], (LANE, LANE)))


def _dest_tiles(dest_t, tm):
    n = dest_t.shape[1]
    return dest_t.reshape(2, n // tm, tm).transpose(1, 0, 2).reshape(n // tm, 2 * tm)


def _dispatch_kernel(pend_ref, nused_ref, dest_hbm, *rest, tm, starts, n_blk):
    n_src = len(starts) - 1
    hf_refs = rest[:n_src]
    xs_out, idx_smem, sem_idx, sem_rows, zbuf, sem_zero = rest[n_src:]
    i = pl.program_id(0)
    n_tiles = 2 * pl.num_programs(0)

    def idx_copy(k, sl):
        return pltpu.make_async_copy(dest_hbm.at[k], idx_smem.at[sl], sem_idx.at[sl])

    @pl.when(i == 0)
    def _():
        idx_copy(0, 0).start()
        zbuf[...] = jnp.zeros(zbuf.shape, zbuf.dtype)

        def zero_block(row0):
            return pltpu.make_async_copy(zbuf, xs_out.at[pl.ds(pl.multiple_of(row0, MOE_BLK), MOE_BLK)], sem_zero)

        def nonempty(e):
            return pend_ref[e] > jnp.where(e == 0, 0, pend_ref[jnp.maximum(e - 1, 0)])

        def visit(fn):
            def expert(e, carry):
                @pl.when(nonempty(e))
                def _():
                    fn(zero_block(pend_ref[e] - MOE_BLK))
                return carry

            def unused(b, carry):
                fn(zero_block(b * MOE_BLK))
                return carry

            lax.fori_loop(0, N_EXPERTS, expert, 0)
            lax.fori_loop(nused_ref[0], n_blk, unused, 0)

        visit(lambda cp: cp.start())
        visit(lambda cp: cp.wait())

    for g in range(n_src):
        hf_ref = hf_refs[g]

        @pl.when(jnp.logical_and(i >= starts[g], i < starts[g + 1]))
        def _():
            for par in range(2):
                t = 2 * i + par
                idx_copy(t, par).wait()

                @pl.when(t + 1 < n_tiles)
                def _():
                    idx_copy(t + 1, 1 - par).start()

                for r in range(tm):
                    for k in range(2):
                        row = idx_smem[par, k * tm + r]
                        pltpu.make_async_copy(hf_ref.at[pl.ds(par * tm + r, 1)], xs_out.at[pl.ds(row, 1)],
                                              sem_rows).start(priority=k)
            for _ in range(2):
                pltpu.make_async_copy(hf_ref, xs_out.at[pl.ds(0, 2 * tm)], sem_rows).wait()


def _dispatch(hfs, dests, pends, nused, n_blk, tm):
    w = hfs[0].shape[1]
    assert all(hf.shape[0] % (2 * tm) == 0 for hf in hfs)
    tiles = [hf.shape[0] // (2 * tm) for hf in hfs]
    starts = [sum(tiles[:g]) for g in range(len(hfs) + 1)]

    def src_spec(g):
        return pl.BlockSpec((2 * tm, w), lambda i, pe, nu: (jnp.clip(i - starts[g], 0, tiles[g] - 1), 0))

    return pl.pallas_call(
        functools.partial(_dispatch_kernel, tm=tm, starts=tuple(starts), n_blk=n_blk),
        out_shape=jax.ShapeDtypeStruct((n_blk * MOE_BLK, w), hfs[0].dtype),
        grid_spec=pltpu.PrefetchScalarGridSpec(
            num_scalar_prefetch=2,
            grid=(starts[-1],),
            in_specs=[pl.BlockSpec(memory_space=pl.ANY)] + [src_spec(g) for g in range(len(hfs))],
            out_specs=pl.BlockSpec(memory_space=pl.ANY),
            scratch_shapes=[
                pltpu.SMEM((2, 2 * tm), I32),
                pltpu.SemaphoreType.DMA((2,)),
                pltpu.SemaphoreType.DMA,
                pltpu.VMEM((MOE_BLK, w), hfs[0].dtype),
                pltpu.SemaphoreType.DMA,
            ],
        ),
        compiler_params=pltpu.CompilerParams(
            dimension_semantics=("arbitrary",), vmem_limit_bytes=VMEM_LIMIT),
        name="moe_dispatch",
    )(pends, nused, jnp.concatenate([_dest_tiles(d, tm) for d in dests], axis=0), *hfs)


def _experts_kernel(bexp_ref, nused_ref, xs_ref, w1_ref, w3_ref, w2_ref, out_ref, w13, w2s, *, d_exp):
    i = pl.program_id(0)
    nused = nused_ref[0]

    @pl.when(i < nused)
    def _():
        changed = jnp.logical_or(i == 0, bexp_ref[i] != bexp_ref[jnp.maximum(i - 1, 0)])

        @pl.when(changed)
        def _():
            w13[:, :d_exp] = w1_ref[0, 0].astype(BF16)
            w13[:, d_exp:] = w3_ref[0, 0].astype(BF16)
            w2s[...] = w2_ref[0, 0].astype(BF16)

        hcat = jnp.dot(_unpack_bf16_pairs(xs_ref[...]), w13[...], preferred_element_type=F32)
        a = hcat[:, :d_exp]
        act = (a * jax.nn.sigmoid(a)) * hcat[:, d_exp:]
        out_ref[...] = jnp.dot(act.astype(BF16), w2s[...], preferred_element_type=F32)

    @pl.when(i >= nused)
    def _():
        out_ref[...] = jnp.zeros(out_ref.shape, F32)


def _experts(xs, blk_exp, nused, l, w1, w3, w2):
    p_rows, dw = xs.shape
    d, d_exp = w1.shape[-2:]
    assert dw * 2 == d
    n_blk = p_rows // MOE_BLK
    wspec = lambda r, c: pl.BlockSpec((1, 1, r, c), lambda i, be, nu: (l, be[jnp.minimum(i, nu[0] - 1)], 0, 0))
    return pl.pallas_call(
        functools.partial(_experts_kernel, d_exp=d_exp),
        out_shape=jax.ShapeDtypeStruct((p_rows, d), F32),
        grid_spec=pltpu.PrefetchScalarGridSpec(
            num_scalar_prefetch=2,
            grid=(n_blk,),
            in_specs=[
                pl.BlockSpec((MOE_BLK, dw), lambda i, be, nu: (jnp.minimum(i, nu[0] - 1), 0)),
                wspec(d, d_exp), wspec(d, d_exp), wspec(d_exp, d),
            ],
            out_specs=pl.BlockSpec((MOE_BLK, d), lambda i, be, nu: (i, 0)),
            scratch_shapes=[
                pltpu.VMEM((d, 2 * d_exp), BF16),
                pltpu.VMEM((d_exp, d), BF16),
            ],
        ),
        compiler_params=pltpu.CompilerParams(
            dimension_semantics=("arbitrary",), vmem_limit_bytes=VMEM_LIMIT),
        name="moe_experts",
    )(blk_exp, nused, xs, w1, w3, w2)


def _moe_layer(l, groups, P):
    eid_all = jnp.concatenate([g[1] for g in groups], axis=1)
    n_all = eid_all.shape[1]
    a = 2 * n_all
    assert a % MOE_BLK == 0
    n_blk = a // MOE_BLK + N_EXPERTS
    counts_f = sum(g[2][0] for g in groups)
    dest_row = _plan(eid_all.reshape(1, a), counts_f)
    counts = counts_f[:N_EXPERTS].astype(I32)
    pends = jnp.cumsum((counts + MOE_BLK - 1) // MOE_BLK * MOE_BLK)
    blk_start = jnp.arange(n_blk, dtype=I32) * MOE_BLK
    blk_exp = jnp.minimum(jnp.sum((pends[None, :] <= blk_start[:, None]).astype(I32), axis=1), N_EXPERTS - 1)
    nused = (pends[-1] // MOE_BLK).astype(I32).reshape(1)
    dest = dest_row.reshape(2, n_all)
    dests = []
    off = 0
    for hf, _, _ in groups:
        dests.append(dest[:, off:off + hf.shape[0]])
        off += hf.shape[0]
    xs = _dispatch([g[0] for g in groups], dests, pends.astype(I32), nused, n_blk, DISPATCH_TILE)
    yr = _experts(xs, blk_exp.astype(I32), nused, l, P["w1"], P["w3"], P["w2"])
    return yr, dests


def _gather_begin(step, slot, nsteps, idx_hbm, src_hbm, idx_smem, buf, sem_idx, sem_rows, nrows, n_chunks):
    nslot = 1 - slot
    last = nsteps - 1
    nxt = jnp.minimum(step + 1, last)
    nxt2 = jnp.minimum(step + 2, last)
    per = nrows // n_chunks

    def idx_copy(k, sl):
        return pltpu.make_async_copy(idx_hbm.at[k], idx_smem.at[sl], sem_idx.at[sl])

    def row_copy(sl, r):
        return pltpu.make_async_copy(src_hbm.at[pl.ds(idx_smem[sl, r], 1)], buf.at[sl, pl.ds(r, 1)],
                                     sem_rows.at[sl])

    def rows_wait(sl):
        pltpu.make_async_copy(src_hbm.at[pl.ds(0, nrows)], buf.at[sl], sem_rows.at[sl]).wait()

    @pl.when(step == 0)
    def _():
        first = idx_copy(0, 0)
        first.start()
        first.wait()

        def body(g, carry):
            for u in range(GATHER_UNROLL):
                row_copy(0, g * GATHER_UNROLL + u).start()
            return carry
        lax.fori_loop(0, nrows // GATHER_UNROLL, body, 0)
        idx_copy(nxt, 1).start()

    idx_copy(nxt, nslot).wait()
    rows_wait(slot)
    idx_copy(nxt2, slot).start()

    def issue_chunk(c):
        for r in range(c * per, (c + 1) * per):
            row_copy(nslot, r).start(priority=r % 2)

    def finish():
        @pl.when(step == last)
        def _():
            rows_wait(nslot)
            idx_copy(nxt2, slot).wait()

    return issue_chunk, finish


def _combine(slot, x1_ref, gate_ref, ybuf, tm):
    g = gate_ref[...]
    y = g[:, 0:1] * ybuf[slot, 0:tm, :] + g[:, 1:2] * ybuf[slot, tm:2 * tm, :]
    return x1_ref[...] + y


def _mid_kernel(dest_hbm, yr_hbm, x1_ref, gate_ref, cos_ref, sin_ref,
                gkv_ref, wkvt_ref, gmix_ref, wqt_ref, wqm_ref,
                x2_ref, kt_ref, vt_ref, qt_ref, qm_ref,
                idx_smem, ybuf, sem_idx, sem_rows, *, tm, d_att):
    n_heads = d_att // HEAD_DIM
    n_chunks = 8
    for par in range(2):
        tok = slice(par * tm, (par + 1) * tm)
        issue_chunk, finish = _gather_begin(2 * pl.program_id(0) + par, par, 2 * pl.num_programs(0),
                                            dest_hbm, yr_hbm, idx_smem, ybuf, sem_idx, sem_rows,
                                            2 * tm, n_chunks)
        chunks = iter(range(n_chunks))
        x2 = _combine(par, x1_ref.at[tok], gate_ref.at[tok], ybuf, tm)
        x2_ref[tok, :] = x2
        issue_chunk(next(chunks))
        cos_t = cos_ref[:, tok]
        sin_t = sin_ref[:, tok]
        hk = _rms(x2, gkv_ref[...]).astype(BF16)
        issue_chunk(next(chunks))
        vt_ref[0, :, tok] = lax.dot_general(wkvt_ref[d_att:, :], hk, NT_DIMS, preferred_element_type=F32)
        issue_chunk(next(chunks))
        kt_ref[0, :, tok] = lax.dot_general(wkvt_ref[:d_att, :], hk, NT_DIMS, preferred_element_type=F32)
        issue_chunk(next(chunks))
        _rot_t(kt_ref, tok, cos_t, sin_t, n_heads)
        hq = _rms(x2, gmix_ref[...]).astype(BF16)
        issue_chunk(next(chunks))
        qt_ref[0, :, tok] = lax.dot_general(wqt_ref[...], hq, NT_DIMS, preferred_element_type=F32)
        issue_chunk(next(chunks))
        _rot_t(qt_ref, tok, cos_t, sin_t, n_heads)
        issue_chunk(next(chunks))
        qm_ref[tok, :] = jnp.dot(hq, wqm_ref[...], preferred_element_type=F32)
        for c in chunks:
            issue_chunk(c)
        finish()


def _mid(x1, gate, yr, dest, pos, n_seq, P, tm):
    n, d = x1.shape
    d_att = P["w_kv_t_bf"].shape[0] // 2
    tg = 2 * tm
    s_out = n // n_seq
    assert s_out % tg == 0
    nt = s_out // tg
    half = ROT_DIM // 2
    reps = s_out // pos.shape[0]
    inv = ROPE_THETA ** (-jnp.arange(half, dtype=F32) * (2.0 / ROT_DIM))
    ang_t = jnp.tile(inv[:, None] * pos.astype(F32)[None, :], (1, reps))
    const = lambda *shape: pl.BlockSpec(shape, lambda i: (0,) * len(shape))
    tok = lambda w: pl.BlockSpec((tg, w), lambda i: (i, 0))
    tbl_t = pl.BlockSpec((half, tg), lambda i: (0, i % nt))
    feat_t = pl.BlockSpec((1, d_att, tg), lambda i: (i // nt, 0, i % nt))
    feat_shape = jax.ShapeDtypeStruct((n_seq, d_att, s_out), F32)
    return pl.pallas_call(
        functools.partial(_mid_kernel, tm=tm, d_att=d_att),
        out_shape=(jax.ShapeDtypeStruct((n, d), F32), feat_shape, feat_shape, feat_shape,
                   jax.ShapeDtypeStruct((n, D_MEMQ), F32)),
        grid=(n // tg,),
        in_specs=[
            pl.BlockSpec(memory_space=pl.ANY), pl.BlockSpec(memory_space=pl.ANY),
            tok(d), tok(2), tbl_t, tbl_t,
            const(1, d), const(2 * d_att, d), const(1, d), const(d_att, d), const(d, D_MEMQ),
        ],
        out_specs=(tok(d), feat_t, feat_t, feat_t, tok(D_MEMQ)),
        scratch_shapes=[
            pltpu.SMEM((2, 2 * tm), I32),
            pltpu.VMEM((2, 2 * tm, d), F32),
            pltpu.SemaphoreType.DMA((2,)),
            pltpu.SemaphoreType.DMA((2,)),
        ],
        compiler_params=pltpu.CompilerParams(
            dimension_semantics=("arbitrary",), vmem_limit_bytes=VMEM_LIMIT),
        name="mid",
    )(_dest_tiles(dest, tm), yr, x1, gate, jnp.cos(ang_t), jnp.sin(ang_t),
      P["g_kv"].reshape(1, d), P["w_kv_t_bf"], P["g_mix"][1].reshape(1, d), P["w_q_t_bf"], P["w_qm_bf"])


def _final_kernel(dest_hbm, yr_hbm, x1_ref, gate_ref, g_ref, y_ref, idx_smem, ybuf, sem_idx, sem_rows, *, tm):
    for par in range(2):
        tok = slice(par * tm, (par + 1) * tm)
        issue_chunk, finish = _gather_begin(2 * pl.program_id(0) + par, par, 2 * pl.num_programs(0),
                                            dest_hbm, yr_hbm, idx_smem, ybuf, sem_idx, sem_rows, 2 * tm, 1)
        issue_chunk(0)
        y_ref[tok, :] = _rms(_combine(par, x1_ref.at[tok], gate_ref.at[tok], ybuf, tm), g_ref[...])
        finish()


def _final(x1, gate, yr, dest, g_final, tm):
    n, d = x1.shape
    assert n % (2 * tm) == 0
    tok = lambda w: pl.BlockSpec((2 * tm, w), lambda i: (i, 0))
    return pl.pallas_call(
        functools.partial(_final_kernel, tm=tm),
        out_shape=jax.ShapeDtypeStruct((n, d), F32),
        grid=(n // (2 * tm),),
        in_specs=[
            pl.BlockSpec(memory_space=pl.ANY), pl.BlockSpec(memory_space=pl.ANY),
            tok(d), tok(2), pl.BlockSpec((1, d), lambda i: (0, 0)),
        ],
        out_specs=tok(d),
        scratch_shapes=[
            pltpu.SMEM((2, 2 * tm), I32),
            pltpu.VMEM((2, 2 * tm, d), F32),
            pltpu.SemaphoreType.DMA((2,)),
            pltpu.SemaphoreType.DMA((2,)),
        ],
        compiler_params=pltpu.CompilerParams(
            dimension_semantics=("arbitrary",), vmem_limit_bytes=VMEM_LIMIT),
        name="final",
    )(_dest_tiles(dest, tm), yr, x1, gate, g_final.reshape(1, d))


def _moba_prompt_kernel(qt_ref, kt_ref, vt_ref, o_ref, kmean, krows, vones, sel_sc, s_buf, p_buf, acc,
                        *, s_len, n_grp):
    qb_i = pl.program_id(2)
    nb = s_len // MOBA_BLOCK
    n_h = LANE // HEAD_DIM
    tq = MOBA_BLOCK
    nq = n_h * tq
    scale = HEAD_DIM ** -0.5
    feat = lax.broadcasted_iota(I32, (LANE, MOBA_BLOCK), 0) // HEAD_DIM
    groups = range(n_grp)

    @pl.when(qb_i == 0)
    def _():
        kmean[...] = jnp.zeros(kmean.shape, F32)
        for g in groups:
            rows = slice(g * LANE, (g + 1) * LANE)
            for jb in range(nb):
                cols = slice(jb * MOBA_BLOCK, (jb + 1) * MOBA_BLOCK)
                kr = kt_ref[0, rows, cols].T
                kmean[g, jb:jb + 1, :] = jnp.sum(kr, axis=0, keepdims=True) * (1.0 / MOBA_BLOCK)
                krows[g, jb] = kr.astype(BF16)
                vblk = vt_ref[0, rows, cols]
                for hh in range(n_h):
                    vones[g, hh, jb] = jnp.where(feat == hh, vblk, 1.0).astype(BF16)

    key = lax.broadcasted_iota(I32, (MOBA_BLOCK, nq), 0)
    qcol = lax.broadcasted_iota(I32, (MOBA_BLOCK, nq), 1) % tq
    qs, m0, p_own = [], [], []
    for g in groups:
        qt = qt_ref[0, g * LANE:(g + 1) * LANE, :]
        qstack = jnp.concatenate([jnp.where(feat == hh, qt, 0.0) for hh in range(n_h)], axis=1)
        gate = jnp.dot(kmean[g], qstack, preferred_element_type=F32, precision=HIGHEST)
        brow = lax.broadcasted_iota(I32, gate.shape, 0)
        gate = jnp.where(brow < qb_i, gate, NEG_INF)
        sel = _top_k_mask(gate, brow, 0)
        for jb in range(nb):
            sel_sc[g, jb] = sel[jb:jb + 1, :]
        qs.append((qstack * scale).astype(BF16))
        s_own = jnp.where(key <= qcol, jnp.dot(krows[g, qb_i], qs[g], preferred_element_type=F32), NEG_INF)
        m0.append(jnp.max(s_own, axis=0, keepdims=True))
        p_own.append(jnp.exp(s_own - m0[g]).astype(BF16))

    def put_scores(g, j, slot):
        s_buf[g, slot] = jnp.dot(krows[g, j], qs[g], preferred_element_type=F32)

    def add_block(g, alpha, j, slot):
        for hh in range(n_h):
            acc[g, hh] = (alpha[:, hh * tq:(hh + 1) * tq] * acc[g, hh]
                          + jnp.dot(vones[g, hh, j], p_buf[g, slot, :, hh * tq:(hh + 1) * tq],
                                    preferred_element_type=F32))

    for g in groups:
        acc[g] = jnp.zeros((n_h, LANE, tq), F32)
        p_buf[g, 1] = p_own[g]
        put_scores(g, 0, 0)

    def half_trip(j, slot, carry):
        out = []
        for g in groups:
            m, alpha_prev = carry[g]
            add_block(g, alpha_prev, jnp.where(j == 0, qb_i, j - 1), 1 - slot)
            put_scores(g, jnp.minimum(j + 1, nb - 1), 1 - slot)
            chosen = sel_sc[g, j] > 0.0
            m_new = jnp.maximum(m, jnp.where(chosen, jnp.max(s_buf[g, slot], axis=0, keepdims=True), NEG_INF))
            p_buf[g, slot] = jnp.exp(s_buf[g, slot] - jnp.where(chosen, m_new, jnp.inf)).astype(BF16)
            out.append((m_new, jnp.exp(m - m_new)))
        return tuple(out)

    def body(t, carry):
        return half_trip(2 * t + 1, 1, half_trip(2 * t, 0, carry))

    n_trips = (qb_i + 1) // 2
    one = jnp.ones((1, nq), F32)
    state = lax.fori_loop(0, n_trips, body, tuple((m0[g], one) for g in groups))
    last = jnp.where(n_trips == 0, qb_i, 2 * n_trips - 1)
    for g in groups:
        add_block(g, state[g][1], last, 1)
        out_t = jnp.zeros((LANE, tq), F32)
        for hh in range(n_h):
            other = (1 - hh) * HEAD_DIM
            a_h = acc[g, hh]
            out_t = jnp.where(feat == hh, a_h / a_h[other:other + 1, :], out_t)
        o_ref[:, g * LANE:(g + 1) * LANE] = out_t.T


def _moba_prompt(qt, kt, vt, b, s, n_grp=2):
    d_att = qt.shape[1]
    nb = s // MOBA_BLOCK
    fw = n_grp * LANE
    assert s % MOBA_BLOCK == 0 and LANE // HEAD_DIM == 2 and d_att % fw == 0
    nb_pad = -(-nb // SUBLANE) * SUBLANE
    kv_spec = pl.BlockSpec((1, fw, s), lambda i, hp, qb: (i, hp, 0))
    return pl.pallas_call(
        functools.partial(_moba_prompt_kernel, s_len=s, n_grp=n_grp),
        out_shape=jax.ShapeDtypeStruct((b * s, d_att), F32),
        grid=(b, d_att // fw, nb),
        in_specs=[
            pl.BlockSpec((1, fw, MOBA_BLOCK), lambda i, hp, qb: (i, hp, qb)),
            kv_spec, kv_spec,
        ],
        out_specs=pl.BlockSpec((MOBA_BLOCK, fw), lambda i, hp, qb: (i * nb + qb, hp)),
        scratch_shapes=[
            pltpu.VMEM((n_grp, nb_pad, LANE), F32),
            pltpu.VMEM((n_grp, nb, MOBA_BLOCK, LANE), BF16),
            pltpu.VMEM((n_grp, 2, nb, LANE, MOBA_BLOCK), BF16),
            pltpu.VMEM((n_grp, nb, 1, 2 * MOBA_BLOCK), F32),
            pltpu.VMEM((n_grp, 2, MOBA_BLOCK, 2 * MOBA_BLOCK), F32),
            pltpu.VMEM((n_grp, 2, MOBA_BLOCK, 2 * MOBA_BLOCK), BF16),
            pltpu.VMEM((n_grp, 2, LANE, MOBA_BLOCK), F32),
        ],
        compiler_params=pltpu.CompilerParams(
            dimension_semantics=("arbitrary", "arbitrary", "arbitrary"), vmem_limit_bytes=VMEM_LIMIT),
        name="moba_prompt",
    )(qt, kt, vt)


def _head_diag(o, n_heads, ds):
    return jnp.concatenate(
        [o[h * ds:(h + 1) * ds, h * HEAD_DIM:(h + 1) * HEAD_DIM] for h in range(n_heads)], axis=0)


def _moba_sample_kernel(pt_ref, q_ref, kn_ref, vn_ref, *rest, ds, n_heads, nb, bps):
    pages = rest[:4 * bps]
    o_ref, qbd, g_all, m_all, l_all, o_all = rest[4 * bps:]
    jg = pl.program_id(1)
    rows = n_heads * ds
    d_att = n_heads * HEAD_DIM
    scale = HEAD_DIM ** -0.5
    blane = lax.broadcasted_iota(I32, (rows, LANE), 1)

    @pl.when(jg == 0)
    def _():
        q = q_ref[...]
        lane = lax.broadcasted_iota(I32, (ds, d_att), 1)
        for h in range(n_heads):
            qbd[h * ds:(h + 1) * ds, :] = jnp.where(lane // HEAD_DIM == h, q, 0.0)
        g_all[...] = jnp.full((rows, LANE), NEG_INF, F32)
        m_all[...] = jnp.zeros((rows, LANE), F32)
        l_all[...] = jnp.zeros((rows, LANE), F32)

    qb = qbd[...].astype(BF16)
    g_new, m_new, l_new = g_all[...], m_all[...], l_all[...]
    for bi in range(bps):
        k0_ref, k1_ref = pages[2 * bi], pages[2 * bi + 1]
        v0_ref, v1_ref = pages[2 * bps + 2 * bi], pages[2 * bps + 2 * bi + 1]
        j = jg * bps + bi
        kt = jnp.concatenate([k0_ref[0].astype(BF16), k1_ref[0].astype(BF16)], axis=1)
        vt = jnp.concatenate([v0_ref[0].astype(BF16), v1_ref[0].astype(BF16)], axis=1)
        s = jnp.dot(qb, kt, preferred_element_type=F32)
        g = jnp.sum(s, axis=-1, keepdims=True) * (1.0 / MOBA_BLOCK)
        m = jnp.max(s, axis=-1, keepdims=True) * scale
        p = jnp.exp(s * scale - m)
        l = jnp.sum(p, axis=-1, keepdims=True)
        o = lax.dot_general(p.astype(BF16), vt, NT_DIMS, preferred_element_type=F32)
        o_all[j] = _head_diag(o, n_heads, ds)
        here = blane == j
        g_new = jnp.where(here, g, g_new)
        m_new = jnp.where(here, m, m_new)
        l_new = jnp.where(here, l, l_new)
    g_all[...] = g_new
    m_all[...] = m_new
    l_all[...] = l_new

    @pl.when(jg == nb // bps - 1)
    def _():
        chosen = _top_k_mask(g_new, blane, -1) > 0.0
        s_own = lax.dot_general(qb, kn_ref[...].astype(BF16), NT_DIMS, preferred_element_type=F32) * scale
        r_i = lax.broadcasted_iota(I32, (rows, ds), 0) % ds
        c_i = lax.broadcasted_iota(I32, (rows, ds), 1)
        s_own = jnp.where(c_i <= r_i, s_own, NEG_INF)
        m_sel = jnp.max(jnp.where(chosen, m_new, NEG_INF), axis=-1, keepdims=True)
        m_tot = jnp.maximum(m_sel, jnp.max(s_own, axis=-1, keepdims=True))
        wgt = jnp.where(chosen, jnp.exp(m_new - m_tot), 0.0)
        p_own = jnp.exp(s_own - m_tot)
        l_tot = jnp.sum(wgt * l_new, axis=-1, keepdims=True) + jnp.sum(p_own, axis=-1, keepdims=True)
        acc = _head_diag(jnp.dot(p_own.astype(BF16), vn_ref[...].astype(BF16), preferred_element_type=F32),
                         n_heads, ds)
        for jj in range(nb):
            acc = acc + wgt[:, jj:jj + 1] * o_all[jj]
        o_ref[0] = acc / l_tot


def _moba_sample(qb, k_new, v_new, cache_k, cache_v, page_table, db, ds, bps=4):
    n_pool, page, n_heads, hd = cache_k.shape
    d_att = n_heads * hd
    n_pages = page_table.shape[1]
    ppb = MOBA_BLOCK // page
    nb = n_pages // ppb
    rows = n_heads * ds
    assert ppb == 2 and n_pages % ppb == 0 and ds <= MOBA_BLOCK and nb <= LANE and nb % bps == 0
    ck = cache_k.transpose(0, 2, 3, 1).reshape(n_pool, d_att, page)
    cv = cache_v.transpose(0, 2, 3, 1).reshape(n_pool, d_att, page)
    seq = pl.BlockSpec((ds, d_att), lambda i, j, pt: (i, 0))
    ppg = ppb * bps
    pg = lambda off: pl.BlockSpec((1, d_att, page), lambda i, j, pt: (pt[i * n_pages + ppg * j + off], 0, 0))
    out = pl.pallas_call(
        functools.partial(_moba_sample_kernel, ds=ds, n_heads=n_heads, nb=nb, bps=bps),
        out_shape=jax.ShapeDtypeStruct((db, rows, hd), F32),
        grid_spec=pltpu.PrefetchScalarGridSpec(
            num_scalar_prefetch=1,
            grid=(db, nb // bps),
            in_specs=[seq, seq, seq] + [pg(o) for o in range(ppg)] * 2,
            out_specs=pl.BlockSpec((1, rows, hd), lambda i, j, pt: (i, 0, 0)),
            scratch_shapes=[
                pltpu.VMEM((rows, d_att), F32),
                pltpu.VMEM((rows, LANE), F32),
                pltpu.VMEM((rows, LANE), F32),
                pltpu.VMEM((rows, LANE), F32),
                pltpu.VMEM((nb, rows, hd), F32),
            ],
        ),
        compiler_params=pltpu.CompilerParams(
            dimension_semantics=("arbitrary", "arbitrary"), vmem_limit_bytes=VMEM_LIMIT),
        name="moba_sample",
    )(page_table.reshape(-1), qb, k_new, v_new, *([ck] * ppg), *([cv] * ppg))
    return out.reshape(db, n_heads, ds, hd).transpose(0, 2, 1, 3).reshape(db * ds, d_att)


def kernel(x_prompt, x_sample, state_conv, cache_k, cache_v, cache_mem_k, cache_mem_v, page_table,
           mem_prompt, g_mix, g_ffn, g_final, g_mem, w_mem_kv, w_in_a, conv_w, conv_b, cln_g, cln_b,
           w_out_a, g_kv, w_kv, w_in_b, w_out_b, w_rg, b_rg, w_re, b_re, w1, w3, w2):
    depth = g_mix.shape[0]
    n_heads = w_kv.shape[1] // (2 * HEAD_DIM)
    d_att = n_heads * HEAD_DIM
    routers = [_router_weights(w_rg[l], b_rg[l], w_re[l], b_re[l]) for l in range(depth)]
    P = {
        "g_mix": g_mix, "g_ffn": g_ffn, "g_final": g_final, "g_kv": g_kv,
        "w_in_a_bf": w_in_a[0].astype(BF16), "w_out_a_bf": w_out_a[0].astype(BF16),
        "w_kv_t_bf": w_kv.T.astype(BF16),
        "w_q_t_bf": w_in_b[0][:, :d_att].T.astype(BF16), "w_qm_bf": w_in_b[0][:, d_att:].astype(BF16),
        "w_out_b_bf": w_out_b[0].astype(BF16),
        "conv_w": conv_w, "conv_b": conv_b, "cln_g": cln_g, "cln_b": cln_b,
        "wr": [r[0] for r in routers], "br": [r[1] for r in routers],
        "w1": w1, "w3": w3, "w2": w2,
    }
    b_p, s_p, d = x_prompt.shape
    db, ds, _ = x_sample.shape
    n_p, n_s = b_p * s_p, db * ds
    mem_len = mem_prompt.shape[1]
    past = page_table.shape[1] * cache_k.shape[1]
    assert past % MOBA_BLOCK == 0
    ts_p, tm_p = 256, 256
    ts_s, tm_s = ds, n_s // 2

    memkv_t = _memkv(mem_prompt, g_mem, w_mem_kv.transpose(0, 2, 1).astype(BF16))
    kv6 = memkv_t.reshape(depth, b_p, 2, MEM_H, HEAD_DIM, mem_len).transpose(2, 0, 1, 5, 3, 4)
    mem_k_p, mem_v_p = kv6[0], kv6[1]
    cmk = cache_mem_k.transpose(0, 1, 3, 4, 2).reshape(depth, db, D_MEMQ, mem_len)
    cmv = cache_mem_v.transpose(0, 1, 3, 4, 2).reshape(depth, db, D_MEMQ, mem_len)
    mem_p = (memkv_t, memkv_t, (0, 1))
    mem_s = (cmk, cmv, (0, 0))

    ctx_p = jnp.zeros((b_p, CONV_W - 1, state_conv.shape[-1]), F32)
    x1_p, hf_p, gate_p, eid_p, cnt_p, conv_p = _layer0(x_prompt.reshape(n_p, d), b_p, s_p, ctx_p, *mem_p, 0, P, ts_p)
    x1_s, hf_s, gate_s, eid_s, cnt_s, conv_s = _layer0(x_sample.reshape(n_s, d), db, ds, state_conv[0], *mem_s, 0, P, ts_s)
    yr, (dest_p, dest_s) = _moe_layer(0, [(hf_p, eid_p, cnt_p), (hf_s, eid_s.T, cnt_s)], P)

    x2_p, kt_p, vt_p, qt_p, qm_p = _mid(x1_p, gate_p, yr, dest_p, jnp.arange(s_p), b_p, P, tm_p)
    x2_s, kt_s, vt_s, qt_s, qm_s = _mid(x1_s, gate_s, yr, dest_s, past + jnp.arange(ds), 1, P, tm_s)
    c1_p = _moba_prompt(qt_p, kt_p, vt_p, b_p, s_p)
    k_new, v_new = kt_s[0].T, vt_s[0].T
    c1_s = _moba_sample(qt_s[0].T, k_new, v_new, cache_k, cache_v, page_table, db, ds)
    x3_p, hf_p, gate_p, eid_p, cnt_p = _layer1(x2_p, c1_p, qm_p, b_p, s_p, *mem_p, 1, P, ts_p)
    x3_s, hf_s, gate_s, eid_s, cnt_s = _layer1(x2_s, c1_s, qm_s, db, ds, *mem_s, 1, P, ts_s)
    yr, (dest_p, dest_s) = _moe_layer(1, [(hf_p, eid_p, cnt_p), (hf_s, eid_s.T, cnt_s)], P)
    y_p = _final(x3_p, gate_p, yr, dest_p, g_final, tm_p).reshape(b_p, s_p, d)
    y_s = _final(x3_s, gate_s, yr, dest_s, g_final, tm_s).reshape(db, ds, d)

    k_p = kt_p.reshape(b_p, n_heads, HEAD_DIM, s_p).transpose(0, 3, 1, 2)
    v_p = vt_p.reshape(b_p, n_heads, HEAD_DIM, s_p).transpose(0, 3, 1, 2)
    k_s = k_new.reshape(db, ds, n_heads, HEAD_DIM)
    v_s = v_new.reshape(db, ds, n_heads, HEAD_DIM)
    return (y_p, y_s, conv_p[None], conv_s[None], k_p, v_p, k_s, v_s, mem_k_p, mem_v_p)
```

```python
import functools
import math

import jax
import jax.numpy as jnp
from jax import lax
from jax.experimental import pallas as pl
from jax.experimental.pallas import tpu as pltpu

F32 = jnp.float32
BF16 = jnp.bfloat16
I32 = jnp.int32
U32 = jnp.uint32
HIGHEST = lax.Precision.HIGHEST

EPS = 1e-6
HEAD_DIM = 64
MEM_H = 4
D_MEMQ = MEM_H * HEAD_DIM
CONV_W = 31
CTX_ROWS = 32
CTX_PAD = CTX_ROWS - (CONV_W - 1)
MOBA_BLOCK = 256
MOBA_TOPK = 3
ROT_DIM = HEAD_DIM // 4
ROPE_THETA = 500000.0
N_GROUPS = 4
E_PER_GROUP = 16
N_EXPERTS = N_GROUPS * E_PER_GROUP
MOE_BLK = 256
PLAN_TILE_MAX = 512
GATHER_UNROLL = 8
DISPATCH_TILE = 128
ROUTER_LANES = 128
LANE = 128
SUBLANE = 8
VMEM_LIMIT = 56 * 1024 * 1024

NEG_INF = float("-inf")
BIG_IDX = 1 << 20
NT_DIMS = (((1,), (1,)), ((), ()))


def _rms(x, g):
    return x * lax.rsqrt(jnp.mean(x * x, axis=-1, keepdims=True) + EPS) * g


def _pack_bf16_pairs(x):
    half = x.shape[1] // 2
    hi = lax.bitcast_convert_type(x[:, :half].astype(BF16).astype(F32), U32)
    lo = lax.bitcast_convert_type(x[:, half:].astype(BF16).astype(F32), U32)
    return hi | (lo >> 16)


def _unpack_bf16_pairs(u):
    hi = lax.bitcast_convert_type(u & jnp.uint32(0xFFFF0000), F32)
    lo = lax.bitcast_convert_type(u << 16, F32)
    return jnp.concatenate([hi, lo], axis=1).astype(BF16)


def _first_argmax(vals, idx, axis):
    m = jnp.max(vals, axis=axis, keepdims=True)
    first = jnp.min(jnp.where(vals == m, idx, BIG_IDX), axis=axis, keepdims=True)
    return m, first


def _top_k_mask(gate, idx, axis):
    sel = jnp.zeros(gate.shape, F32)
    for _ in range(MOBA_TOPK):
        m, first = _first_argmax(gate, idx, axis)
        pick = jnp.logical_and(idx == first, m > NEG_INF)
        sel = jnp.where(pick, 1.0, sel)
        gate = jnp.where(pick, NEG_INF, gate)
    return sel


def _rot_t(xt_ref, tok, cos_t, sin_t, n_heads):
    half = ROT_DIM // 2
    for hh in range(n_heads):
        r0 = hh * HEAD_DIM
        xa = xt_ref[0, r0:r0 + half, tok]
        xb = xt_ref[0, r0 + half:r0 + ROT_DIM, tok]
        xt_ref[0, r0:r0 + half, tok] = xa * cos_t - xb * sin_t
        xt_ref[0, r0 + half:r0 + ROT_DIM, tok] = xb * cos_t + xa * sin_t


def _memkv_kernel(mem_ref, g_ref, wt_ref, out_ref):
    h = _rms(mem_ref[0], g_ref[0]).astype(BF16)
    out_ref[0, 0] = lax.dot_general(wt_ref[0], h, NT_DIMS, preferred_element_type=F32)


def _memkv(mem, g_mem, w_mem_kv_t_bf):
    b, m, d = mem.shape
    depth = g_mem.shape[0]
    n_out = w_mem_kv_t_bf.shape[1]
    return pl.pallas_call(
        _memkv_kernel,
        out_shape=jax.ShapeDtypeStruct((depth, b, n_out, m), F32),
        grid=(depth, b),
        in_specs=[
            pl.BlockSpec((1, m, d), lambda l, i: (i, 0, 0)),
            pl.BlockSpec((1, 1, d), lambda l, i: (l, 0, 0)),
            pl.BlockSpec((1, n_out, d), lambda l, i: (l, 0, 0)),
        ],
        out_specs=pl.BlockSpec((1, 1, n_out, m), lambda l, i: (l, i, 0, 0)),
        compiler_params=pltpu.CompilerParams(dimension_semantics=("arbitrary", "arbitrary")),
        name="memkv",
    )(mem, g_mem.reshape(depth, 1, d), w_mem_kv_t_bf)


def _mem_attn_into(mix, off, qm_all, mkt_ref, mvt_ref, nb):
    t = qm_all.shape[0] // nb
    scale = HEAD_DIM ** -0.5
    lane = lax.broadcasted_iota(I32, (t, D_MEMQ), 1)
    for bb in range(nb):
        qm = qm_all[bb * t:(bb + 1) * t]
        mkt = mkt_ref[0, bb].astype(BF16)
        mvt = mvt_ref[0, bb].astype(BF16)
        om = jnp.zeros(qm.shape, F32)
        for hh in range(MEM_H):
            in_head = (lane // HEAD_DIM) == hh
            q = jnp.where(in_head, qm, 0.0).astype(BF16)
            s = jnp.dot(q, mkt, preferred_element_type=F32) * scale
            m = jnp.max(s, axis=-1, keepdims=True)
            e = jnp.exp(s - m)
            l = jnp.sum(e, axis=-1, keepdims=True)
            o = lax.dot_general(e.astype(BF16), mvt, NT_DIMS, preferred_element_type=F32) / l
            om = jnp.where(in_head, o, om)
        mix[bb * t:(bb + 1) * t, off:off + D_MEMQ] = om.astype(BF16)


def _out_and_route(x, mix, wout_ref, gffn_ref, wr_ref, br_ref, x1_ref, hf_ref, gate_ref, eid_ref, cnt_ref, first):
    x1 = x + jnp.dot(mix[...], wout_ref[...], preferred_element_type=F32)
    x1_ref[...] = x1
    hf = _rms(x1, gffn_ref[...])
    hf_ref[...] = _pack_bf16_pairs(hf)
    hf_hi = hf.astype(BF16)
    hf_lo = (hf - hf_hi.astype(F32)).astype(BF16)
    logits = jnp.dot(jnp.concatenate([hf_hi, hf_lo, hf_hi], axis=1), wr_ref[...],
                     preferred_element_type=F32) + br_ref[...]
    lane = lax.broadcasted_iota(I32, logits.shape, 1)
    gl = jnp.where((lane >= N_EXPERTS) & (lane < N_EXPERTS + N_GROUPS), logits, NEG_INF)
    gmax, glane = _first_argmax(gl, lane, -1)
    p_group = 1.0 / jnp.sum(jnp.exp(gl - gmax), axis=-1, keepdims=True)
    e_lo = (glane - N_EXPERTS) * E_PER_GROUP
    el = jnp.where((lane >= e_lo) & (lane < e_lo + E_PER_GROUP), logits, NEG_INF)
    m1, i1 = _first_argmax(el, lane, -1)
    el2 = jnp.where(lane == i1, NEG_INF, el)
    m2, i2 = _first_argmax(el2, lane, -1)
    e2 = jnp.exp(m2 - m1)
    g1 = p_group / (1.0 + e2)
    g2 = g1 * e2
    picked = jnp.where(lane == i1, 1.0, 0.0) + jnp.where(lane == i2, 1.0, 0.0)
    tile_counts = jnp.sum(picked, axis=0, keepdims=True)

    @pl.when(first)
    def _():
        cnt_ref[...] = tile_counts

    @pl.when(jnp.logical_not(first))
    def _():
        cnt_ref[...] = cnt_ref[...] + tile_counts

    lane2 = lax.broadcasted_iota(I32, gate_ref.shape, 1)
    gate_ref[...] = jnp.where(lane2 == 0, g1, g2)
    if eid_ref.shape[1] == 2:
        eid_ref[...] = jnp.where(lane2 == 0, i1, i2)
    else:
        t = x.shape[0]
        diag = lax.broadcasted_iota(I32, (t, t), 0) == lax.broadcasted_iota(I32, (t, t), 1)
        rows = [jnp.sum(jnp.where(diag, col.astype(F32), 0.0), axis=0, keepdims=True) for col in (i1, i2)]
        eid_ref[...] = jnp.concatenate(rows, axis=0).astype(I32)


def _eid_out(n, ts, nt):
    if ts % LANE == 0:
        return jax.ShapeDtypeStruct((2, n), I32), pl.BlockSpec((2, ts), lambda i, t: (0, i * nt + t))
    return jax.ShapeDtypeStruct((n, 2), I32), pl.BlockSpec((ts, 2), lambda i, t: (i * nt + t, 0))


def _router_weights(w_rg, b_rg, w_re, b_re):
    d = w_rg.shape[0]
    pad = ROUTER_LANES - N_EXPERTS - N_GROUPS
    wr = jnp.concatenate([w_re, w_rg, jnp.zeros((d, pad), F32)], axis=1)
    br = jnp.concatenate([b_re, b_rg, jnp.zeros((pad,), F32)]).reshape(1, ROUTER_LANES)
    w_hi = wr.astype(BF16)
    w_lo = (wr - w_hi.astype(F32)).astype(BF16)
    return jnp.concatenate([w_hi, w_hi, w_lo], axis=0), br


def _layer0_kernel(x_ref, ctx_ref, mk_ref, mv_ref, gmix_ref, win_ref, cw_ref, cb_ref, lg_ref, lb_ref,
                   wout_ref, gffn_ref, wr_ref, br_ref,
                   x1_ref, hf_ref, gate_ref, eid_ref, cnt_ref, nc_ref,
                   zext, zsh, cbuf, mix, *, ts, rc, d_conv, nb):
    t = pl.program_id(1)
    nt = pl.num_programs(1)

    x = x_ref[...]
    h = _rms(x, gmix_ref[...]).astype(BF16)
    u = jnp.dot(h, win_ref[...], preferred_element_type=F32)
    z = u[:, :d_conv] * jax.nn.sigmoid(u[:, d_conv:2 * d_conv])
    qm = u[:, 2 * d_conv:]

    for bb in range(nb):
        rows = slice(bb * ts, (bb + 1) * ts)

        @pl.when(t == 0)
        def _():
            zext[bb, 0:CTX_PAD, :] = jnp.zeros((CTX_PAD, d_conv), F32)
            zext[bb, CTX_PAD:CTX_ROWS, :] = ctx_ref[bb]

        @pl.when(t > 0)
        def _():
            zext[bb, 0:CTX_ROWS, :] = zext[bb, ts:ts + CTX_ROWS, :]

        zext[bb, CTX_ROWS:CTX_ROWS + ts, :] = z[rows]

        @pl.when(t == nt - 1)
        def _():
            nc_ref[bb] = zext[bb, ts + CTX_PAD: ts + CTX_ROWS, :]

        for sh in range(1, SUBLANE):
            zsh[bb, sh - 1] = zext[bb, sh:sh + ts + CTX_ROWS - SUBLANE, :]

        def chunk(ci, carry):
            r0 = pl.multiple_of(ci * rc, SUBLANE)
            for cbk in range(d_conv // LANE):
                cs = slice(cbk * LANE, (cbk + 1) * LANE)
                acc = jnp.zeros((rc, LANE), F32) + cb_ref[:, cs]
                for w in range(CONV_W):
                    hi, sh = divmod(w + CTX_PAD, SUBLANE)
                    if sh == 0:
                        win = zext[bb, pl.ds(r0 + hi * SUBLANE, rc), cs]
                    else:
                        win = zsh[bb, sh - 1, pl.ds(r0 + hi * SUBLANE, rc), cs]
                    acc = acc + win * cw_ref[w:w + 1, cs]
                cbuf[pl.ds(bb * ts + r0, rc), cs] = acc
            return carry

        lax.fori_loop(0, ts // rc, chunk, 0)

    c = cbuf[...]
    mu = jnp.mean(c, axis=-1, keepdims=True)
    xc = c - mu
    y = xc * lax.rsqrt(jnp.mean(xc * xc, axis=-1, keepdims=True) + EPS) * lg_ref[...] + lb_ref[...]
    mix[:, :d_conv] = (y * jax.nn.sigmoid(y)).astype(BF16)
    _mem_attn_into(mix, d_conv, qm, mk_ref, mv_ref, nb)
    first = jnp.logical_and(pl.program_id(0) == 0, t == 0)
    _out_and_route(x, mix, wout_ref, gffn_ref, wr_ref, br_ref, x1_ref, hf_ref, gate_ref, eid_ref, cnt_ref, first)


def _layer0(x2d, b, s, ctx, mk_arr, mv_arr, kv_rows, l, P, ts, nb=1):
    n, d = x2d.shape
    nt = s // ts
    assert b % nb == 0 and (nb == 1 or nt == 1)
    tr = nb * ts
    d_conv = ctx.shape[-1]
    rc = min(32, ts)
    d_in = P["w_in_a_bf"].shape[-1]
    krow, vrow = kv_rows
    const = lambda *shape: pl.BlockSpec(shape, lambda i, t: (0,) * len(shape))
    tok = lambda w: pl.BlockSpec((tr, w), lambda i, t: (i * nt + t, 0))
    eid_shape, eid_spec = _eid_out(n, tr, nt)
    kern = functools.partial(_layer0_kernel, ts=ts, rc=rc, d_conv=d_conv, nb=nb)
    return pl.pallas_call(
        kern,
        out_shape=(
            jax.ShapeDtypeStruct((n, d), F32),
            jax.ShapeDtypeStruct((n, d // 2), U32),
            jax.ShapeDtypeStruct((n, 2), F32),
            eid_shape,
            jax.ShapeDtypeStruct((1, LANE), F32),
            jax.ShapeDtypeStruct((b, CONV_W - 1, d_conv), F32),
        ),
        grid=(b // nb, nt),
        in_specs=[
            tok(d),
            pl.BlockSpec((nb, CONV_W - 1, d_conv), lambda i, t: (i, 0, 0)),
            pl.BlockSpec((1, nb, D_MEMQ, mk_arr.shape[3]), lambda i, t: (l, i, krow, 0)),
            pl.BlockSpec((1, nb, D_MEMQ, mv_arr.shape[3]), lambda i, t: (l, i, vrow, 0)),
            const(1, d), const(d, d_in), const(CONV_W, d_conv), const(1, d_conv), const(1, d_conv),
            const(1, d_conv), const(d_conv + D_MEMQ, d), const(1, d), const(3 * d, ROUTER_LANES),
            const(1, ROUTER_LANES),
        ],
        out_specs=(tok(d), tok(d // 2), tok(2), eid_spec, const(1, LANE),
                   pl.BlockSpec((nb, CONV_W - 1, d_conv), lambda i, t: (i, 0, 0))),
        scratch_shapes=[
            pltpu.VMEM((nb, ts + CTX_ROWS, d_conv), F32),
            pltpu.VMEM((nb, SUBLANE - 1, ts + CTX_ROWS - SUBLANE, d_conv), F32),
            pltpu.VMEM((tr, d_conv), F32),
            pltpu.VMEM((tr, d_conv + D_MEMQ), BF16),
        ],
        compiler_params=pltpu.CompilerParams(
            dimension_semantics=("arbitrary", "arbitrary"), vmem_limit_bytes=VMEM_LIMIT),
        name="layer0",
    )(x2d, ctx, mk_arr, mv_arr, P["g_mix"][l].reshape(1, d), P["w_in_a_bf"], P["conv_w"][0],
      P["conv_b"][0].reshape(1, d_conv), P["cln_g"][0].reshape(1, d_conv), P["cln_b"][0].reshape(1, d_conv),
      P["w_out_a_bf"], P["g_ffn"][l].reshape(1, d), P["wr"][l], P["br"][l])


def _layer1_kernel(x_ref, c_ref, qm_ref, mk_ref, mv_ref, wout_ref, gffn_ref, wr_ref, br_ref,
                   x1_ref, hf_ref, gate_ref, eid_ref, cnt_ref, mix, *, d_att, nb):
    mix[:, :d_att] = c_ref[...].astype(BF16)
    _mem_attn_into(mix, d_att, qm_ref[...], mk_ref, mv_ref, nb)
    first = jnp.logical_and(pl.program_id(0) == 0, pl.program_id(1) == 0)
    _out_and_route(x_ref[...], mix, wout_ref, gffn_ref, wr_ref, br_ref, x1_ref, hf_ref, gate_ref, eid_ref,
                   cnt_ref, first)


def _layer1(x2d, c2d, qm2d, b, s, mk_arr, mv_arr, kv_rows, l, P, ts, nb=1):
    n, d = x2d.shape
    nt = s // ts
    assert b % nb == 0 and (nb == 1 or nt == 1)
    tr = nb * ts
    d_att = c2d.shape[-1]
    krow, vrow = kv_rows
    const = lambda *shape: pl.BlockSpec(shape, lambda i, t: (0,) * len(shape))
    tok = lambda w: pl.BlockSpec((tr, w), lambda i, t: (i * nt + t, 0))
    eid_shape, eid_spec = _eid_out(n, tr, nt)
    return pl.pallas_call(
        functools.partial(_layer1_kernel, d_att=d_att, nb=nb),
        out_shape=(
            jax.ShapeDtypeStruct((n, d), F32),
            jax.ShapeDtypeStruct((n, d // 2), U32),
            jax.ShapeDtypeStruct((n, 2), F32),
            eid_shape,
            jax.ShapeDtypeStruct((1, LANE), F32),
        ),
        grid=(b // nb, nt),
        in_specs=[
            tok(d), tok(d_att), tok(D_MEMQ),
            pl.BlockSpec((1, nb, D_MEMQ, mk_arr.shape[3]), lambda i, t: (l, i, krow, 0)),
            pl.BlockSpec((1, nb, D_MEMQ, mv_arr.shape[3]), lambda i, t: (l, i, vrow, 0)),
            const(d_att + D_MEMQ, d), const(1, d), const(3 * d, ROUTER_LANES), const(1, ROUTER_LANES),
        ],
        out_specs=(tok(d), tok(d // 2), tok(2), eid_spec, const(1, LANE)),
        scratch_shapes=[pltpu.VMEM((tr, d_att + D_MEMQ), BF16)],
        compiler_params=pltpu.CompilerParams(
            dimension_semantics=("arbitrary", "arbitrary"), vmem_limit_bytes=VMEM_LIMIT),
        name="layer1",
    )(x2d, c2d, qm2d, mk_arr, mv_arr, P["w_out_b_bf"], P["g_ffn"][l].reshape(1, d), P["wr"][l], P["br"][l])


def _plan_kernel(eid_ref, counts_ref, dest_ref, base, pstart, *, ta):
    t = pl.program_id(0)
    expert = lax.broadcasted_iota(I32, (LANE, ta), 0)
    onehot = eid_ref[...] == expert
    ohf = jnp.where(onehot, 1.0, 0.0)

    @pl.when(t == 0)
    def _():
        counts = counts_ref[...]
        pc = (((counts.astype(I32) + (MOE_BLK - 1)) // MOE_BLK) * MOE_BLK).astype(F32)
        r = lax.broadcasted_iota(I32, (LANE, LANE), 0)
        c = lax.broadcasted_iota(I32, (LANE, LANE), 1)
        below = jnp.where(c < r, 1.0, 0.0)
        pstart[...] = jnp.dot(below, pc, preferred_element_type=F32, precision=HIGHEST)
        base[...] = jnp.zeros(base.shape, F32)

    r = lax.broadcasted_iota(I32, (ta, ta), 0)
    c = lax.broadcasted_iota(I32, (ta, ta), 1)
    upto = jnp.where(r <= c, 1.0, 0.0).astype(BF16)
    prefix = jnp.dot(ohf.astype(BF16), upto, preferred_element_type=F32)
    slot = pstart[:, 0:1] + base[:, 0:1] + prefix - 1.0
    dest_ref[...] = jnp.sum(jnp.where(onehot, slot, 0.0), axis=0, keepdims=True).astype(I32)
    base[...] = base[...] + jnp.sum(ohf, axis=1, keepdims=True)


def _plan(eid_row, counts):
    a = eid_row.shape[1]
    ta = max(t for t in range(LANE, PLAN_TILE_MAX + 1, LANE) if a % t == 0)
    return pl.pallas_call(
        functools.partial(_plan_kernel, ta=ta),
        out_shape=jax.ShapeDtypeStruct((1, a), I32),
        grid=(a // ta,),
        in_specs=[pl.BlockSpec((1, ta), lambda t: (0, t)), pl.BlockSpec((LANE, LANE), lambda t: (0, 0))],
        out_specs=pl.BlockSpec((1, ta), lambda t: (0, t)),
        scratch_shapes=[pltpu.VMEM((LANE, LANE), F32)] * 2,
        compiler_params=pltpu.CompilerParams(dimension_semantics=("arbitrary",)),
        name="moe_plan",
    )(eid_row, jnp.broadcast_to(counts[:, None], (LANE, LANE)))


def _dest_tiles(dest_t, tm):
    n = dest_t.shape[1]
    return dest_t.reshape(2, n // tm, tm).transpose(1, 0, 2).reshape(n // tm, 2 * tm)


def _dispatch_kernel(pend_ref, nused_ref, dest_hbm, *rest, tm, starts, n_blk):
    n_src = len(starts) - 1
    hf_refs = rest[:n_src]
    xs_out, idx_smem, sem_idx, sem_rows, zbuf, sem_zero = rest[n_src:]
    i = pl.program_id(0)
    n_tiles = 2 * pl.num_programs(0)

    def idx_copy(k, sl):
        return pltpu.make_async_copy(dest_hbm.at[k], idx_smem.at[sl], sem_idx.at[sl])

    @pl.when(i == 0)
    def _():
        idx_copy(0, 0).start()
        zbuf[...] = jnp.zeros(zbuf.shape, zbuf.dtype)

        def zero_block(row0):
            return pltpu.make_async_copy(zbuf, xs_out.at[pl.ds(pl.multiple_of(row0, MOE_BLK), MOE_BLK)], sem_zero)

        def nonempty(e):
            return pend_ref[e] > jnp.where(e == 0, 0, pend_ref[jnp.maximum(e - 1, 0)])

        def visit(fn):
            def expert(e, carry):
                @pl.when(nonempty(e))
                def _():
                    fn(zero_block(pend_ref[e] - MOE_BLK))
                return carry

            def unused(b, carry):
                fn(zero_block(b * MOE_BLK))
                return carry

            lax.fori_loop(0, N_EXPERTS, expert, 0)
            lax.fori_loop(nused_ref[0], n_blk, unused, 0)

        visit(lambda cp: cp.start())
        visit(lambda cp: cp.wait())

    for g in range(n_src):
        hf_ref = hf_refs[g]

        @pl.when(jnp.logical_and(i >= starts[g], i < starts[g + 1]))
        def _():
            for par in range(2):
                t = 2 * i + par
                idx_copy(t, par).wait()

                @pl.when(t + 1 < n_tiles)
                def _():
                    idx_copy(t + 1, 1 - par).start()

                for r in range(tm):
                    for k in range(2):
                        row = idx_smem[par, k * tm + r]
                        pltpu.make_async_copy(hf_ref.at[pl.ds(par * tm + r, 1)], xs_out.at[pl.ds(row, 1)],
                                              sem_rows).start(priority=k)
            for _ in range(2):
                pltpu.make_async_copy(hf_ref, xs_out.at[pl.ds(0, 2 * tm)], sem_rows).wait()


def _dispatch(hfs, dests, pends, nused, n_blk, tm):
    w = hfs[0].shape[1]
    assert all(hf.shape[0] % (2 * tm) == 0 for hf in hfs)
    tiles = [hf.shape[0] // (2 * tm) for hf in hfs]
    starts = [sum(tiles[:g]) for g in range(len(hfs) + 1)]

    def src_spec(g):
        return pl.BlockSpec((2 * tm, w), lambda i, pe, nu: (jnp.clip(i - starts[g], 0, tiles[g] - 1), 0))

    return pl.pallas_call(
        functools.partial(_dispatch_kernel, tm=tm, starts=tuple(starts), n_blk=n_blk),
        out_shape=jax.ShapeDtypeStruct((n_blk * MOE_BLK, w), hfs[0].dtype),
        grid_spec=pltpu.PrefetchScalarGridSpec(
            num_scalar_prefetch=2,
            grid=(starts[-1],),
            in_specs=[pl.BlockSpec(memory_space=pl.ANY)] + [src_spec(g) for g in range(len(hfs))],
            out_specs=pl.BlockSpec(memory_space=pl.ANY),
            scratch_shapes=[
                pltpu.SMEM((2, 2 * tm), I32),
                pltpu.SemaphoreType.DMA((2,)),
                pltpu.SemaphoreType.DMA,
                pltpu.VMEM((MOE_BLK, w), hfs[0].dtype),
                pltpu.SemaphoreType.DMA,
            ],
        ),
        compiler_params=pltpu.CompilerParams(
            dimension_semantics=("arbitrary",), vmem_limit_bytes=VMEM_LIMIT),
        name="moe_dispatch",
    )(pends, nused, jnp.concatenate([_dest_tiles(d, tm) for d in dests], axis=0), *hfs)


def _experts_kernel(bexp_ref, nused_ref, xs_ref, w1_ref, w3_ref, w2_ref, out_ref, w13, w2s, *, d_exp):
    i = pl.program_id(0)
    nused = nused_ref[0]

    @pl.when(i < nused)
    def _():
        changed = jnp.logical_or(i == 0, bexp_ref[i] != bexp_ref[jnp.maximum(i - 1, 0)])

        @pl.when(changed)
        def _():
            w13[:, :d_exp] = w1_ref[0, 0].astype(BF16)
            w13[:, d_exp:] = w3_ref[0, 0].astype(BF16)
            w2s[...] = w2_ref[0, 0].astype(BF16)

        hcat = jnp.dot(_unpack_bf16_pairs(xs_ref[...]), w13[...], preferred_element_type=F32)
        a = hcat[:, :d_exp]
        act = (a * jax.nn.sigmoid(a)) * hcat[:, d_exp:]
        out_ref[...] = jnp.dot(act.astype(BF16), w2s[...], preferred_element_type=F32)

    @pl.when(i >= nused)
    def _():
        out_ref[...] = jnp.zeros(out_ref.shape, F32)


def _experts(xs, blk_exp, nused, l, w1, w3, w2):
    p_rows, dw = xs.shape
    d, d_exp = w1.shape[-2:]
    assert dw * 2 == d
    n_blk = p_rows // MOE_BLK
    wspec = lambda r, c: pl.BlockSpec((1, 1, r, c), lambda i, be, nu: (l, be[jnp.minimum(i, nu[0] - 1)], 0, 0))
    return pl.pallas_call(
        functools.partial(_experts_kernel, d_exp=d_exp),
        out_shape=jax.ShapeDtypeStruct((p_rows, d), F32),
        grid_spec=pltpu.PrefetchScalarGridSpec(
            num_scalar_prefetch=2,
            grid=(n_blk,),
            in_specs=[
                pl.BlockSpec((MOE_BLK, dw), lambda i, be, nu: (jnp.minimum(i, nu[0] - 1), 0)),
                wspec(d, d_exp), wspec(d, d_exp), wspec(d_exp, d),
            ],
            out_specs=pl.BlockSpec((MOE_BLK, d), lambda i, be, nu: (i, 0)),
            scratch_shapes=[
                pltpu.VMEM((d, 2 * d_exp), BF16),
                pltpu.VMEM((d_exp, d), BF16),
            ],
        ),
        compiler_params=pltpu.CompilerParams(
            dimension_semantics=("arbitrary",), vmem_limit_bytes=VMEM_LIMIT),
        name="moe_experts",
    )(blk_exp, nused, xs, w1, w3, w2)


def _moe_layer(l, groups, P):
    eid_all = jnp.concatenate([g[1] for g in groups], axis=1)
    n_all = eid_all.shape[1]
    a = 2 * n_all
    assert a % MOE_BLK == 0
    n_blk = a // MOE_BLK + N_EXPERTS
    counts_f = sum(g[2][0] for g in groups)
    dest_row = _plan(eid_all.reshape(1, a), counts_f)
    counts = counts_f[:N_EXPERTS].astype(I32)
    pends = jnp.cumsum((counts + MOE_BLK - 1) // MOE_BLK * MOE_BLK)
    blk_start = jnp.arange(n_blk, dtype=I32) * MOE_BLK
    blk_exp = jnp.minimum(jnp.sum((pends[None, :] <= blk_start[:, None]).astype(I32), axis=1), N_EXPERTS - 1)
    nused = (pends[-1] // MOE_BLK).astype(I32).reshape(1)
    dest = dest_row.reshape(2, n_all)
    dests = []
    off = 0
    for hf, _, _ in groups:
        dests.append(dest[:, off:off + hf.shape[0]])
        off += hf.shape[0]
    xs = _dispatch([g[0] for g in groups], dests, pends.astype(I32), nused, n_blk, DISPATCH_TILE)
    yr = _experts(xs, blk_exp.astype(I32), nused, l, P["w1"], P["w3"], P["w2"])
    return yr, dests


def _gather_begin(step, slot, nsteps, idx_hbm, src_hbm, idx_smem, buf, sem_idx, sem_rows, nrows, n_chunks):
    nslot = 1 - slot
    last = nsteps - 1
    nxt = jnp.minimum(step + 1, last)
    nxt2 = jnp.minimum(step + 2, last)
    per = nrows // n_chunks

    def idx_copy(k, sl):
        return pltpu.make_async_copy(idx_hbm.at[k], idx_smem.at[sl], sem_idx.at[sl])

    def row_copy(sl, r):
        return pltpu.make_async_copy(src_hbm.at[pl.ds(idx_smem[sl, r], 1)], buf.at[sl, pl.ds(r, 1)],
                                     sem_rows.at[sl])

    def rows_wait(sl):
        pltpu.make_async_copy(src_hbm.at[pl.ds(0, nrows)], buf.at[sl], sem_rows.at[sl]).wait()

    @pl.when(step == 0)
    def _():
        first = idx_copy(0, 0)
        first.start()
        first.wait()

        def body(g, carry):
            for u in range(GATHER_UNROLL):
                row_copy(0, g * GATHER_UNROLL + u).start()
            return carry
        lax.fori_loop(0, nrows // GATHER_UNROLL, body, 0)
        idx_copy(nxt, 1).start()

    idx_copy(nxt, nslot).wait()
    rows_wait(slot)
    idx_copy(nxt2, slot).start()

    def issue_chunk(c):
        for r in range(c * per, (c + 1) * per):
            row_copy(nslot, r).start(priority=r % 2)

    def finish():
        @pl.when(step == last)
        def _():
            rows_wait(nslot)
            idx_copy(nxt2, slot).wait()

    return issue_chunk, finish


def _combine(slot, x1_ref, gate_ref, ybuf, tm):
    g = gate_ref[...]
    y = g[:, 0:1] * ybuf[slot, 0:tm, :] + g[:, 1:2] * ybuf[slot, tm:2 * tm, :]
    return x1_ref[...] + y


def _mid_kernel(dest_hbm, yr_hbm, x1_ref, gate_ref, cos_ref, sin_ref,
                gkv_ref, wkvt_ref, gmix_ref, wqt_ref, wqm_ref,
                x2_ref, kt_ref, vt_ref, qt_ref, qm_ref,
                idx_smem, ybuf, sem_idx, sem_rows, *, tm, d_att):
    n_heads = d_att // HEAD_DIM
    n_chunks = 8
    for par in range(2):
        tok = slice(par * tm, (par + 1) * tm)
        issue_chunk, finish = _gather_begin(2 * pl.program_id(0) + par, par, 2 * pl.num_programs(0),
                                            dest_hbm, yr_hbm, idx_smem, ybuf, sem_idx, sem_rows,
                                            2 * tm, n_chunks)
        chunks = iter(range(n_chunks))
        x2 = _combine(par, x1_ref.at[tok], gate_ref.at[tok], ybuf, tm)
        x2_ref[tok, :] = x2
        issue_chunk(next(chunks))
        cos_t = cos_ref[:, tok]
        sin_t = sin_ref[:, tok]
        hk = _rms(x2, gkv_ref[...]).astype(BF16)
        issue_chunk(next(chunks))
        vt_ref[0, :, tok] = lax.dot_general(wkvt_ref[d_att:, :], hk, NT_DIMS, preferred_element_type=F32)
        issue_chunk(next(chunks))
        kt_ref[0, :, tok] = lax.dot_general(wkvt_ref[:d_att, :], hk, NT_DIMS, preferred_element_type=F32)
        issue_chunk(next(chunks))
        _rot_t(kt_ref, tok, cos_t, sin_t, n_heads)
        hq = _rms(x2, gmix_ref[...]).astype(BF16)
        issue_chunk(next(chunks))
        qt_ref[0, :, tok] = lax.dot_general(wqt_ref[...], hq, NT_DIMS, preferred_element_type=F32)
        issue_chunk(next(chunks))
        _rot_t(qt_ref, tok, cos_t, sin_t, n_heads)
        issue_chunk(next(chunks))
        qm_ref[tok, :] = jnp.dot(hq, wqm_ref[...], preferred_element_type=F32)
        for c in chunks:
            issue_chunk(c)
        finish()


def _mid(x1, gate, yr, dest, pos, n_seq, P, tm):
    n, d = x1.shape
    d_att = P["w_kv_t_bf"].shape[0] // 2
    tg = 2 * tm
    s_out = n // n_seq
    assert s_out % tg == 0
    nt = s_out // tg
    half = ROT_DIM // 2
    reps = s_out // pos.shape[0]
    inv = ROPE_THETA ** (-jnp.arange(half, dtype=F32) * (2.0 / ROT_DIM))
    ang_t = jnp.tile(inv[:, None] * pos.astype(F32)[None, :], (1, reps))
    const = lambda *shape: pl.BlockSpec(shape, lambda i: (0,) * len(shape))
    tok = lambda w: pl.BlockSpec((tg, w), lambda i: (i, 0))
    tbl_t = pl.BlockSpec((half, tg), lambda i: (0, i % nt))
    feat_t = pl.BlockSpec((1, d_att, tg), lambda i: (i // nt, 0, i % nt))
    feat_shape = jax.ShapeDtypeStruct((n_seq, d_att, s_out), F32)
    return pl.pallas_call(
        functools.partial(_mid_kernel, tm=tm, d_att=d_att),
        out_shape=(jax.ShapeDtypeStruct((n, d), F32), feat_shape, feat_shape, feat_shape,
                   jax.ShapeDtypeStruct((n, D_MEMQ), F32)),
        grid=(n // tg,),
        in_specs=[
            pl.BlockSpec(memory_space=pl.ANY), pl.BlockSpec(memory_space=pl.ANY),
            tok(d), tok(2), tbl_t, tbl_t,
            const(1, d), const(2 * d_att, d), const(1, d), const(d_att, d), const(d, D_MEMQ),
        ],
        out_specs=(tok(d), feat_t, feat_t, feat_t, tok(D_MEMQ)),
        scratch_shapes=[
            pltpu.SMEM((2, 2 * tm), I32),
            pltpu.VMEM((2, 2 * tm, d), F32),
            pltpu.SemaphoreType.DMA((2,)),
            pltpu.SemaphoreType.DMA((2,)),
        ],
        compiler_params=pltpu.CompilerParams(
            dimension_semantics=("arbitrary",), vmem_limit_bytes=VMEM_LIMIT),
        name="mid",
    )(_dest_tiles(dest, tm), yr, x1, gate, jnp.cos(ang_t), jnp.sin(ang_t),
      P["g_kv"].reshape(1, d), P["w_kv_t_bf"], P["g_mix"][1].reshape(1, d), P["w_q_t_bf"], P["w_qm_bf"])


def _final_kernel(dest_hbm, yr_hbm, x1_ref, gate_ref, g_ref, y_ref, idx_smem, ybuf, sem_idx, sem_rows, *, tm):
    for par in range(2):
        tok = slice(par * tm, (par + 1) * tm)
        issue_chunk, finish = _gather_begin(2 * pl.program_id(0) + par, par, 2 * pl.num_programs(0),
                                            dest_hbm, yr_hbm, idx_smem, ybuf, sem_idx, sem_rows, 2 * tm, 1)
        issue_chunk(0)
        y_ref[tok, :] = _rms(_combine(par, x1_ref.at[tok], gate_ref.at[tok], ybuf, tm), g_ref[...])
        finish()


def _final(x1, gate, yr, dest, g_final, tm):
    n, d = x1.shape
    assert n % (2 * tm) == 0
    tok = lambda w: pl.BlockSpec((2 * tm, w), lambda i: (i, 0))
    return pl.pallas_call(
        functools.partial(_final_kernel, tm=tm),
        out_shape=jax.ShapeDtypeStruct((n, d), F32),
        grid=(n // (2 * tm),),
        in_specs=[
            pl.BlockSpec(memory_space=pl.ANY), pl.BlockSpec(memory_space=pl.ANY),
            tok(d), tok(2), pl.BlockSpec((1, d), lambda i: (0, 0)),
        ],
        out_specs=tok(d),
        scratch_shapes=[
            pltpu.SMEM((2, 2 * tm), I32),
            pltpu.VMEM((2, 2 * tm, d), F32),
            pltpu.SemaphoreType.DMA((2,)),
            pltpu.SemaphoreType.DMA((2,)),
        ],
        compiler_params=pltpu.CompilerParams(
            dimension_semantics=("arbitrary",), vmem_limit_bytes=VMEM_LIMIT),
        name="final",
    )(_dest_tiles(dest, tm), yr, x1, gate, g_final.reshape(1, d))


def _moba_prompt_kernel(qt_ref, kt_ref, vt_ref, o_ref, kmean, krows, vones, sel_sc, s_buf, p_buf, acc,
                        *, s_len, n_grp):
    qb_i = pl.program_id(2)
    nb = s_len // MOBA_BLOCK
    n_h = LANE // HEAD_DIM
    tq = MOBA_BLOCK
    nq = n_h * tq
    scale = HEAD_DIM ** -0.5
    feat = lax.broadcasted_iota(I32, (LANE, MOBA_BLOCK), 0) // HEAD_DIM
    groups = range(n_grp)

    @pl.when(qb_i == 0)
    def _():
        kmean[...] = jnp.zeros(kmean.shape, F32)
        for g in groups:
            rows = slice(g * LANE, (g + 1) * LANE)
            for jb in range(nb):
                cols = slice(jb * MOBA_BLOCK, (jb + 1) * MOBA_BLOCK)
                kr = kt_ref[0, rows, cols].T
                kmean[g, jb:jb + 1, :] = jnp.sum(kr, axis=0, keepdims=True) * (1.0 / MOBA_BLOCK)
                krows[g, jb] = kr.astype(BF16)
                vblk = vt_ref[0, rows, cols]
                for hh in range(n_h):
                    vones[g, hh, jb] = jnp.where(feat == hh, vblk, 1.0).astype(BF16)

    key = lax.broadcasted_iota(I32, (MOBA_BLOCK, nq), 0)
    qcol = lax.broadcasted_iota(I32, (MOBA_BLOCK, nq), 1) % tq
    qs, m0, p_own = [], [], []
    for g in groups:
        qt = qt_ref[0, g * LANE:(g + 1) * LANE, :]
        qstack = jnp.concatenate([jnp.where(feat == hh, qt, 0.0) for hh in range(n_h)], axis=1)
        gate = jnp.dot(kmean[g], qstack, preferred_element_type=F32, precision=HIGHEST)
        brow = lax.broadcasted_iota(I32, gate.shape, 0)
        gate = jnp.where(brow < qb_i, gate, NEG_INF)
        sel = _top_k_mask(gate, brow, 0)
        for jb in range(nb):
            sel_sc[g, jb] = sel[jb:jb + 1, :]
        qs.append((qstack * scale).astype(BF16))
        s_own = jnp.where(key <= qcol, jnp.dot(krows[g, qb_i], qs[g], preferred_element_type=F32), NEG_INF)
        m0.append(jnp.max(s_own, axis=0, keepdims=True))
        p_own.append(jnp.exp(s_own - m0[g]).astype(BF16))

    def put_scores(g, j, slot):
        s_buf[g, slot] = jnp.dot(krows[g, j], qs[g], preferred_element_type=F32)

    def add_block(g, alpha, j, slot):
        for hh in range(n_h):
            acc[g, hh] = (alpha[:, hh * tq:(hh + 1) * tq] * acc[g, hh]
                          + jnp.dot(vones[g, hh, j], p_buf[g, slot, :, hh * tq:(hh + 1) * tq],
                                    preferred_element_type=F32))

    for g in groups:
        acc[g] = jnp.zeros((n_h, LANE, tq), F32)
        p_buf[g, 1] = p_own[g]
        put_scores(g, 0, 0)

    def half_trip(j, slot, carry):
        out = []
        for g in groups:
            m, alpha_prev = carry[g]
            add_block(g, alpha_prev, jnp.where(j == 0, qb_i, j - 1), 1 - slot)
            put_scores(g, jnp.minimum(j + 1, nb - 1), 1 - slot)
            chosen = sel_sc[g, j] > 0.0
            m_new = jnp.maximum(m, jnp.where(chosen, jnp.max(s_buf[g, slot], axis=0, keepdims=True), NEG_INF))
            p_buf[g, slot] = jnp.exp(s_buf[g, slot] - jnp.where(chosen, m_new, jnp.inf)).astype(BF16)
            out.append((m_new, jnp.exp(m - m_new)))
        return tuple(out)

    def body(t, carry):
        return half_trip(2 * t + 1, 1, half_trip(2 * t, 0, carry))

    n_trips = (qb_i + 1) // 2
    one = jnp.ones((1, nq), F32)
    state = lax.fori_loop(0, n_trips, body, tuple((m0[g], one) for g in groups))
    last = jnp.where(n_trips == 0, qb_i, 2 * n_trips - 1)
    for g in groups:
        add_block(g, state[g][1], last, 1)
        out_t = jnp.zeros((LANE, tq), F32)
        for hh in range(n_h):
            other = (1 - hh) * HEAD_DIM
            a_h = acc[g, hh]
            out_t = jnp.where(feat == hh, a_h / a_h[other:other + 1, :], out_t)
        o_ref[:, g * LANE:(g + 1) * LANE] = out_t.T


def _moba_prompt(qt, kt, vt, b, s, n_grp=2):
    d_att = qt.shape[1]
    nb = s // MOBA_BLOCK
    fw = n_grp * LANE
    assert s % MOBA_BLOCK == 0 and LANE // HEAD_DIM == 2 and d_att % fw == 0
    nb_pad = -(-nb // SUBLANE) * SUBLANE
    kv_spec = pl.BlockSpec((1, fw, s), lambda i, hp, qb: (i, hp, 0))
    return pl.pallas_call(
        functools.partial(_moba_prompt_kernel, s_len=s, n_grp=n_grp),
        out_shape=jax.ShapeDtypeStruct((b * s, d_att), F32),
        grid=(b, d_att // fw, nb),
        in_specs=[
            pl.BlockSpec((1, fw, MOBA_BLOCK), lambda i, hp, qb: (i, hp, qb)),
            kv_spec, kv_spec,
        ],
        out_specs=pl.BlockSpec((MOBA_BLOCK, fw), lambda i, hp, qb: (i * nb + qb, hp)),
        scratch_shapes=[
            pltpu.VMEM((n_grp, nb_pad, LANE), F32),
            pltpu.VMEM((n_grp, nb, MOBA_BLOCK, LANE), BF16),
            pltpu.VMEM((n_grp, 2, nb, LANE, MOBA_BLOCK), BF16),
            pltpu.VMEM((n_grp, nb, 1, 2 * MOBA_BLOCK), F32),
            pltpu.VMEM((n_grp, 2, MOBA_BLOCK, 2 * MOBA_BLOCK), F32),
            pltpu.VMEM((n_grp, 2, MOBA_BLOCK, 2 * MOBA_BLOCK), BF16),
            pltpu.VMEM((n_grp, 2, LANE, MOBA_BLOCK), F32),
        ],
        compiler_params=pltpu.CompilerParams(
            dimension_semantics=("arbitrary", "arbitrary", "arbitrary"), vmem_limit_bytes=VMEM_LIMIT),
        name="moba_prompt",
    )(qt, kt, vt)


def _head_diag(o, n_heads, ds):
    return jnp.concatenate(
        [o[h * ds:(h + 1) * ds, h * HEAD_DIM:(h + 1) * HEAD_DIM] for h in range(n_heads)], axis=0)


def _moba_sample_kernel(pt_ref, q_ref, kn_ref, vn_ref, *rest, ds, n_heads, nb, bps):
    pages = rest[:4 * bps]
    o_ref, qbd, g_all, m_all, l_all, o_all = rest[4 * bps:]
    jg = pl.program_id(1)
    rows = n_heads * ds
    d_att = n_heads * HEAD_DIM
    scale = HEAD_DIM ** -0.5
    blane = lax.broadcasted_iota(I32, (rows, LANE), 1)

    @pl.when(jg == 0)
    def _():
        q = q_ref[...]
        lane = lax.broadcasted_iota(I32, (ds, d_att), 1)
        for h in range(n_heads):
            qbd[h * ds:(h + 1) * ds, :] = jnp.where(lane // HEAD_DIM == h, q, 0.0)
        g_all[...] = jnp.full((rows, LANE), NEG_INF, F32)
        m_all[...] = jnp.zeros((rows, LANE), F32)
        l_all[...] = jnp.zeros((rows, LANE), F32)

    qb = qbd[...].astype(BF16)
    g_new, m_new, l_new = g_all[...], m_all[...], l_all[...]
    for bi in range(bps):
        k0_ref, k1_ref = pages[2 * bi], pages[2 * bi + 1]
        v0_ref, v1_ref = pages[2 * bps + 2 * bi], pages[2 * bps + 2 * bi + 1]
        j = jg * bps + bi
        kt = jnp.concatenate([k0_ref[0].astype(BF16), k1_ref[0].astype(BF16)], axis=1)
        vt = jnp.concatenate([v0_ref[0].astype(BF16), v1_ref[0].astype(BF16)], axis=1)
        s = jnp.dot(qb, kt, preferred_element_type=F32)
        g = jnp.sum(s, axis=-1, keepdims=True) * (1.0 / MOBA_BLOCK)
        m = jnp.max(s, axis=-1, keepdims=True) * scale
        p = jnp.exp(s * scale - m)
        l = jnp.sum(p, axis=-1, keepdims=True)
        o = lax.dot_general(p.astype(BF16), vt, NT_DIMS, preferred_element_type=F32)
        o_all[j] = _head_diag(o, n_heads, ds)
        here = blane == j
        g_new = jnp.where(here, g, g_new)
        m_new = jnp.where(here, m, m_new)
        l_new = jnp.where(here, l, l_new)
    g_all[...] = g_new
    m_all[...] = m_new
    l_all[...] = l_new

    @pl.when(jg == nb // bps - 1)
    def _():
        chosen = _top_k_mask(g_new, blane, -1) > 0.0
        s_own = lax.dot_general(qb, kn_ref[...].astype(BF16), NT_DIMS, preferred_element_type=F32) * scale
        r_i = lax.broadcasted_iota(I32, (rows, ds), 0) % ds
        c_i = lax.broadcasted_iota(I32, (rows, ds), 1)
        s_own = jnp.where(c_i <= r_i, s_own, NEG_INF)
        m_sel = jnp.max(jnp.where(chosen, m_new, NEG_INF), axis=-1, keepdims=True)
        m_tot = jnp.maximum(m_sel, jnp.max(s_own, axis=-1, keepdims=True))
        wgt = jnp.where(chosen, jnp.exp(m_new - m_tot), 0.0)
        p_own = jnp.exp(s_own - m_tot)
        l_tot = jnp.sum(wgt * l_new, axis=-1, keepdims=True) + jnp.sum(p_own, axis=-1, keepdims=True)
        acc = _head_diag(jnp.dot(p_own.astype(BF16), vn_ref[...].astype(BF16), preferred_element_type=F32),
                         n_heads, ds)
        for jj in range(nb):
            acc = acc + wgt[:, jj:jj + 1] * o_all[jj]
        o_ref[0] = acc / l_tot


def _moba_sample(qb, k_new, v_new, cache_k, cache_v, page_table, db, ds, bps=4):
    n_pool, page, n_heads, hd = cache_k.shape
    d_att = n_heads * hd
    n_pages = page_table.shape[1]
    ppb = MOBA_BLOCK // page
    nb = n_pages // ppb
    rows = n_heads * ds
    assert ppb == 2 and n_pages % ppb == 0 and ds <= MOBA_BLOCK and nb <= LANE and nb % bps == 0
    ck = cache_k.transpose(0, 2, 3, 1).reshape(n_pool, d_att, page)
    cv = cache_v.transpose(0, 2, 3, 1).reshape(n_pool, d_att, page)
    seq = pl.BlockSpec((ds, d_att), lambda i, j, pt: (i, 0))
    ppg = ppb * bps
    pg = lambda off: pl.BlockSpec((1, d_att, page), lambda i, j, pt: (pt[i * n_pages + ppg * j + off], 0, 0))
    out = pl.pallas_call(
        functools.partial(_moba_sample_kernel, ds=ds, n_heads=n_heads, nb=nb, bps=bps),
        out_shape=jax.ShapeDtypeStruct((db, rows, hd), F32),
        grid_spec=pltpu.PrefetchScalarGridSpec(
            num_scalar_prefetch=1,
            grid=(db, nb // bps),
            in_specs=[seq, seq, seq] + [pg(o) for o in range(ppg)] * 2,
            out_specs=pl.BlockSpec((1, rows, hd), lambda i, j, pt: (i, 0, 0)),
            scratch_shapes=[
                pltpu.VMEM((rows, d_att), F32),
                pltpu.VMEM((rows, LANE), F32),
                pltpu.VMEM((rows, LANE), F32),
                pltpu.VMEM((rows, LANE), F32),
                pltpu.VMEM((nb, rows, hd), F32),
            ],
        ),
        compiler_params=pltpu.CompilerParams(
            dimension_semantics=("arbitrary", "arbitrary"), vmem_limit_bytes=VMEM_LIMIT),
        name="moba_sample",
    )(page_table.reshape(-1), qb, k_new, v_new, *([ck] * ppg), *([cv] * ppg))
    return out.reshape(db, n_heads, ds, hd).transpose(0, 2, 1, 3).reshape(db * ds, d_att)


def kernel(x_prompt, x_sample, state_conv, cache_k, cache_v, cache_mem_k, cache_mem_v, page_table,
           mem_prompt, g_mix, g_ffn, g_final, g_mem, w_mem_kv, w_in_a, conv_w, conv_b, cln_g, cln_b,
           w_out_a, g_kv, w_kv, w_in_b, w_out_b, w_rg, b_rg, w_re, b_re, w1, w3, w2):
    depth = g_mix.shape[0]
    n_heads = w_kv.shape[1] // (2 * HEAD_DIM)
    d_att = n_heads * HEAD_DIM
    routers = [_router_weights(w_rg[l], b_rg[l], w_re[l], b_re[l]) for l in range(depth)]
    P = {
        "g_mix": g_mix, "g_ffn": g_ffn, "g_final": g_final, "g_kv": g_kv,
        "w_in_a_bf": w_in_a[0].astype(BF16), "w_out_a_bf": w_out_a[0].astype(BF16),
        "w_kv_t_bf": w_kv.T.astype(BF16),
        "w_q_t_bf": w_in_b[0][:, :d_att].T.astype(BF16), "w_qm_bf": w_in_b[0][:, d_att:].astype(BF16),
        "w_out_b_bf": w_out_b[0].astype(BF16),
        "conv_w": conv_w, "conv_b": conv_b, "cln_g": cln_g, "cln_b": cln_b,
        "wr": [r[0] for r in routers], "br": [r[1] for r in routers],
        "w1": w1, "w3": w3, "w2": w2,
    }
    b_p, s_p, d = x_prompt.shape
    db, ds, _ = x_sample.shape
    n_p, n_s = b_p * s_p, db * ds
    mem_len = mem_prompt.shape[1]
    past = page_table.shape[1] * cache_k.shape[1]
    assert past % MOBA_BLOCK == 0
    ts_p, tm_p = 256, 256
    ts_s, tm_s = ds, n_s // 2
    nb_s = math.gcd(db, 8)

    memkv_t = _memkv(mem_prompt, g_mem, w_mem_kv.transpose(0, 2, 1).astype(BF16))
    kv6 = memkv_t.reshape(depth, b_p, 2, MEM_H, HEAD_DIM, mem_len).transpose(2, 0, 1, 5, 3, 4)
    mem_k_p, mem_v_p = kv6[0], kv6[1]
    cmk = cache_mem_k.transpose(0, 1, 3, 4, 2).reshape(depth, db, D_MEMQ, mem_len)
    cmv = cache_mem_v.transpose(0, 1, 3, 4, 2).reshape(depth, db, D_MEMQ, mem_len)
    mem_p = (memkv_t, memkv_t, (0, 1))
    mem_s = (cmk, cmv, (0, 0))

    ctx_p = jnp.zeros((b_p, CONV_W - 1, state_conv.shape[-1]), F32)
    x1_p, hf_p, gate_p, eid_p, cnt_p, conv_p = _layer0(x_prompt.reshape(n_p, d), b_p, s_p, ctx_p, *mem_p, 0, P, ts_p)
    x1_s, hf_s, gate_s, eid_s, cnt_s, conv_s = _layer0(x_sample.reshape(n_s, d), db, ds, state_conv[0], *mem_s, 0, P, ts_s, nb_s)
    yr, (dest_p, dest_s) = _moe_layer(0, [(hf_p, eid_p, cnt_p), (hf_s, eid_s.T, cnt_s)], P)

    x2_p, kt_p, vt_p, qt_p, qm_p = _mid(x1_p, gate_p, yr, dest_p, jnp.arange(s_p), b_p, P, tm_p)
    x2_s, kt_s, vt_s, qt_s, qm_s = _mid(x1_s, gate_s, yr, dest_s, past + jnp.arange(ds), 1, P, tm_s)
    c1_p = _moba_prompt(qt_p, kt_p, vt_p, b_p, s_p)
    k_new, v_new = kt_s[0].T, vt_s[0].T
    c1_s = _moba_sample(qt_s[0].T, k_new, v_new, cache_k, cache_v, page_table, db, ds)
    x3_p, hf_p, gate_p, eid_p, cnt_p = _layer1(x2_p, c1_p, qm_p, b_p, s_p, *mem_p, 1, P, ts_p)
    x3_s, hf_s, gate_s, eid_s, cnt_s = _layer1(x2_s, c1_s, qm_s, db, ds, *mem_s, 1, P, ts_s, nb_s)
    yr, (dest_p, dest_s) = _moe_layer(1, [(hf_p, eid_p, cnt_p), (hf_s, eid_s.T, cnt_s)], P)
    y_p = _final(x3_p, gate_p, yr, dest_p, g_final, tm_p).reshape(b_p, s_p, d)
    y_s = _final(x3_s, gate_s, yr, dest_s, g_final, tm_s).reshape(db, ds, d)

    k_p = kt_p.reshape(b_p, n_heads, HEAD_DIM, s_p).transpose(0, 3, 1, 2)
    v_p = vt_p.reshape(b_p, n_heads, HEAD_DIM, s_p).transpose(0, 3, 1, 2)
    k_s = k_new.reshape(db, ds, n_heads, HEAD_DIM)
    v_s = v_new.reshape(db, ds, n_heads, HEAD_DIM)
    return (y_p, y_s, conv_p[None], conv_s[None], k_p, v_p, k_s, v_s, mem_k_p, mem_v_p)
```

```python
import functools
import math

import jax
import jax.numpy as jnp
from jax import lax
from jax.experimental import pallas as pl
from jax.experimental.pallas import tpu as pltpu

F32 = jnp.float32
BF16 = jnp.bfloat16
I32 = jnp.int32
U32 = jnp.uint32
HIGHEST = lax.Precision.HIGHEST

EPS = 1e-6
HEAD_DIM = 64
MEM_H = 4
D_MEMQ = MEM_H * HEAD_DIM
CONV_W = 31
CTX_ROWS = 32
CTX_PAD = CTX_ROWS - (CONV_W - 1)
MOBA_BLOCK = 256
MOBA_TOPK = 3
ROT_DIM = HEAD_DIM // 4
ROPE_THETA = 500000.0
N_GROUPS = 4
E_PER_GROUP = 16
N_EXPERTS = N_GROUPS * E_PER_GROUP
MOE_BLK = 256
PLAN_TILE_MAX = 512
GATHER_UNROLL = 8
DISPATCH_TILE = 128
ROUTER_LANES = 128
LANE = 128
SUBLANE = 8
VMEM_LIMIT = 56 * 1024 * 1024

NEG_INF = float("-inf")
BIG_IDX = 1 << 20
NT_DIMS = (((1,), (1,)), ((), ()))


def _rms(x, g):
    return x * lax.rsqrt(jnp.mean(x * x, axis=-1, keepdims=True) + EPS) * g


def _pack_bf16_pairs(x):
    half = x.shape[1] // 2
    hi = lax.bitcast_convert_type(x[:, :half].astype(BF16).astype(F32), U32)
    lo = lax.bitcast_convert_type(x[:, half:].astype(BF16).astype(F32), U32)
    return hi | (lo >> 16)


def _unpack_bf16_pairs(u):
    hi = lax.bitcast_convert_type(u & jnp.uint32(0xFFFF0000), F32)
    lo = lax.bitcast_convert_type(u << 16, F32)
    return jnp.concatenate([hi, lo], axis=1).astype(BF16)


def _first_argmax(vals, idx, axis):
    m = jnp.max(vals, axis=axis, keepdims=True)
    first = jnp.min(jnp.where(vals == m, idx, BIG_IDX), axis=axis, keepdims=True)
    return m, first


def _top_k_mask(gate, idx, axis):
    sel = jnp.zeros(gate.shape, F32)
    for _ in range(MOBA_TOPK):
        m, first = _first_argmax(gate, idx, axis)
        pick = jnp.logical_and(idx == first, m > NEG_INF)
        sel = jnp.where(pick, 1.0, sel)
        gate = jnp.where(pick, NEG_INF, gate)
    return sel


def _rot_t(xt_ref, tok, cos_t, sin_t, n_heads):
    half = ROT_DIM // 2
    for hh in range(n_heads):
        r0 = hh * HEAD_DIM
        xa = xt_ref[0, r0:r0 + half, tok]
        xb = xt_ref[0, r0 + half:r0 + ROT_DIM, tok]
        xt_ref[0, r0:r0 + half, tok] = xa * cos_t - xb * sin_t
        xt_ref[0, r0 + half:r0 + ROT_DIM, tok] = xb * cos_t + xa * sin_t


def _memkv_kernel(mem_ref, g_ref, wt_ref, out_ref):
    h = _rms(mem_ref[0], g_ref[0]).astype(BF16)
    out_ref[0, 0] = lax.dot_general(wt_ref[0], h, NT_DIMS, preferred_element_type=F32)


def _memkv(mem, g_mem, w_mem_kv_t_bf):
    b, m, d = mem.shape
    depth = g_mem.shape[0]
    n_out = w_mem_kv_t_bf.shape[1]
    return pl.pallas_call(
        _memkv_kernel,
        out_shape=jax.ShapeDtypeStruct((depth, b, n_out, m), F32),
        grid=(depth, b),
        in_specs=[
            pl.BlockSpec((1, m, d), lambda l, i: (i, 0, 0)),
            pl.BlockSpec((1, 1, d), lambda l, i: (l, 0, 0)),
            pl.BlockSpec((1, n_out, d), lambda l, i: (l, 0, 0)),
        ],
        out_specs=pl.BlockSpec((1, 1, n_out, m), lambda l, i: (l, i, 0, 0)),
        compiler_params=pltpu.CompilerParams(dimension_semantics=("arbitrary", "arbitrary")),
        name="memkv",
    )(mem, g_mem.reshape(depth, 1, d), w_mem_kv_t_bf)


def _mem_attn_into(mix, off, qm_all, mkt_ref, mvt_ref, nb):
    t = qm_all.shape[0] // nb
    scale = HEAD_DIM ** -0.5
    lane = lax.broadcasted_iota(I32, (t, D_MEMQ), 1)
    for bb in range(nb):
        qm = qm_all[bb * t:(bb + 1) * t]
        mkt = mkt_ref[0, bb].astype(BF16)
        mvt = mvt_ref[0, bb].astype(BF16)
        om = jnp.zeros(qm.shape, F32)
        for hh in range(MEM_H):
            in_head = (lane // HEAD_DIM) == hh
            q = jnp.where(in_head, qm, 0.0).astype(BF16)
            s = jnp.dot(q, mkt, preferred_element_type=F32) * scale
            m = jnp.max(s, axis=-1, keepdims=True)
            e = jnp.exp(s - m)
            l = jnp.sum(e, axis=-1, keepdims=True)
            o = lax.dot_general(e.astype(BF16), mvt, NT_DIMS, preferred_element_type=F32) / l
            om = jnp.where(in_head, o, om)
        mix[bb * t:(bb + 1) * t, off:off + D_MEMQ] = om.astype(BF16)


def _out_and_route(x, mix, wout_ref, gffn_ref, wr_ref, br_ref, x1_ref, hf_ref, gate_ref, eid_ref, cnt_ref, first):
    x1 = x + jnp.dot(mix[...], wout_ref[...], preferred_element_type=F32)
    x1_ref[...] = x1
    hf = _rms(x1, gffn_ref[...])
    hf_ref[...] = _pack_bf16_pairs(hf)
    hf_hi = hf.astype(BF16)
    hf_lo = (hf - hf_hi.astype(F32)).astype(BF16)
    logits = jnp.dot(jnp.concatenate([hf_hi, hf_lo, hf_hi], axis=1), wr_ref[...],
                     preferred_element_type=F32) + br_ref[...]
    lane = lax.broadcasted_iota(I32, logits.shape, 1)
    gl = jnp.where((lane >= N_EXPERTS) & (lane < N_EXPERTS + N_GROUPS), logits, NEG_INF)
    gmax, glane = _first_argmax(gl, lane, -1)
    p_group = 1.0 / jnp.sum(jnp.exp(gl - gmax), axis=-1, keepdims=True)
    e_lo = (glane - N_EXPERTS) * E_PER_GROUP
    el = jnp.where((lane >= e_lo) & (lane < e_lo + E_PER_GROUP), logits, NEG_INF)
    m1, i1 = _first_argmax(el, lane, -1)
    el2 = jnp.where(lane == i1, NEG_INF, el)
    m2, i2 = _first_argmax(el2, lane, -1)
    e2 = jnp.exp(m2 - m1)
    g1 = p_group / (1.0 + e2)
    g2 = g1 * e2
    picked = jnp.where(lane == i1, 1.0, 0.0) + jnp.where(lane == i2, 1.0, 0.0)
    tile_counts = jnp.sum(picked, axis=0, keepdims=True)

    @pl.when(first)
    def _():
        cnt_ref[...] = tile_counts

    @pl.when(jnp.logical_not(first))
    def _():
        cnt_ref[...] = cnt_ref[...] + tile_counts

    lane2 = lax.broadcasted_iota(I32, gate_ref.shape, 1)
    gate_ref[...] = jnp.where(lane2 == 0, g1, g2)
    if eid_ref.shape[1] == 2:
        eid_ref[...] = jnp.where(lane2 == 0, i1, i2)
    else:
        t = x.shape[0]
        diag = lax.broadcasted_iota(I32, (t, t), 0) == lax.broadcasted_iota(I32, (t, t), 1)
        rows = [jnp.sum(jnp.where(diag, col.astype(F32), 0.0), axis=0, keepdims=True) for col in (i1, i2)]
        eid_ref[...] = jnp.concatenate(rows, axis=0).astype(I32)


def _eid_out(n, ts, nt):
    if ts % LANE == 0:
        return jax.ShapeDtypeStruct((2, n), I32), pl.BlockSpec((2, ts), lambda i, t: (0, i * nt + t))
    return jax.ShapeDtypeStruct((n, 2), I32), pl.BlockSpec((ts, 2), lambda i, t: (i * nt + t, 0))


def _router_weights(w_rg, b_rg, w_re, b_re):
    d = w_rg.shape[0]
    pad = ROUTER_LANES - N_EXPERTS - N_GROUPS
    wr = jnp.concatenate([w_re, w_rg, jnp.zeros((d, pad), F32)], axis=1)
    br = jnp.concatenate([b_re, b_rg, jnp.zeros((pad,), F32)]).reshape(1, ROUTER_LANES)
    w_hi = wr.astype(BF16)
    w_lo = (wr - w_hi.astype(F32)).astype(BF16)
    return jnp.concatenate([w_hi, w_hi, w_lo], axis=0), br


def _layer0_kernel(x_ref, ctx_ref, mk_ref, mv_ref, gmix_ref, win_ref, cw_ref, cb_ref, lg_ref, lb_ref,
                   wout_ref, gffn_ref, wr_ref, br_ref,
                   x1_ref, hf_ref, gate_ref, eid_ref, cnt_ref, nc_ref,
                   zext, zsh, cbuf, mix, *, ts, rc, d_conv, nb):
    t = pl.program_id(1)
    nt = pl.num_programs(1)

    x = x_ref[...]
    h = _rms(x, gmix_ref[...]).astype(BF16)
    u = jnp.dot(h, win_ref[...], preferred_element_type=F32)
    z = u[:, :d_conv] * jax.nn.sigmoid(u[:, d_conv:2 * d_conv])
    qm = u[:, 2 * d_conv:]

    for bb in range(nb):
        rows = slice(bb * ts, (bb + 1) * ts)

        @pl.when(t == 0)
        def _():
            zext[bb, 0:CTX_PAD, :] = jnp.zeros((CTX_PAD, d_conv), F32)
            zext[bb, CTX_PAD:CTX_ROWS, :] = ctx_ref[bb]

        @pl.when(t > 0)
        def _():
            zext[bb, 0:CTX_ROWS, :] = zext[bb, ts:ts + CTX_ROWS, :]

        zext[bb, CTX_ROWS:CTX_ROWS + ts, :] = z[rows]

        @pl.when(t == nt - 1)
        def _():
            nc_ref[bb] = zext[bb, ts + CTX_PAD: ts + CTX_ROWS, :]

        for sh in range(1, SUBLANE):
            zsh[bb, sh - 1] = zext[bb, sh:sh + ts + CTX_ROWS - SUBLANE, :]

        def chunk(ci, carry):
            r0 = pl.multiple_of(ci * rc, SUBLANE)
            for cbk in range(d_conv // LANE):
                cs = slice(cbk * LANE, (cbk + 1) * LANE)
                acc = jnp.zeros((rc, LANE), F32) + cb_ref[:, cs]
                for w in range(CONV_W):
                    hi, sh = divmod(w + CTX_PAD, SUBLANE)
                    if sh == 0:
                        win = zext[bb, pl.ds(r0 + hi * SUBLANE, rc), cs]
                    else:
                        win = zsh[bb, sh - 1, pl.ds(r0 + hi * SUBLANE, rc), cs]
                    acc = acc + win * cw_ref[w:w + 1, cs]
                cbuf[pl.ds(bb * ts + r0, rc), cs] = acc
            return carry

        lax.fori_loop(0, ts // rc, chunk, 0)

    c = cbuf[...]
    mu = jnp.mean(c, axis=-1, keepdims=True)
    xc = c - mu
    y = xc * lax.rsqrt(jnp.mean(xc * xc, axis=-1, keepdims=True) + EPS) * lg_ref[...] + lb_ref[...]
    mix[:, :d_conv] = (y * jax.nn.sigmoid(y)).astype(BF16)
    _mem_attn_into(mix, d_conv, qm, mk_ref, mv_ref, nb)
    first = jnp.logical_and(pl.program_id(0) == 0, t == 0)
    _out_and_route(x, mix, wout_ref, gffn_ref, wr_ref, br_ref, x1_ref, hf_ref, gate_ref, eid_ref, cnt_ref, first)


def _layer0(x2d, b, s, ctx, mk_arr, mv_arr, kv_rows, l, P, ts, nb=1):
    n, d = x2d.shape
    nt = s // ts
    assert b % nb == 0 and (nb == 1 or nt == 1)
    tr = nb * ts
    d_conv = ctx.shape[-1]
    rc = min(32, ts)
    d_in = P["w_in_a_bf"].shape[-1]
    krow, vrow = kv_rows
    const = lambda *shape: pl.BlockSpec(shape, lambda i, t: (0,) * len(shape))
    tok = lambda w: pl.BlockSpec((tr, w), lambda i, t: (i * nt + t, 0))
    eid_shape, eid_spec = _eid_out(n, tr, nt)
    kern = functools.partial(_layer0_kernel, ts=ts, rc=rc, d_conv=d_conv, nb=nb)
    return pl.pallas_call(
        kern,
        out_shape=(
            jax.ShapeDtypeStruct((n, d), F32),
            jax.ShapeDtypeStruct((n, d // 2), U32),
            jax.ShapeDtypeStruct((n, 2), F32),
            eid_shape,
            jax.ShapeDtypeStruct((1, LANE), F32),
            jax.ShapeDtypeStruct((b, CONV_W - 1, d_conv), F32),
        ),
        grid=(b // nb, nt),
        in_specs=[
            tok(d),
            pl.BlockSpec((nb, CONV_W - 1, d_conv), lambda i, t: (i, 0, 0)),
            pl.BlockSpec((1, nb, D_MEMQ, mk_arr.shape[3]), lambda i, t: (l, i, krow, 0)),
            pl.BlockSpec((1, nb, D_MEMQ, mv_arr.shape[3]), lambda i, t: (l, i, vrow, 0)),
            const(1, d), const(d, d_in), const(CONV_W, d_conv), const(1, d_conv), const(1, d_conv),
            const(1, d_conv), const(d_conv + D_MEMQ, d), const(1, d), const(3 * d, ROUTER_LANES),
            const(1, ROUTER_LANES),
        ],
        out_specs=(tok(d), tok(d // 2), tok(2), eid_spec, const(1, LANE),
                   pl.BlockSpec((nb, CONV_W - 1, d_conv), lambda i, t: (i, 0, 0))),
        scratch_shapes=[
            pltpu.VMEM((nb, ts + CTX_ROWS, d_conv), F32),
            pltpu.VMEM((nb, SUBLANE - 1, ts + CTX_ROWS - SUBLANE, d_conv), F32),
            pltpu.VMEM((tr, d_conv), F32),
            pltpu.VMEM((tr, d_conv + D_MEMQ), BF16),
        ],
        compiler_params=pltpu.CompilerParams(
            dimension_semantics=("arbitrary", "arbitrary"), vmem_limit_bytes=VMEM_LIMIT),
        name="layer0",
    )(x2d, ctx, mk_arr, mv_arr, P["g_mix"][l].reshape(1, d), P["w_in_a_bf"], P["conv_w"][0],
      P["conv_b"][0].reshape(1, d_conv), P["cln_g"][0].reshape(1, d_conv), P["cln_b"][0].reshape(1, d_conv),
      P["w_out_a_bf"], P["g_ffn"][l].reshape(1, d), P["wr"][l], P["br"][l])


def _layer1_kernel(x_ref, c_ref, qm_ref, mk_ref, mv_ref, wout_ref, gffn_ref, wr_ref, br_ref,
                   x1_ref, hf_ref, gate_ref, eid_ref, cnt_ref, mix, *, d_att, nb):
    mix[:, :d_att] = c_ref[...].astype(BF16)
    _mem_attn_into(mix, d_att, qm_ref[...], mk_ref, mv_ref, nb)
    first = jnp.logical_and(pl.program_id(0) == 0, pl.program_id(1) == 0)
    _out_and_route(x_ref[...], mix, wout_ref, gffn_ref, wr_ref, br_ref, x1_ref, hf_ref, gate_ref, eid_ref,
                   cnt_ref, first)


def _layer1(x2d, c2d, qm2d, b, s, mk_arr, mv_arr, kv_rows, l, P, ts, nb=1):
    n, d = x2d.shape
    nt = s // ts
    assert b % nb == 0 and (nb == 1 or nt == 1)
    tr = nb * ts
    d_att = c2d.shape[-1]
    krow, vrow = kv_rows
    const = lambda *shape: pl.BlockSpec(shape, lambda i, t: (0,) * len(shape))
    tok = lambda w: pl.BlockSpec((tr, w), lambda i, t: (i * nt + t, 0))
    eid_shape, eid_spec = _eid_out(n, tr, nt)
    return pl.pallas_call(
        functools.partial(_layer1_kernel, d_att=d_att, nb=nb),
        out_shape=(
            jax.ShapeDtypeStruct((n, d), F32),
            jax.ShapeDtypeStruct((n, d // 2), U32),
            jax.ShapeDtypeStruct((n, 2), F32),
            eid_shape,
            jax.ShapeDtypeStruct((1, LANE), F32),
        ),
        grid=(b // nb, nt),
        in_specs=[
            tok(d), tok(d_att), tok(D_MEMQ),
            pl.BlockSpec((1, nb, D_MEMQ, mk_arr.shape[3]), lambda i, t: (l, i, krow, 0)),
            pl.BlockSpec((1, nb, D_MEMQ, mv_arr.shape[3]), lambda i, t: (l, i, vrow, 0)),
            const(d_att + D_MEMQ, d), const(1, d), const(3 * d, ROUTER_LANES), const(1, ROUTER_LANES),
        ],
        out_specs=(tok(d), tok(d // 2), tok(2), eid_spec, const(1, LANE)),
        scratch_shapes=[pltpu.VMEM((tr, d_att + D_MEMQ), BF16)],
        compiler_params=pltpu.CompilerParams(
            dimension_semantics=("arbitrary", "arbitrary"), vmem_limit_bytes=VMEM_LIMIT),
        name="layer1",
    )(x2d, c2d, qm2d, mk_arr, mv_arr, P["w_out_b_bf"], P["g_ffn"][l].reshape(1, d), P["wr"][l], P["br"][l])


def _plan_kernel(eid_ref, counts_ref, dest_ref, base, pstart, *, ta):
    t = pl.program_id(0)
    expert = lax.broadcasted_iota(I32, (LANE, ta), 0)
    onehot = eid_ref[...] == expert
    ohf = jnp.where(onehot, 1.0, 0.0)

    @pl.when(t == 0)
    def _():
        counts = counts_ref[...]
        pc = (((counts.astype(I32) + (MOE_BLK - 1)) // MOE_BLK) * MOE_BLK).astype(F32)
        r = lax.broadcasted_iota(I32, (LANE, LANE), 0)
        c = lax.broadcasted_iota(I32, (LANE, LANE), 1)
        below = jnp.where(c < r, 1.0, 0.0)
        pstart[...] = jnp.dot(below, pc, preferred_element_type=F32, precision=HIGHEST)
        base[...] = jnp.zeros(base.shape, F32)

    r = lax.broadcasted_iota(I32, (ta, ta), 0)
    c = lax.broadcasted_iota(I32, (ta, ta), 1)
    upto = jnp.where(r <= c, 1.0, 0.0).astype(BF16)
    prefix = jnp.dot(ohf.astype(BF16), upto, preferred_element_type=F32)
    slot = pstart[:, 0:1] + base[:, 0:1] + prefix - 1.0
    dest_ref[...] = jnp.sum(jnp.where(onehot, slot, 0.0), axis=0, keepdims=True).astype(I32)
    base[...] = base[...] + jnp.sum(ohf, axis=1, keepdims=True)


def _plan(eid_row, counts):
    a = eid_row.shape[1]
    ta = max(t for t in range(LANE, PLAN_TILE_MAX + 1, LANE) if a % t == 0)
    return pl.pallas_call(
        functools.partial(_plan_kernel, ta=ta),
        out_shape=jax.ShapeDtypeStruct((1, a), I32),
        grid=(a // ta,),
        in_specs=[pl.BlockSpec((1, ta), lambda t: (0, t)), pl.BlockSpec((LANE, LANE), lambda t: (0, 0))],
        out_specs=pl.BlockSpec((1, ta), lambda t: (0, t)),
        scratch_shapes=[pltpu.VMEM((LANE, LANE), F32)] * 2,
        compiler_params=pltpu.CompilerParams(dimension_semantics=("arbitrary",)),
        name="moe_plan",
    )(eid_row, jnp.broadcast_to(counts[:, None], (LANE, LANE)))


def _dest_tiles(dest_t, tm):
    n = dest_t.shape[1]
    return dest_t.reshape(2, n // tm, tm).transpose(1, 0, 2).reshape(n // tm, 2 * tm)


def _dispatch_kernel(pend_ref, nused_ref, dest_hbm, *rest, tm, starts, n_blk):
    n_src = len(starts) - 1
    hf_refs = rest[:n_src]
    xs_out, idx_smem, sem_idx, sem_rows, zbuf, sem_zero = rest[n_src:]
    i = pl.program_id(0)
    n_tiles = 2 * pl.num_programs(0)

    def idx_copy(k, sl):
        return pltpu.make_async_copy(dest_hbm.at[k], idx_smem.at[sl], sem_idx.at[sl])

    @pl.when(i == 0)
    def _():
        idx_copy(0, 0).start()
        zbuf[...] = jnp.zeros(zbuf.shape, zbuf.dtype)

        def zero_block(row0):
            return pltpu.make_async_copy(zbuf, xs_out.at[pl.ds(pl.multiple_of(row0, MOE_BLK), MOE_BLK)], sem_zero)

        def nonempty(e):
            return pend_ref[e] > jnp.where(e == 0, 0, pend_ref[jnp.maximum(e - 1, 0)])

        def visit(fn):
            def expert(e, carry):
                @pl.when(nonempty(e))
                def _():
                    fn(zero_block(pend_ref[e] - MOE_BLK))
                return carry

            def unused(b, carry):
                fn(zero_block(b * MOE_BLK))
                return carry

            lax.fori_loop(0, N_EXPERTS, expert, 0)
            lax.fori_loop(nused_ref[0], n_blk, unused, 0)

        visit(lambda cp: cp.start())
        visit(lambda cp: cp.wait())

    for g in range(n_src):
        hf_ref = hf_refs[g]

        @pl.when(jnp.logical_and(i >= starts[g], i < starts[g + 1]))
        def _():
            for par in range(2):
                t = 2 * i + par
                idx_copy(t, par).wait()

                @pl.when(t + 1 < n_tiles)
                def _():
                    idx_copy(t + 1, 1 - par).start()

                for r in range(tm):
                    for k in range(2):
                        row = idx_smem[par, k * tm + r]
                        pltpu.make_async_copy(hf_ref.at[pl.ds(par * tm + r, 1)], xs_out.at[pl.ds(row, 1)],
                                              sem_rows).start(priority=k)
            for _ in range(2):
                pltpu.make_async_copy(hf_ref, xs_out.at[pl.ds(0, 2 * tm)], sem_rows).wait()


def _dispatch(hfs, dests, pends, nused, n_blk, tm):
    w = hfs[0].shape[1]
    assert all(hf.shape[0] % (2 * tm) == 0 for hf in hfs)
    tiles = [hf.shape[0] // (2 * tm) for hf in hfs]
    starts = [sum(tiles[:g]) for g in range(len(hfs) + 1)]

    def src_spec(g):
        return pl.BlockSpec((2 * tm, w), lambda i, pe, nu: (jnp.clip(i - starts[g], 0, tiles[g] - 1), 0))

    return pl.pallas_call(
        functools.partial(_dispatch_kernel, tm=tm, starts=tuple(starts), n_blk=n_blk),
        out_shape=jax.ShapeDtypeStruct((n_blk * MOE_BLK, w), hfs[0].dtype),
        grid_spec=pltpu.PrefetchScalarGridSpec(
            num_scalar_prefetch=2,
            grid=(starts[-1],),
            in_specs=[pl.BlockSpec(memory_space=pl.ANY)] + [src_spec(g) for g in range(len(hfs))],
            out_specs=pl.BlockSpec(memory_space=pl.ANY),
            scratch_shapes=[
                pltpu.SMEM((2, 2 * tm), I32),
                pltpu.SemaphoreType.DMA((2,)),
                pltpu.SemaphoreType.DMA,
                pltpu.VMEM((MOE_BLK, w), hfs[0].dtype),
                pltpu.SemaphoreType.DMA,
            ],
        ),
        compiler_params=pltpu.CompilerParams(
            dimension_semantics=("arbitrary",), vmem_limit_bytes=VMEM_LIMIT),
        name="moe_dispatch",
    )(pends, nused, jnp.concatenate([_dest_tiles(d, tm) for d in dests], axis=0), *hfs)


def _experts_kernel(bexp_ref, nused_ref, xs_ref, w1_ref, w3_ref, w2_ref, out_ref, w13, w2s, *, d_exp):
    i = pl.program_id(0)
    nused = nused_ref[0]

    @pl.when(i < nused)
    def _():
        changed = jnp.logical_or(i == 0, bexp_ref[i] != bexp_ref[jnp.maximum(i - 1, 0)])

        @pl.when(changed)
        def _():
            w13[:, :d_exp] = w1_ref[0, 0].astype(BF16)
            w13[:, d_exp:] = w3_ref[0, 0].astype(BF16)
            w2s[...] = w2_ref[0, 0].astype(BF16)

        hcat = jnp.dot(_unpack_bf16_pairs(xs_ref[...]), w13[...], preferred_element_type=F32)
        a = hcat[:, :d_exp]
        act = (a * jax.nn.sigmoid(a)) * hcat[:, d_exp:]
        out_ref[...] = jnp.dot(act.astype(BF16), w2s[...], preferred_element_type=F32)

    @pl.when(i >= nused)
    def _():
        out_ref[...] = jnp.zeros(out_ref.shape, F32)


def _experts(xs, blk_exp, nused, l, w1, w3, w2):
    p_rows, dw = xs.shape
    d, d_exp = w1.shape[-2:]
    assert dw * 2 == d
    n_blk = p_rows // MOE_BLK
    wspec = lambda r, c: pl.BlockSpec((1, 1, r, c), lambda i, be, nu: (l, be[jnp.minimum(i, nu[0] - 1)], 0, 0))
    return pl.pallas_call(
        functools.partial(_experts_kernel, d_exp=d_exp),
        out_shape=jax.ShapeDtypeStruct((p_rows, d), F32),
        grid_spec=pltpu.PrefetchScalarGridSpec(
            num_scalar_prefetch=2,
            grid=(n_blk,),
            in_specs=[
                pl.BlockSpec((MOE_BLK, dw), lambda i, be, nu: (jnp.minimum(i, nu[0] - 1), 0)),
                wspec(d, d_exp), wspec(d, d_exp), wspec(d_exp, d),
            ],
            out_specs=pl.BlockSpec((MOE_BLK, d), lambda i, be, nu: (i, 0)),
            scratch_shapes=[
                pltpu.VMEM((d, 2 * d_exp), BF16),
                pltpu.VMEM((d_exp, d), BF16),
            ],
        ),
        compiler_params=pltpu.CompilerParams(
            dimension_semantics=("arbitrary",), vmem_limit_bytes=VMEM_LIMIT),
        name="moe_experts",
    )(blk_exp, nused, xs, w1, w3, w2)


def _moe_layer(l, groups, P):
    eid_all = jnp.concatenate([g[1] for g in groups], axis=1)
    n_all = eid_all.shape[1]
    a = 2 * n_all
    assert a % MOE_BLK == 0
    n_blk = a // MOE_BLK + N_EXPERTS
    counts_f = sum(g[2][0] for g in groups)
    dest_row = _plan(eid_all.reshape(1, a), counts_f)
    counts = counts_f[:N_EXPERTS].astype(I32)
    pends = jnp.cumsum((counts + MOE_BLK - 1) // MOE_BLK * MOE_BLK)
    blk_start = jnp.arange(n_blk, dtype=I32) * MOE_BLK
    blk_exp = jnp.minimum(jnp.sum((pends[None, :] <= blk_start[:, None]).astype(I32), axis=1), N_EXPERTS - 1)
    nused = (pends[-1] // MOE_BLK).astype(I32).reshape(1)
    dest = dest_row.reshape(2, n_all)
    dests = []
    off = 0
    for hf, _, _ in groups:
        dests.append(dest[:, off:off + hf.shape[0]])
        off += hf.shape[0]
    xs = _dispatch([g[0] for g in groups], dests, pends.astype(I32), nused, n_blk, DISPATCH_TILE)
    yr = _experts(xs, blk_exp.astype(I32), nused, l, P["w1"], P["w3"], P["w2"])
    return yr, dests


def _gather_begin(step, slot, nsteps, idx_hbm, src_hbm, idx_smem, buf, sem_idx, sem_rows, nrows, n_chunks):
    nslot = 1 - slot
    last = nsteps - 1
    nxt = jnp.minimum(step + 1, last)
    nxt2 = jnp.minimum(step + 2, last)
    per = nrows // n_chunks

    def idx_copy(k, sl):
        return pltpu.make_async_copy(idx_hbm.at[k], idx_smem.at[sl], sem_idx.at[sl])

    def row_copy(sl, r):
        return pltpu.make_async_copy(src_hbm.at[pl.ds(idx_smem[sl, r], 1)], buf.at[sl, pl.ds(r, 1)],
                                     sem_rows.at[sl])

    def rows_wait(sl):
        pltpu.make_async_copy(src_hbm.at[pl.ds(0, nrows)], buf.at[sl], sem_rows.at[sl]).wait()

    @pl.when(step == 0)
    def _():
        first = idx_copy(0, 0)
        first.start()
        first.wait()

        def body(g, carry):
            for u in range(GATHER_UNROLL):
                row_copy(0, g * GATHER_UNROLL + u).start()
            return carry
        lax.fori_loop(0, nrows // GATHER_UNROLL, body, 0)
        idx_copy(nxt, 1).start()

    idx_copy(nxt, nslot).wait()
    rows_wait(slot)
    idx_copy(nxt2, slot).start()

    def issue_chunk(c):
        for r in range(c * per, (c + 1) * per):
            row_copy(nslot, r).start(priority=r % 2)

    def finish():
        @pl.when(step == last)
        def _():
            rows_wait(nslot)
            idx_copy(nxt2, slot).wait()

    return issue_chunk, finish


def _combine(slot, x1_ref, gate_ref, ybuf, tm):
    g = gate_ref[...]
    y = g[:, 0:1] * ybuf[slot, 0:tm, :] + g[:, 1:2] * ybuf[slot, tm:2 * tm, :]
    return x1_ref[...] + y


def _mid_kernel(dest_hbm, yr_hbm, x1_ref, gate_ref, cos_ref, sin_ref,
                gkv_ref, wkvt_ref, gmix_ref, wqt_ref, wqm_ref,
                x2_ref, kt_ref, vt_ref, qt_ref, qm_ref,
                idx_smem, ybuf, sem_idx, sem_rows, *, tm, d_att):
    n_heads = d_att // HEAD_DIM
    n_chunks = 8
    for par in range(2):
        tok = slice(par * tm, (par + 1) * tm)
        issue_chunk, finish = _gather_begin(2 * pl.program_id(0) + par, par, 2 * pl.num_programs(0),
                                            dest_hbm, yr_hbm, idx_smem, ybuf, sem_idx, sem_rows,
                                            2 * tm, n_chunks)
        chunks = iter(range(n_chunks))
        x2 = _combine(par, x1_ref.at[tok], gate_ref.at[tok], ybuf, tm)
        x2_ref[tok, :] = x2
        issue_chunk(next(chunks))
        cos_t = cos_ref[:, tok]
        sin_t = sin_ref[:, tok]
        hk = _rms(x2, gkv_ref[...]).astype(BF16)
        issue_chunk(next(chunks))
        vt_ref[0, :, tok] = lax.dot_general(wkvt_ref[d_att:, :], hk, NT_DIMS, preferred_element_type=F32)
        issue_chunk(next(chunks))
        kt_ref[0, :, tok] = lax.dot_general(wkvt_ref[:d_att, :], hk, NT_DIMS, preferred_element_type=F32)
        issue_chunk(next(chunks))
        _rot_t(kt_ref, tok, cos_t, sin_t, n_heads)
        hq = _rms(x2, gmix_ref[...]).astype(BF16)
        issue_chunk(next(chunks))
        qt_ref[0, :, tok] = lax.dot_general(wqt_ref[...], hq, NT_DIMS, preferred_element_type=F32)
        issue_chunk(next(chunks))
        _rot_t(qt_ref, tok, cos_t, sin_t, n_heads)
        issue_chunk(next(chunks))
        qm_ref[tok, :] = jnp.dot(hq, wqm_ref[...], preferred_element_type=F32)
        for c in chunks:
            issue_chunk(c)
        finish()


def _mid(x1, gate, yr, dest, pos, n_seq, P, tm):
    n, d = x1.shape
    d_att = P["w_kv_t_bf"].shape[0] // 2
    tg = 2 * tm
    s_out = n // n_seq
    assert s_out % tg == 0
    nt = s_out // tg
    half = ROT_DIM // 2
    reps = s_out // pos.shape[0]
    inv = ROPE_THETA ** (-jnp.arange(half, dtype=F32) * (2.0 / ROT_DIM))
    ang_t = jnp.tile(inv[:, None] * pos.astype(F32)[None, :], (1, reps))
    const = lambda *shape: pl.BlockSpec(shape, lambda i: (0,) * len(shape))
    tok = lambda w: pl.BlockSpec((tg, w), lambda i: (i, 0))
    tbl_t = pl.BlockSpec((half, tg), lambda i: (0, i % nt))
    feat_t = pl.BlockSpec((1, d_att, tg), lambda i: (i // nt, 0, i % nt))
    feat_shape = jax.ShapeDtypeStruct((n_seq, d_att, s_out), F32)
    return pl.pallas_call(
        functools.partial(_mid_kernel, tm=tm, d_att=d_att),
        out_shape=(jax.ShapeDtypeStruct((n, d), F32), feat_shape, feat_shape, feat_shape,
                   jax.ShapeDtypeStruct((n, D_MEMQ), F32)),
        grid=(n // tg,),
        in_specs=[
            pl.BlockSpec(memory_space=pl.ANY), pl.BlockSpec(memory_space=pl.ANY),
            tok(d), tok(2), tbl_t, tbl_t,
            const(1, d), const(2 * d_att, d), const(1, d), const(d_att, d), const(d, D_MEMQ),
        ],
        out_specs=(tok(d), feat_t, feat_t, feat_t, tok(D_MEMQ)),
        scratch_shapes=[
            pltpu.SMEM((2, 2 * tm), I32),
            pltpu.VMEM((2, 2 * tm, d), F32),
            pltpu.SemaphoreType.DMA((2,)),
            pltpu.SemaphoreType.DMA((2,)),
        ],
        compiler_params=pltpu.CompilerParams(
            dimension_semantics=("arbitrary",), vmem_limit_bytes=VMEM_LIMIT),
        name="mid",
    )(_dest_tiles(dest, tm), yr, x1, gate, jnp.cos(ang_t), jnp.sin(ang_t),
      P["g_kv"].reshape(1, d), P["w_kv_t_bf"], P["g_mix"][1].reshape(1, d), P["w_q_t_bf"], P["w_qm_bf"])


def _final_kernel(dest_hbm, yr_hbm, x1_ref, gate_ref, g_ref, y_ref, idx_smem, ybuf, sem_idx, sem_rows, *, tm):
    for par in range(2):
        tok = slice(par * tm, (par + 1) * tm)
        issue_chunk, finish = _gather_begin(2 * pl.program_id(0) + par, par, 2 * pl.num_programs(0),
                                            dest_hbm, yr_hbm, idx_smem, ybuf, sem_idx, sem_rows, 2 * tm, 1)
        issue_chunk(0)
        y_ref[tok, :] = _rms(_combine(par, x1_ref.at[tok], gate_ref.at[tok], ybuf, tm), g_ref[...])
        finish()


def _final(x1, gate, yr, dest, g_final, tm):
    n, d = x1.shape
    assert n % (2 * tm) == 0
    tok = lambda w: pl.BlockSpec((2 * tm, w), lambda i: (i, 0))
    return pl.pallas_call(
        functools.partial(_final_kernel, tm=tm),
        out_shape=jax.ShapeDtypeStruct((n, d), F32),
        grid=(n // (2 * tm),),
        in_specs=[
            pl.BlockSpec(memory_space=pl.ANY), pl.BlockSpec(memory_space=pl.ANY),
            tok(d), tok(2), pl.BlockSpec((1, d), lambda i: (0, 0)),
        ],
        out_specs=tok(d),
        scratch_shapes=[
            pltpu.SMEM((2, 2 * tm), I32),
            pltpu.VMEM((2, 2 * tm, d), F32),
            pltpu.SemaphoreType.DMA((2,)),
            pltpu.SemaphoreType.DMA((2,)),
        ],
        compiler_params=pltpu.CompilerParams(
            dimension_semantics=("arbitrary",), vmem_limit_bytes=VMEM_LIMIT),
        name="final",
    )(_dest_tiles(dest, tm), yr, x1, gate, g_final.reshape(1, d))


def _moba_prompt_kernel(qt_ref, kt_ref, vt_ref, o_ref, kmean, krows, vones, sel_sc, s_buf, p_buf, acc,
                        *, s_len, n_grp):
    qb_i = pl.program_id(2)
    nb = s_len // MOBA_BLOCK
    n_h = LANE // HEAD_DIM
    tq = MOBA_BLOCK
    nq = n_h * tq
    scale = HEAD_DIM ** -0.5
    feat = lax.broadcasted_iota(I32, (LANE, MOBA_BLOCK), 0) // HEAD_DIM
    groups = range(n_grp)

    @pl.when(qb_i == 0)
    def _():
        kmean[...] = jnp.zeros(kmean.shape, F32)
        for g in groups:
            rows = slice(g * LANE, (g + 1) * LANE)
            for jb in range(nb):
                cols = slice(jb * MOBA_BLOCK, (jb + 1) * MOBA_BLOCK)
                kr = kt_ref[0, rows, cols].T
                kmean[g, jb:jb + 1, :] = jnp.sum(kr, axis=0, keepdims=True) * (1.0 / MOBA_BLOCK)
                krows[g, jb] = kr.astype(BF16)
                vblk = vt_ref[0, rows, cols]
                for hh in range(n_h):
                    vones[g, hh, jb] = jnp.where(feat == hh, vblk, 1.0).astype(BF16)

    key = lax.broadcasted_iota(I32, (MOBA_BLOCK, nq), 0)
    qcol = lax.broadcasted_iota(I32, (MOBA_BLOCK, nq), 1) % tq
    qs, m0, p_own = [], [], []
    for g in groups:
        qt = qt_ref[0, g * LANE:(g + 1) * LANE, :]
        qstack = jnp.concatenate([jnp.where(feat == hh, qt, 0.0) for hh in range(n_h)], axis=1)
        gate = jnp.dot(kmean[g], qstack, preferred_element_type=F32, precision=HIGHEST)
        brow = lax.broadcasted_iota(I32, gate.shape, 0)
        gate = jnp.where(brow < qb_i, gate, NEG_INF)
        sel = _top_k_mask(gate, brow, 0)
        for jb in range(nb):
            sel_sc[g, jb] = sel[jb:jb + 1, :]
        qs.append((qstack * scale).astype(BF16))
        s_own = jnp.where(key <= qcol, jnp.dot(krows[g, qb_i], qs[g], preferred_element_type=F32), NEG_INF)
        m0.append(jnp.max(s_own, axis=0, keepdims=True))
        p_own.append(jnp.exp(s_own - m0[g]).astype(BF16))

    def put_scores(g, j, slot):
        s_buf[g, slot] = jnp.dot(krows[g, j], qs[g], preferred_element_type=F32)

    def add_block(g, alpha, j, slot):
        for hh in range(n_h):
            acc[g, hh] = (alpha[:, hh * tq:(hh + 1) * tq] * acc[g, hh]
                          + jnp.dot(vones[g, hh, j], p_buf[g, slot, :, hh * tq:(hh + 1) * tq],
                                    preferred_element_type=F32))

    for g in groups:
        acc[g] = jnp.zeros((n_h, LANE, tq), F32)
        p_buf[g, 1] = p_own[g]
        put_scores(g, 0, 0)

    def half_trip(j, slot, carry):
        out = []
        for g in groups:
            m, alpha_prev = carry[g]
            add_block(g, alpha_prev, jnp.where(j == 0, qb_i, j - 1), 1 - slot)
            put_scores(g, jnp.minimum(j + 1, nb - 1), 1 - slot)
            chosen = sel_sc[g, j] > 0.0
            m_new = jnp.maximum(m, jnp.where(chosen, jnp.max(s_buf[g, slot], axis=0, keepdims=True), NEG_INF))
            p_buf[g, slot] = jnp.exp(s_buf[g, slot] - jnp.where(chosen, m_new, jnp.inf)).astype(BF16)
            out.append((m_new, jnp.exp(m - m_new)))
        return tuple(out)

    def body(t, carry):
        return half_trip(2 * t + 1, 1, half_trip(2 * t, 0, carry))

    n_trips = (qb_i + 1) // 2
    one = jnp.ones((1, nq), F32)
    state = lax.fori_loop(0, n_trips, body, tuple((m0[g], one) for g in groups))
    last = jnp.where(n_trips == 0, qb_i, 2 * n_trips - 1)
    for g in groups:
        add_block(g, state[g][1], last, 1)
        out_t = jnp.zeros((LANE, tq), F32)
        for hh in range(n_h):
            other = (1 - hh) * HEAD_DIM
            a_h = acc[g, hh]
            out_t = jnp.where(feat == hh, a_h / a_h[other:other + 1, :], out_t)
        o_ref[:, g * LANE:(g + 1) * LANE] = out_t.T


def _moba_prompt(qt, kt, vt, b, s, n_grp=2):
    d_att = qt.shape[1]
    nb = s // MOBA_BLOCK
    fw = n_grp * LANE
    assert s % MOBA_BLOCK == 0 and LANE // HEAD_DIM == 2 and d_att % fw == 0
    nb_pad = -(-nb // SUBLANE) * SUBLANE
    kv_spec = pl.BlockSpec((1, fw, s), lambda i, hp, qb: (i, hp, 0))
    return pl.pallas_call(
        functools.partial(_moba_prompt_kernel, s_len=s, n_grp=n_grp),
        out_shape=jax.ShapeDtypeStruct((b * s, d_att), F32),
        grid=(b, d_att // fw, nb),
        in_specs=[
            pl.BlockSpec((1, fw, MOBA_BLOCK), lambda i, hp, qb: (i, hp, qb)),
            kv_spec, kv_spec,
        ],
        out_specs=pl.BlockSpec((MOBA_BLOCK, fw), lambda i, hp, qb: (i * nb + qb, hp)),
        scratch_shapes=[
            pltpu.VMEM((n_grp, nb_pad, LANE), F32),
            pltpu.VMEM((n_grp, nb, MOBA_BLOCK, LANE), BF16),
            pltpu.VMEM((n_grp, 2, nb, LANE, MOBA_BLOCK), BF16),
            pltpu.VMEM((n_grp, nb, 1, 2 * MOBA_BLOCK), F32),
            pltpu.VMEM((n_grp, 2, MOBA_BLOCK, 2 * MOBA_BLOCK), F32),
            pltpu.VMEM((n_grp, 2, MOBA_BLOCK, 2 * MOBA_BLOCK), BF16),
            pltpu.VMEM((n_grp, 2, LANE, MOBA_BLOCK), F32),
        ],
        compiler_params=pltpu.CompilerParams(
            dimension_semantics=("arbitrary", "arbitrary", "arbitrary"), vmem_limit_bytes=VMEM_LIMIT),
        name="moba_prompt",
    )(qt, kt, vt)


def _head_diag(o, n_heads, ds):
    return jnp.concatenate(
        [o[h * ds:(h + 1) * ds, h * HEAD_DIM:(h + 1) * HEAD_DIM] for h in range(n_heads)], axis=0)


def _moba_sample_kernel(pt_ref, q_ref, kn_ref, vn_ref, *rest, ds, n_heads, nb, bps):
    pages = rest[:4 * bps]
    o_ref, qbd, g_all, m_all, l_all, o_all = rest[4 * bps:]
    jg = pl.program_id(1)
    rows = n_heads * ds
    d_att = n_heads * HEAD_DIM
    scale = HEAD_DIM ** -0.5
    blane = lax.broadcasted_iota(I32, (rows, LANE), 1)

    @pl.when(jg == 0)
    def _():
        q = q_ref[...]
        lane = lax.broadcasted_iota(I32, (ds, d_att), 1)
        for h in range(n_heads):
            qbd[h * ds:(h + 1) * ds, :] = jnp.where(lane // HEAD_DIM == h, q, 0.0)
        g_all[...] = jnp.full((rows, LANE), NEG_INF, F32)
        m_all[...] = jnp.zeros((rows, LANE), F32)
        l_all[...] = jnp.zeros((rows, LANE), F32)

    qb = qbd[...].astype(BF16)
    g_new, m_new, l_new = g_all[...], m_all[...], l_all[...]
    for bi in range(bps):
        k0_ref, k1_ref = pages[2 * bi], pages[2 * bi + 1]
        v0_ref, v1_ref = pages[2 * bps + 2 * bi], pages[2 * bps + 2 * bi + 1]
        j = jg * bps + bi
        kt = jnp.concatenate([k0_ref[0].astype(BF16), k1_ref[0].astype(BF16)], axis=1)
        vt = jnp.concatenate([v0_ref[0].astype(BF16), v1_ref[0].astype(BF16)], axis=1)
        s = jnp.dot(qb, kt, preferred_element_type=F32)
        g = jnp.sum(s, axis=-1, keepdims=True) * (1.0 / MOBA_BLOCK)
        m = jnp.max(s, axis=-1, keepdims=True) * scale
        p = jnp.exp(s * scale - m)
        l = jnp.sum(p, axis=-1, keepdims=True)
        o = lax.dot_general(p.astype(BF16), vt, NT_DIMS, preferred_element_type=F32)
        o_all[j] = _head_diag(o, n_heads, ds)
        here = blane == j
        g_new = jnp.where(here, g, g_new)
        m_new = jnp.where(here, m, m_new)
        l_new = jnp.where(here, l, l_new)
    g_all[...] = g_new
    m_all[...] = m_new
    l_all[...] = l_new

    @pl.when(jg == nb // bps - 1)
    def _():
        chosen = _top_k_mask(g_new, blane, -1) > 0.0
        s_own = lax.dot_general(qb, kn_ref[...].astype(BF16), NT_DIMS, preferred_element_type=F32) * scale
        r_i = lax.broadcasted_iota(I32, (rows, ds), 0) % ds
        c_i = lax.broadcasted_iota(I32, (rows, ds), 1)
        s_own = jnp.where(c_i <= r_i, s_own, NEG_INF)
        m_sel = jnp.max(jnp.where(chosen, m_new, NEG_INF), axis=-1, keepdims=True)
        m_tot = jnp.maximum(m_sel, jnp.max(s_own, axis=-1, keepdims=True))
        wgt = jnp.where(chosen, jnp.exp(m_new - m_tot), 0.0)
        p_own = jnp.exp(s_own - m_tot)
        l_tot = jnp.sum(wgt * l_new, axis=-1, keepdims=True) + jnp.sum(p_own, axis=-1, keepdims=True)
        acc = _head_diag(jnp.dot(p_own.astype(BF16), vn_ref[...].astype(BF16), preferred_element_type=F32),
                         n_heads, ds)
        for jj in range(nb):
            acc = acc + wgt[:, jj:jj + 1] * o_all[jj]
        o_ref[0] = acc / l_tot


def _moba_sample(qb, k_new, v_new, cache_k, cache_v, page_table, db, ds, bps=8):
    n_pool, page, n_heads, hd = cache_k.shape
    d_att = n_heads * hd
    n_pages = page_table.shape[1]
    ppb = MOBA_BLOCK // page
    nb = n_pages // ppb
    rows = n_heads * ds
    assert ppb == 2 and n_pages % ppb == 0 and ds <= MOBA_BLOCK and nb <= LANE and nb % bps == 0
    ck = cache_k.transpose(0, 2, 3, 1).reshape(n_pool, d_att, page)
    cv = cache_v.transpose(0, 2, 3, 1).reshape(n_pool, d_att, page)
    seq = pl.BlockSpec((ds, d_att), lambda i, j, pt: (i, 0))
    ppg = ppb * bps
    pg = lambda off: pl.BlockSpec((1, d_att, page), lambda i, j, pt: (pt[i * n_pages + ppg * j + off], 0, 0))
    out = pl.pallas_call(
        functools.partial(_moba_sample_kernel, ds=ds, n_heads=n_heads, nb=nb, bps=bps),
        out_shape=jax.ShapeDtypeStruct((db, rows, hd), F32),
        grid_spec=pltpu.PrefetchScalarGridSpec(
            num_scalar_prefetch=1,
            grid=(db, nb // bps),
            in_specs=[seq, seq, seq] + [pg(o) for o in range(ppg)] * 2,
            out_specs=pl.BlockSpec((1, rows, hd), lambda i, j, pt: (i, 0, 0)),
            scratch_shapes=[
                pltpu.VMEM((rows, d_att), F32),
                pltpu.VMEM((rows, LANE), F32),
                pltpu.VMEM((rows, LANE), F32),
                pltpu.VMEM((rows, LANE), F32),
                pltpu.VMEM((nb, rows, hd), F32),
            ],
        ),
        compiler_params=pltpu.CompilerParams(
            dimension_semantics=("arbitrary", "arbitrary"), vmem_limit_bytes=VMEM_LIMIT),
        name="moba_sample",
    )(page_table.reshape(-1), qb, k_new, v_new, *([ck] * ppg), *([cv] * ppg))
    return out.reshape(db, n_heads, ds, hd).transpose(0, 2, 1, 3).reshape(db * ds, d_att)


def kernel(x_prompt, x_sample, state_conv, cache_k, cache_v, cache_mem_k, cache_mem_v, page_table,
           mem_prompt, g_mix, g_ffn, g_final, g_mem, w_mem_kv, w_in_a, conv_w, conv_b, cln_g, cln_b,
           w_out_a, g_kv, w_kv, w_in_b, w_out_b, w_rg, b_rg, w_re, b_re, w1, w3, w2):
    depth = g_mix.shape[0]
    n_heads = w_kv.shape[1] // (2 * HEAD_DIM)
    d_att = n_heads * HEAD_DIM
    routers = [_router_weights(w_rg[l], b_rg[l], w_re[l], b_re[l]) for l in range(depth)]
    P = {
        "g_mix": g_mix, "g_ffn": g_ffn, "g_final": g_final, "g_kv": g_kv,
        "w_in_a_bf": w_in_a[0].astype(BF16), "w_out_a_bf": w_out_a[0].astype(BF16),
        "w_kv_t_bf": w_kv.T.astype(BF16),
        "w_q_t_bf": w_in_b[0][:, :d_att].T.astype(BF16), "w_qm_bf": w_in_b[0][:, d_att:].astype(BF16),
        "w_out_b_bf": w_out_b[0].astype(BF16),
        "conv_w": conv_w, "conv_b": conv_b, "cln_g": cln_g, "cln_b": cln_b,
        "wr": [r[0] for r in routers], "br": [r[1] for r in routers],
        "w1": w1, "w3": w3, "w2": w2,
    }
    b_p, s_p, d = x_prompt.shape
    db, ds, _ = x_sample.shape
    n_p, n_s = b_p * s_p, db * ds
    mem_len = mem_prompt.shape[1]
    past = page_table.shape[1] * cache_k.shape[1]
    assert past % MOBA_BLOCK == 0
    ts_p, tm_p = 256, 256
    ts_s, tm_s = ds, n_s // 2
    nb_s = math.gcd(db, 8)

    memkv_t = _memkv(mem_prompt, g_mem, w_mem_kv.transpose(0, 2, 1).astype(BF16))
    kv6 = memkv_t.reshape(depth, b_p, 2, MEM_H, HEAD_DIM, mem_len).transpose(2, 0, 1, 5, 3, 4)
    mem_k_p, mem_v_p = kv6[0], kv6[1]
    cmk = cache_mem_k.transpose(0, 1, 3, 4, 2).reshape(depth, db, D_MEMQ, mem_len)
    cmv = cache_mem_v.transpose(0, 1, 3, 4, 2).reshape(depth, db, D_MEMQ, mem_len)
    mem_p = (memkv_t, memkv_t, (0, 1))
    mem_s = (cmk, cmv, (0, 0))

    ctx_p = jnp.zeros((b_p, CONV_W - 1, state_conv.shape[-1]), F32)
    x1_p, hf_p, gate_p, eid_p, cnt_p, conv_p = _layer0(x_prompt.reshape(n_p, d), b_p, s_p, ctx_p, *mem_p, 0, P, ts_p)
    x1_s, hf_s, gate_s, eid_s, cnt_s, conv_s = _layer0(x_sample.reshape(n_s, d), db, ds, state_conv[0], *mem_s, 0, P, ts_s, nb_s)
    yr, (dest_p, dest_s) = _moe_layer(0, [(hf_p, eid_p, cnt_p), (hf_s, eid_s.T, cnt_s)], P)

    x2_p, kt_p, vt_p, qt_p, qm_p = _mid(x1_p, gate_p, yr, dest_p, jnp.arange(s_p), b_p, P, tm_p)
    x2_s, kt_s, vt_s, qt_s, qm_s = _mid(x1_s, gate_s, yr, dest_s, past + jnp.arange(ds), 1, P, tm_s)
    c1_p = _moba_prompt(qt_p, kt_p, vt_p, b_p, s_p)
    k_new, v_new = kt_s[0].T, vt_s[0].T
    c1_s = _moba_sample(qt_s[0].T, k_new, v_new, cache_k, cache_v, page_table, db, ds)
    x3_p, hf_p, gate_p, eid_p, cnt_p = _layer1(x2_p, c1_p, qm_p, b_p, s_p, *mem_p, 1, P, ts_p)
    x3_s, hf_s, gate_s, eid_s, cnt_s = _layer1(x2_s, c1_s, qm_s, db, ds, *mem_s, 1, P, ts_s, nb_s)
    yr, (dest_p, dest_s) = _moe_layer(1, [(hf_p, eid_p, cnt_p), (hf_s, eid_s.T, cnt_s)], P)
    y_p = _final(x3_p, gate_p, yr, dest_p, g_final, tm_p).reshape(b_p, s_p, d)
    y_s = _final(x3_s, gate_s, yr, dest_s, g_final, tm_s).reshape(db, ds, d)

    k_p = kt_p.reshape(b_p, n_heads, HEAD_DIM, s_p).transpose(0, 3, 1, 2)
    v_p = vt_p.reshape(b_p, n_heads, HEAD_DIM, s_p).transpose(0, 3, 1, 2)
    k_s = k_new.reshape(db, ds, n_heads, HEAD_DIM)
    v_s = v_new.reshape(db, ds, n_heads, HEAD_DIM)
    return (y_p, y_s, conv_p[None], conv_s[None], k_p, v_p, k_s, v_s, mem_k_p, mem_v_p)
```

```python
import functools
import math

import jax
import jax.numpy as jnp
from jax import lax
from jax.experimental import pallas as pl
from jax.experimental.pallas import tpu as pltpu

F32 = jnp.float32
BF16 = jnp.bfloat16
I32 = jnp.int32
U32 = jnp.uint32
HIGHEST = lax.Precision.HIGHEST

EPS = 1e-6
LOG2_E = math.log2(math.e)
HEAD_DIM = 64
MEM_H = 4
D_MEMQ = MEM_H * HEAD_DIM
CONV_W = 31
CTX_ROWS = 32
CTX_PAD = CTX_ROWS - (CONV_W - 1)
MOBA_BLOCK = 256
MOBA_TOPK = 3
ROT_DIM = HEAD_DIM // 4
ROPE_THETA = 500000.0
N_GROUPS = 4
E_PER_GROUP = 16
N_EXPERTS = N_GROUPS * E_PER_GROUP
MOE_BLK = 256
PLAN_TILE_MAX = 512
GATHER_UNROLL = 8
DISPATCH_TILE = 128
ROUTER_LANES = 128
LANE = 128
SUBLANE = 8
VMEM_LIMIT = 56 * 1024 * 1024

NEG_INF = float("-inf")
BIG_IDX = 1 << 20
NT_DIMS = (((1,), (1,)), ((), ()))


def _rms(x, g):
    return x * lax.rsqrt(jnp.mean(x * x, axis=-1, keepdims=True) + EPS) * g


def _pack_bf16_pairs(x):
    half = x.shape[1] // 2
    hi = lax.bitcast_convert_type(x[:, :half].astype(BF16).astype(F32), U32)
    lo = lax.bitcast_convert_type(x[:, half:].astype(BF16).astype(F32), U32)
    return hi | (lo >> 16)


def _unpack_bf16_pairs(u):
    hi = lax.bitcast_convert_type(u & jnp.uint32(0xFFFF0000), F32)
    lo = lax.bitcast_convert_type(u << 16, F32)
    return jnp.concatenate([hi, lo], axis=1).astype(BF16)


def _first_argmax(vals, idx, axis):
    m = jnp.max(vals, axis=axis, keepdims=True)
    first = jnp.min(jnp.where(vals == m, idx, BIG_IDX), axis=axis, keepdims=True)
    return m, first


def _top_k_mask(gate, idx, axis):
    sel = jnp.zeros(gate.shape, F32)
    for _ in range(MOBA_TOPK):
        m, first = _first_argmax(gate, idx, axis)
        pick = jnp.logical_and(idx == first, m > NEG_INF)
        sel = jnp.where(pick, 1.0, sel)
        gate = jnp.where(pick, NEG_INF, gate)
    return sel


def _rot_t(xt_ref, tok, cos_t, sin_t, n_heads):
    half = ROT_DIM // 2
    for hh in range(n_heads):
        r0 = hh * HEAD_DIM
        xa = xt_ref[0, r0:r0 + half, tok]
        xb = xt_ref[0, r0 + half:r0 + ROT_DIM, tok]
        xt_ref[0, r0:r0 + half, tok] = xa * cos_t - xb * sin_t
        xt_ref[0, r0 + half:r0 + ROT_DIM, tok] = xb * cos_t + xa * sin_t


def _memkv_kernel(mem_ref, g_ref, wt_ref, out_ref):
    h = _rms(mem_ref[0], g_ref[0]).astype(BF16)
    out_ref[0, 0] = lax.dot_general(wt_ref[0], h, NT_DIMS, preferred_element_type=F32)


def _memkv(mem, g_mem, w_mem_kv_t_bf):
    b, m, d = mem.shape
    depth = g_mem.shape[0]
    n_out = w_mem_kv_t_bf.shape[1]
    return pl.pallas_call(
        _memkv_kernel,
        out_shape=jax.ShapeDtypeStruct((depth, b, n_out, m), F32),
        grid=(depth, b),
        in_specs=[
            pl.BlockSpec((1, m, d), lambda l, i: (i, 0, 0)),
            pl.BlockSpec((1, 1, d), lambda l, i: (l, 0, 0)),
            pl.BlockSpec((1, n_out, d), lambda l, i: (l, 0, 0)),
        ],
        out_specs=pl.BlockSpec((1, 1, n_out, m), lambda l, i: (l, i, 0, 0)),
        compiler_params=pltpu.CompilerParams(dimension_semantics=("arbitrary", "arbitrary")),
        name="memkv",
    )(mem, g_mem.reshape(depth, 1, d), w_mem_kv_t_bf)


def _mem_attn_into(mix, off, qm_all, mkt_ref, mvt_ref, nb):
    t = qm_all.shape[0] // nb
    scale = HEAD_DIM ** -0.5
    lane = lax.broadcasted_iota(I32, (t, D_MEMQ), 1)
    for bb in range(nb):
        qm = qm_all[bb * t:(bb + 1) * t]
        mkt = mkt_ref[0, bb].astype(BF16)
        mvt = mvt_ref[0, bb].astype(BF16)
        om = jnp.zeros(qm.shape, F32)
        for hh in range(MEM_H):
            in_head = (lane // HEAD_DIM) == hh
            q = jnp.where(in_head, qm, 0.0).astype(BF16)
            s = jnp.dot(q, mkt, preferred_element_type=F32) * scale
            m = jnp.max(s, axis=-1, keepdims=True)
            e = jnp.exp(s - m)
            l = jnp.sum(e, axis=-1, keepdims=True)
            o = lax.dot_general(e.astype(BF16), mvt, NT_DIMS, preferred_element_type=F32) / l
            om = jnp.where(in_head, o, om)
        mix[bb * t:(bb + 1) * t, off:off + D_MEMQ] = om.astype(BF16)


def _out_and_route(x, mix, wout_ref, gffn_ref, wr_ref, br_ref, x1_ref, hf_ref, gate_ref, eid_ref, cnt_ref, first):
    x1 = x + jnp.dot(mix[...], wout_ref[...], preferred_element_type=F32)
    x1_ref[...] = x1
    hf = _rms(x1, gffn_ref[...])
    hf_ref[...] = _pack_bf16_pairs(hf)
    hf_hi = hf.astype(BF16)
    hf_lo = (hf - hf_hi.astype(F32)).astype(BF16)
    logits = jnp.dot(jnp.concatenate([hf_hi, hf_lo, hf_hi], axis=1), wr_ref[...],
                     preferred_element_type=F32) + br_ref[...]
    lane = lax.broadcasted_iota(I32, logits.shape, 1)
    gl = jnp.where((lane >= N_EXPERTS) & (lane < N_EXPERTS + N_GROUPS), logits, NEG_INF)
    gmax, glane = _first_argmax(gl, lane, -1)
    p_group = 1.0 / jnp.sum(jnp.exp(gl - gmax), axis=-1, keepdims=True)
    e_lo = (glane - N_EXPERTS) * E_PER_GROUP
    el = jnp.where((lane >= e_lo) & (lane < e_lo + E_PER_GROUP), logits, NEG_INF)
    m1, i1 = _first_argmax(el, lane, -1)
    el2 = jnp.where(lane == i1, NEG_INF, el)
    m2, i2 = _first_argmax(el2, lane, -1)
    e2 = jnp.exp(m2 - m1)
    g1 = p_group / (1.0 + e2)
    g2 = g1 * e2
    picked = jnp.where(lane == i1, 1.0, 0.0) + jnp.where(lane == i2, 1.0, 0.0)
    tile_counts = jnp.sum(picked, axis=0, keepdims=True)

    @pl.when(first)
    def _():
        cnt_ref[...] = tile_counts

    @pl.when(jnp.logical_not(first))
    def _():
        cnt_ref[...] = cnt_ref[...] + tile_counts

    lane2 = lax.broadcasted_iota(I32, gate_ref.shape, 1)
    gate_ref[...] = jnp.where(lane2 == 0, g1, g2)
    if eid_ref.shape[1] == 2:
        eid_ref[...] = jnp.where(lane2 == 0, i1, i2)
    else:
        t = x.shape[0]
        diag = lax.broadcasted_iota(I32, (t, t), 0) == lax.broadcasted_iota(I32, (t, t), 1)
        rows = [jnp.sum(jnp.where(diag, col.astype(F32), 0.0), axis=0, keepdims=True) for col in (i1, i2)]
        eid_ref[...] = jnp.concatenate(rows, axis=0).astype(I32)


def _eid_out(n, ts, nt):
    if ts % LANE == 0:
        return jax.ShapeDtypeStruct((2, n), I32), pl.BlockSpec((2, ts), lambda i, t: (0, i * nt + t))
    return jax.ShapeDtypeStruct((n, 2), I32), pl.BlockSpec((ts, 2), lambda i, t: (i * nt + t, 0))


def _router_weights(w_rg, b_rg, w_re, b_re):
    d = w_rg.shape[0]
    pad = ROUTER_LANES - N_EXPERTS - N_GROUPS
    wr = jnp.concatenate([w_re, w_rg, jnp.zeros((d, pad), F32)], axis=1)
    br = jnp.concatenate([b_re, b_rg, jnp.zeros((pad,), F32)]).reshape(1, ROUTER_LANES)
    w_hi = wr.astype(BF16)
    w_lo = (wr - w_hi.astype(F32)).astype(BF16)
    return jnp.concatenate([w_hi, w_hi, w_lo], axis=0), br


def _layer0_kernel(x_ref, ctx_ref, mk_ref, mv_ref, gmix_ref, win_ref, cw_ref, cb_ref, lg_ref, lb_ref,
                   wout_ref, gffn_ref, wr_ref, br_ref,
                   x1_ref, hf_ref, gate_ref, eid_ref, cnt_ref, nc_ref,
                   zext, zsh, cbuf, mix, *, ts, rc, d_conv, nb):
    t = pl.program_id(1)
    nt = pl.num_programs(1)

    x = x_ref[...]
    h = _rms(x, gmix_ref[...]).astype(BF16)
    u = jnp.dot(h, win_ref[...], preferred_element_type=F32)
    z = u[:, :d_conv] * jax.nn.sigmoid(u[:, d_conv:2 * d_conv])
    qm = u[:, 2 * d_conv:]

    for bb in range(nb):
        rows = slice(bb * ts, (bb + 1) * ts)

        @pl.when(t == 0)
        def _():
            zext[bb, 0:CTX_PAD, :] = jnp.zeros((CTX_PAD, d_conv), F32)
            zext[bb, CTX_PAD:CTX_ROWS, :] = ctx_ref[bb]

        @pl.when(t > 0)
        def _():
            zext[bb, 0:CTX_ROWS, :] = zext[bb, ts:ts + CTX_ROWS, :]

        zext[bb, CTX_ROWS:CTX_ROWS + ts, :] = z[rows]

        @pl.when(t == nt - 1)
        def _():
            nc_ref[bb] = zext[bb, ts + CTX_PAD: ts + CTX_ROWS, :]

        for sh in range(1, SUBLANE):
            zsh[bb, sh - 1] = zext[bb, sh:sh + ts + CTX_ROWS - SUBLANE, :]

        def chunk(ci, carry):
            r0 = pl.multiple_of(ci * rc, SUBLANE)
            for cbk in range(d_conv // LANE):
                cs = slice(cbk * LANE, (cbk + 1) * LANE)
                acc = jnp.zeros((rc, LANE), F32) + cb_ref[:, cs]
                for w in range(CONV_W):
                    hi, sh = divmod(w + CTX_PAD, SUBLANE)
                    if sh == 0:
                        win = zext[bb, pl.ds(r0 + hi * SUBLANE, rc), cs]
                    else:
                        win = zsh[bb, sh - 1, pl.ds(r0 + hi * SUBLANE, rc), cs]
                    acc = acc + win * cw_ref[w:w + 1, cs]
                cbuf[pl.ds(bb * ts + r0, rc), cs] = acc
            return carry

        lax.fori_loop(0, ts // rc, chunk, 0)

    c = cbuf[...]
    mu = jnp.mean(c, axis=-1, keepdims=True)
    xc = c - mu
    y = xc * lax.rsqrt(jnp.mean(xc * xc, axis=-1, keepdims=True) + EPS) * lg_ref[...] + lb_ref[...]
    mix[:, :d_conv] = (y * jax.nn.sigmoid(y)).astype(BF16)
    _mem_attn_into(mix, d_conv, qm, mk_ref, mv_ref, nb)
    first = jnp.logical_and(pl.program_id(0) == 0, t == 0)
    _out_and_route(x, mix, wout_ref, gffn_ref, wr_ref, br_ref, x1_ref, hf_ref, gate_ref, eid_ref, cnt_ref, first)


def _layer0(x2d, b, s, ctx, mk_arr, mv_arr, kv_rows, l, P, ts, nb=1):
    n, d = x2d.shape
    nt = s // ts
    assert b % nb == 0 and (nb == 1 or nt == 1)
    tr = nb * ts
    d_conv = ctx.shape[-1]
    rc = min(32, ts)
    d_in = P["w_in_a_bf"].shape[-1]
    krow, vrow = kv_rows
    const = lambda *shape: pl.BlockSpec(shape, lambda i, t: (0,) * len(shape))
    tok = lambda w: pl.BlockSpec((tr, w), lambda i, t: (i * nt + t, 0))
    eid_shape, eid_spec = _eid_out(n, tr, nt)
    kern = functools.partial(_layer0_kernel, ts=ts, rc=rc, d_conv=d_conv, nb=nb)
    return pl.pallas_call(
        kern,
        out_shape=(
            jax.ShapeDtypeStruct((n, d), F32),
            jax.ShapeDtypeStruct((n, d // 2), U32),
            jax.ShapeDtypeStruct((n, 2), F32),
            eid_shape,
            jax.ShapeDtypeStruct((1, LANE), F32),
            jax.ShapeDtypeStruct((b, CONV_W - 1, d_conv), F32),
        ),
        grid=(b // nb, nt),
        in_specs=[
            tok(d),
            pl.BlockSpec((nb, CONV_W - 1, d_conv), lambda i, t: (i, 0, 0)),
            pl.BlockSpec((1, nb, D_MEMQ, mk_arr.shape[3]), lambda i, t: (l, i, krow, 0)),
            pl.BlockSpec((1, nb, D_MEMQ, mv_arr.shape[3]), lambda i, t: (l, i, vrow, 0)),
            const(1, d), const(d, d_in), const(CONV_W, d_conv), const(1, d_conv), const(1, d_conv),
            const(1, d_conv), const(d_conv + D_MEMQ, d), const(1, d), const(3 * d, ROUTER_LANES),
            const(1, ROUTER_LANES),
        ],
        out_specs=(tok(d), tok(d // 2), tok(2), eid_spec, const(1, LANE),
                   pl.BlockSpec((nb, CONV_W - 1, d_conv), lambda i, t: (i, 0, 0))),
        scratch_shapes=[
            pltpu.VMEM((nb, ts + CTX_ROWS, d_conv), F32),
            pltpu.VMEM((nb, SUBLANE - 1, ts + CTX_ROWS - SUBLANE, d_conv), F32),
            pltpu.VMEM((tr, d_conv), F32),
            pltpu.VMEM((tr, d_conv + D_MEMQ), BF16),
        ],
        compiler_params=pltpu.CompilerParams(
            dimension_semantics=("arbitrary", "arbitrary"), vmem_limit_bytes=VMEM_LIMIT),
        name="layer0",
    )(x2d, ctx, mk_arr, mv_arr, P["g_mix"][l].reshape(1, d), P["w_in_a_bf"], P["conv_w"][0],
      P["conv_b"][0].reshape(1, d_conv), P["cln_g"][0].reshape(1, d_conv), P["cln_b"][0].reshape(1, d_conv),
      P["w_out_a_bf"], P["g_ffn"][l].reshape(1, d), P["wr"][l], P["br"][l])


def _layer1_kernel(x_ref, c_ref, qm_ref, mk_ref, mv_ref, wout_ref, gffn_ref, wr_ref, br_ref,
                   x1_ref, hf_ref, gate_ref, eid_ref, cnt_ref, mix, *, d_att, nb):
    mix[:, :d_att] = c_ref[...].astype(BF16)
    _mem_attn_into(mix, d_att, qm_ref[...], mk_ref, mv_ref, nb)
    first = jnp.logical_and(pl.program_id(0) == 0, pl.program_id(1) == 0)
    _out_and_route(x_ref[...], mix, wout_ref, gffn_ref, wr_ref, br_ref, x1_ref, hf_ref, gate_ref, eid_ref,
                   cnt_ref, first)


def _layer1(x2d, c2d, qm2d, b, s, mk_arr, mv_arr, kv_rows, l, P, ts, nb=1):
    n, d = x2d.shape
    nt = s // ts
    assert b % nb == 0 and (nb == 1 or nt == 1)
    tr = nb * ts
    d_att = c2d.shape[-1]
    krow, vrow = kv_rows
    const = lambda *shape: pl.BlockSpec(shape, lambda i, t: (0,) * len(shape))
    tok = lambda w: pl.BlockSpec((tr, w), lambda i, t: (i * nt + t, 0))
    eid_shape, eid_spec = _eid_out(n, tr, nt)
    return pl.pallas_call(
        functools.partial(_layer1_kernel, d_att=d_att, nb=nb),
        out_shape=(
            jax.ShapeDtypeStruct((n, d), F32),
            jax.ShapeDtypeStruct((n, d // 2), U32),
            jax.ShapeDtypeStruct((n, 2), F32),
            eid_shape,
            jax.ShapeDtypeStruct((1, LANE), F32),
        ),
        grid=(b // nb, nt),
        in_specs=[
            tok(d), tok(d_att), tok(D_MEMQ),
            pl.BlockSpec((1, nb, D_MEMQ, mk_arr.shape[3]), lambda i, t: (l, i, krow, 0)),
            pl.BlockSpec((1, nb, D_MEMQ, mv_arr.shape[3]), lambda i, t: (l, i, vrow, 0)),
            const(d_att + D_MEMQ, d), const(1, d), const(3 * d, ROUTER_LANES), const(1, ROUTER_LANES),
        ],
        out_specs=(tok(d), tok(d // 2), tok(2), eid_spec, const(1, LANE)),
        scratch_shapes=[pltpu.VMEM((tr, d_att + D_MEMQ), BF16)],
        compiler_params=pltpu.CompilerParams(
            dimension_semantics=("arbitrary", "arbitrary"), vmem_limit_bytes=VMEM_LIMIT),
        name="layer1",
    )(x2d, c2d, qm2d, mk_arr, mv_arr, P["w_out_b_bf"], P["g_ffn"][l].reshape(1, d), P["wr"][l], P["br"][l])


def _plan_kernel(eid_ref, counts_ref, dest_ref, base, pstart, *, ta):
    t = pl.program_id(0)
    expert = lax.broadcasted_iota(I32, (LANE, ta), 0)
    onehot = eid_ref[...] == expert
    ohf = jnp.where(onehot, 1.0, 0.0)

    @pl.when(t == 0)
    def _():
        counts = counts_ref[...]
        pc = (((counts.astype(I32) + (MOE_BLK - 1)) // MOE_BLK) * MOE_BLK).astype(F32)
        r = lax.broadcasted_iota(I32, (LANE, LANE), 0)
        c = lax.broadcasted_iota(I32, (LANE, LANE), 1)
        below = jnp.where(c < r, 1.0, 0.0)
        pstart[...] = jnp.dot(below, pc, preferred_element_type=F32, precision=HIGHEST)
        base[...] = jnp.zeros(base.shape, F32)

    r = lax.broadcasted_iota(I32, (ta, ta), 0)
    c = lax.broadcasted_iota(I32, (ta, ta), 1)
    upto = jnp.where(r <= c, 1.0, 0.0).astype(BF16)
    prefix = jnp.dot(ohf.astype(BF16), upto, preferred_element_type=F32)
    slot = pstart[:, 0:1] + base[:, 0:1] + prefix - 1.0
    dest_ref[...] = jnp.sum(jnp.where(onehot, slot, 0.0), axis=0, keepdims=True).astype(I32)
    base[...] = base[...] + jnp.sum(ohf, axis=1, keepdims=True)


def _plan(eid_row, counts):
    a = eid_row.shape[1]
    ta = max(t for t in range(LANE, PLAN_TILE_MAX + 1, LANE) if a % t == 0)
    return pl.pallas_call(
        functools.partial(_plan_kernel, ta=ta),
        out_shape=jax.ShapeDtypeStruct((1, a), I32),
        grid=(a // ta,),
        in_specs=[pl.BlockSpec((1, ta), lambda t: (0, t)), pl.BlockSpec((LANE, LANE), lambda t: (0, 0))],
        out_specs=pl.BlockSpec((1, ta), lambda t: (0, t)),
        scratch_shapes=[pltpu.VMEM((LANE, LANE), F32)] * 2,
        compiler_params=pltpu.CompilerParams(dimension_semantics=("arbitrary",)),
        name="moe_plan",
    )(eid_row, jnp.broadcast_to(counts[:, None], (LANE, LANE)))


def _dest_tiles(dest_t, tm):
    n = dest_t.shape[1]
    return dest_t.reshape(2, n // tm, tm).transpose(1, 0, 2).reshape(n // tm, 2 * tm)


def _dispatch_kernel(pend_ref, nused_ref, dest_hbm, *rest, tm, starts, n_blk):
    n_src = len(starts) - 1
    hf_refs = rest[:n_src]
    xs_out, idx_smem, sem_idx, sem_rows, zbuf, sem_zero = rest[n_src:]
    i = pl.program_id(0)
    n_tiles = 2 * pl.num_programs(0)

    def idx_copy(k, sl):
        return pltpu.make_async_copy(dest_hbm.at[k], idx_smem.at[sl], sem_idx.at[sl])

    @pl.when(i == 0)
    def _():
        idx_copy(0, 0).start()
        zbuf[...] = jnp.zeros(zbuf.shape, zbuf.dtype)

        def zero_block(row0):
            return pltpu.make_async_copy(zbuf, xs_out.at[pl.ds(pl.multiple_of(row0, MOE_BLK), MOE_BLK)], sem_zero)

        def nonempty(e):
            return pend_ref[e] > jnp.where(e == 0, 0, pend_ref[jnp.maximum(e - 1, 0)])

        def visit(fn):
            def expert(e, carry):
                @pl.when(nonempty(e))
                def _():
                    fn(zero_block(pend_ref[e] - MOE_BLK))
                return carry

            def unused(b, carry):
                fn(zero_block(b * MOE_BLK))
                return carry

            lax.fori_loop(0, N_EXPERTS, expert, 0)
            lax.fori_loop(nused_ref[0], n_blk, unused, 0)

        visit(lambda cp: cp.start())
        visit(lambda cp: cp.wait())

    for g in range(n_src):
        hf_ref = hf_refs[g]

        @pl.when(jnp.logical_and(i >= starts[g], i < starts[g + 1]))
        def _():
            for par in range(2):
                t = 2 * i + par
                idx_copy(t, par).wait()

                @pl.when(t + 1 < n_tiles)
                def _():
                    idx_copy(t + 1, 1 - par).start()

                for r in range(tm):
                    for k in range(2):
                        row = idx_smem[par, k * tm + r]
                        pltpu.make_async_copy(hf_ref.at[pl.ds(par * tm + r, 1)], xs_out.at[pl.ds(row, 1)],
                                              sem_rows).start(priority=k)
            for _ in range(2):
                pltpu.make_async_copy(hf_ref, xs_out.at[pl.ds(0, 2 * tm)], sem_rows).wait()


def _dispatch(hfs, dests, pends, nused, n_blk, tm):
    w = hfs[0].shape[1]
    assert all(hf.shape[0] % (2 * tm) == 0 for hf in hfs)
    tiles = [hf.shape[0] // (2 * tm) for hf in hfs]
    starts = [sum(tiles[:g]) for g in range(len(hfs) + 1)]

    def src_spec(g):
        return pl.BlockSpec((2 * tm, w), lambda i, pe, nu: (jnp.clip(i - starts[g], 0, tiles[g] - 1), 0))

    return pl.pallas_call(
        functools.partial(_dispatch_kernel, tm=tm, starts=tuple(starts), n_blk=n_blk),
        out_shape=jax.ShapeDtypeStruct((n_blk * MOE_BLK, w), hfs[0].dtype),
        grid_spec=pltpu.PrefetchScalarGridSpec(
            num_scalar_prefetch=2,
            grid=(starts[-1],),
            in_specs=[pl.BlockSpec(memory_space=pl.ANY)] + [src_spec(g) for g in range(len(hfs))],
            out_specs=pl.BlockSpec(memory_space=pl.ANY),
            scratch_shapes=[
                pltpu.SMEM((2, 2 * tm), I32),
                pltpu.SemaphoreType.DMA((2,)),
                pltpu.SemaphoreType.DMA,
                pltpu.VMEM((MOE_BLK, w), hfs[0].dtype),
                pltpu.SemaphoreType.DMA,
            ],
        ),
        compiler_params=pltpu.CompilerParams(
            dimension_semantics=("arbitrary",), vmem_limit_bytes=VMEM_LIMIT),
        name="moe_dispatch",
    )(pends, nused, jnp.concatenate([_dest_tiles(d, tm) for d in dests], axis=0), *hfs)


def _experts_kernel(bexp_ref, nused_ref, xs_ref, w1_ref, w3_ref, w2_ref, out_ref, w13, w2s, *, d_exp):
    i = pl.program_id(0)
    nused = nused_ref[0]

    @pl.when(i < nused)
    def _():
        changed = jnp.logical_or(i == 0, bexp_ref[i] != bexp_ref[jnp.maximum(i - 1, 0)])

        @pl.when(changed)
        def _():
            w13[:, :d_exp] = w1_ref[0, 0].astype(BF16)
            w13[:, d_exp:] = w3_ref[0, 0].astype(BF16)
            w2s[...] = w2_ref[0, 0].astype(BF16)

        hcat = jnp.dot(_unpack_bf16_pairs(xs_ref[...]), w13[...], preferred_element_type=F32)
        a = hcat[:, :d_exp]
        act = (a * jax.nn.sigmoid(a)) * hcat[:, d_exp:]
        out_ref[...] = jnp.dot(act.astype(BF16), w2s[...], preferred_element_type=F32)

    @pl.when(i >= nused)
    def _():
        out_ref[...] = jnp.zeros(out_ref.shape, F32)


def _experts(xs, blk_exp, nused, l, w1, w3, w2):
    p_rows, dw = xs.shape
    d, d_exp = w1.shape[-2:]
    assert dw * 2 == d
    n_blk = p_rows // MOE_BLK
    wspec = lambda r, c: pl.BlockSpec((1, 1, r, c), lambda i, be, nu: (l, be[jnp.minimum(i, nu[0] - 1)], 0, 0))
    return pl.pallas_call(
        functools.partial(_experts_kernel, d_exp=d_exp),
        out_shape=jax.ShapeDtypeStruct((p_rows, d), F32),
        grid_spec=pltpu.PrefetchScalarGridSpec(
            num_scalar_prefetch=2,
            grid=(n_blk,),
            in_specs=[
                pl.BlockSpec((MOE_BLK, dw), lambda i, be, nu: (jnp.minimum(i, nu[0] - 1), 0)),
                wspec(d, d_exp), wspec(d, d_exp), wspec(d_exp, d),
            ],
            out_specs=pl.BlockSpec((MOE_BLK, d), lambda i, be, nu: (i, 0)),
            scratch_shapes=[
                pltpu.VMEM((d, 2 * d_exp), BF16),
                pltpu.VMEM((d_exp, d), BF16),
            ],
        ),
        compiler_params=pltpu.CompilerParams(
            dimension_semantics=("arbitrary",), vmem_limit_bytes=VMEM_LIMIT),
        name="moe_experts",
    )(blk_exp, nused, xs, w1, w3, w2)


def _moe_layer(l, groups, P):
    eid_all = jnp.concatenate([g[1] for g in groups], axis=1)
    n_all = eid_all.shape[1]
    a = 2 * n_all
    assert a % MOE_BLK == 0
    n_blk = a // MOE_BLK + N_EXPERTS
    counts_f = sum(g[2][0] for g in groups)
    dest_row = _plan(eid_all.reshape(1, a), counts_f)
    counts = counts_f[:N_EXPERTS].astype(I32)
    pends = jnp.cumsum((counts + MOE_BLK - 1) // MOE_BLK * MOE_BLK)
    blk_start = jnp.arange(n_blk, dtype=I32) * MOE_BLK
    blk_exp = jnp.minimum(jnp.sum((pends[None, :] <= blk_start[:, None]).astype(I32), axis=1), N_EXPERTS - 1)
    nused = (pends[-1] // MOE_BLK).astype(I32).reshape(1)
    dest = dest_row.reshape(2, n_all)
    dests = []
    off = 0
    for hf, _, _ in groups:
        dests.append(dest[:, off:off + hf.shape[0]])
        off += hf.shape[0]
    xs = _dispatch([g[0] for g in groups], dests, pends.astype(I32), nused, n_blk, DISPATCH_TILE)
    yr = _experts(xs, blk_exp.astype(I32), nused, l, P["w1"], P["w3"], P["w2"])
    return yr, dests


def _gather_begin(step, slot, nsteps, idx_hbm, src_hbm, idx_smem, buf, sem_idx, sem_rows, nrows, n_chunks):
    nslot = 1 - slot
    last = nsteps - 1
    nxt = jnp.minimum(step + 1, last)
    nxt2 = jnp.minimum(step + 2, last)
    per = nrows // n_chunks

    def idx_copy(k, sl):
        return pltpu.make_async_copy(idx_hbm.at[k], idx_smem.at[sl], sem_idx.at[sl])

    def row_copy(sl, r):
        return pltpu.make_async_copy(src_hbm.at[pl.ds(idx_smem[sl, r], 1)], buf.at[sl, pl.ds(r, 1)],
                                     sem_rows.at[sl])

    def rows_wait(sl):
        pltpu.make_async_copy(src_hbm.at[pl.ds(0, nrows)], buf.at[sl], sem_rows.at[sl]).wait()

    @pl.when(step == 0)
    def _():
        first = idx_copy(0, 0)
        first.start()
        first.wait()

        def body(g, carry):
            for u in range(GATHER_UNROLL):
                row_copy(0, g * GATHER_UNROLL + u).start()
            return carry
        lax.fori_loop(0, nrows // GATHER_UNROLL, body, 0)
        idx_copy(nxt, 1).start()

    idx_copy(nxt, nslot).wait()
    rows_wait(slot)
    idx_copy(nxt2, slot).start()

    def issue_chunk(c):
        for r in range(c * per, (c + 1) * per):
            row_copy(nslot, r).start(priority=r % 2)

    def finish():
        @pl.when(step == last)
        def _():
            rows_wait(nslot)
            idx_copy(nxt2, slot).wait()

    return issue_chunk, finish


def _combine(slot, x1_ref, gate_ref, ybuf, tm):
    g = gate_ref[...]
    y = g[:, 0:1] * ybuf[slot, 0:tm, :] + g[:, 1:2] * ybuf[slot, tm:2 * tm, :]
    return x1_ref[...] + y


def _mid_kernel(dest_hbm, yr_hbm, x1_ref, gate_ref, cos_ref, sin_ref,
                gkv_ref, wkvt_ref, gmix_ref, wqt_ref, wqm_ref,
                x2_ref, kt_ref, vt_ref, qt_ref, qm_ref,
                idx_smem, ybuf, sem_idx, sem_rows, *, tm, d_att):
    n_heads = d_att // HEAD_DIM
    n_chunks = 8
    for par in range(2):
        tok = slice(par * tm, (par + 1) * tm)
        issue_chunk, finish = _gather_begin(2 * pl.program_id(0) + par, par, 2 * pl.num_programs(0),
                                            dest_hbm, yr_hbm, idx_smem, ybuf, sem_idx, sem_rows,
                                            2 * tm, n_chunks)
        chunks = iter(range(n_chunks))
        x2 = _combine(par, x1_ref.at[tok], gate_ref.at[tok], ybuf, tm)
        x2_ref[tok, :] = x2
        issue_chunk(next(chunks))
        cos_t = cos_ref[:, tok]
        sin_t = sin_ref[:, tok]
        hk = _rms(x2, gkv_ref[...]).astype(BF16)
        issue_chunk(next(chunks))
        vt_ref[0, :, tok] = lax.dot_general(wkvt_ref[d_att:, :], hk, NT_DIMS, preferred_element_type=F32)
        issue_chunk(next(chunks))
        kt_ref[0, :, tok] = lax.dot_general(wkvt_ref[:d_att, :], hk, NT_DIMS, preferred_element_type=F32)
        issue_chunk(next(chunks))
        _rot_t(kt_ref, tok, cos_t, sin_t, n_heads)
        hq = _rms(x2, gmix_ref[...]).astype(BF16)
        issue_chunk(next(chunks))
        qt_ref[0, :, tok] = lax.dot_general(wqt_ref[...], hq, NT_DIMS, preferred_element_type=F32)
        issue_chunk(next(chunks))
        _rot_t(qt_ref, tok, cos_t, sin_t, n_heads)
        issue_chunk(next(chunks))
        qm_ref[tok, :] = jnp.dot(hq, wqm_ref[...], preferred_element_type=F32)
        for c in chunks:
            issue_chunk(c)
        finish()


def _mid(x1, gate, yr, dest, pos, n_seq, P, tm):
    n, d = x1.shape
    d_att = P["w_kv_t_bf"].shape[0] // 2
    tg = 2 * tm
    s_out = n // n_seq
    assert s_out % tg == 0
    nt = s_out // tg
    half = ROT_DIM // 2
    reps = s_out // pos.shape[0]
    inv = ROPE_THETA ** (-jnp.arange(half, dtype=F32) * (2.0 / ROT_DIM))
    ang_t = jnp.tile(inv[:, None] * pos.astype(F32)[None, :], (1, reps))
    const = lambda *shape: pl.BlockSpec(shape, lambda i: (0,) * len(shape))
    tok = lambda w: pl.BlockSpec((tg, w), lambda i: (i, 0))
    tbl_t = pl.BlockSpec((half, tg), lambda i: (0, i % nt))
    feat_t = pl.BlockSpec((1, d_att, tg), lambda i: (i // nt, 0, i % nt))
    feat_shape = jax.ShapeDtypeStruct((n_seq, d_att, s_out), F32)
    return pl.pallas_call(
        functools.partial(_mid_kernel, tm=tm, d_att=d_att),
        out_shape=(jax.ShapeDtypeStruct((n, d), F32), feat_shape, feat_shape, feat_shape,
                   jax.ShapeDtypeStruct((n, D_MEMQ), F32)),
        grid=(n // tg,),
        in_specs=[
            pl.BlockSpec(memory_space=pl.ANY), pl.BlockSpec(memory_space=pl.ANY),
            tok(d), tok(2), tbl_t, tbl_t,
            const(1, d), const(2 * d_att, d), const(1, d), const(d_att, d), const(d, D_MEMQ),
        ],
        out_specs=(tok(d), feat_t, feat_t, feat_t, tok(D_MEMQ)),
        scratch_shapes=[
            pltpu.SMEM((2, 2 * tm), I32),
            pltpu.VMEM((2, 2 * tm, d), F32),
            pltpu.SemaphoreType.DMA((2,)),
            pltpu.SemaphoreType.DMA((2,)),
        ],
        compiler_params=pltpu.CompilerParams(
            dimension_semantics=("arbitrary",), vmem_limit_bytes=VMEM_LIMIT),
        name="mid",
    )(_dest_tiles(dest, tm), yr, x1, gate, jnp.cos(ang_t), jnp.sin(ang_t),
      P["g_kv"].reshape(1, d), P["w_kv_t_bf"], P["g_mix"][1].reshape(1, d), P["w_q_t_bf"], P["w_qm_bf"])


def _final_kernel(dest_hbm, yr_hbm, x1_ref, gate_ref, g_ref, y_ref, idx_smem, ybuf, sem_idx, sem_rows, *, tm):
    for par in range(2):
        tok = slice(par * tm, (par + 1) * tm)
        issue_chunk, finish = _gather_begin(2 * pl.program_id(0) + par, par, 2 * pl.num_programs(0),
                                            dest_hbm, yr_hbm, idx_smem, ybuf, sem_idx, sem_rows, 2 * tm, 1)
        issue_chunk(0)
        y_ref[tok, :] = _rms(_combine(par, x1_ref.at[tok], gate_ref.at[tok], ybuf, tm), g_ref[...])
        finish()


def _final(x1, gate, yr, dest, g_final, tm):
    n, d = x1.shape
    assert n % (2 * tm) == 0
    tok = lambda w: pl.BlockSpec((2 * tm, w), lambda i: (i, 0))
    return pl.pallas_call(
        functools.partial(_final_kernel, tm=tm),
        out_shape=jax.ShapeDtypeStruct((n, d), F32),
        grid=(n // (2 * tm),),
        in_specs=[
            pl.BlockSpec(memory_space=pl.ANY), pl.BlockSpec(memory_space=pl.ANY),
            tok(d), tok(2), pl.BlockSpec((1, d), lambda i: (0, 0)),
        ],
        out_specs=tok(d),
        scratch_shapes=[
            pltpu.SMEM((2, 2 * tm), I32),
            pltpu.VMEM((2, 2 * tm, d), F32),
            pltpu.SemaphoreType.DMA((2,)),
            pltpu.SemaphoreType.DMA((2,)),
        ],
        compiler_params=pltpu.CompilerParams(
            dimension_semantics=("arbitrary",), vmem_limit_bytes=VMEM_LIMIT),
        name="final",
    )(_dest_tiles(dest, tm), yr, x1, gate, g_final.reshape(1, d))


def _moba_prompt_kernel(qt_ref, kt_ref, vt_ref, o_ref, kmean, krows, vones, sel_sc, s_buf, p_buf, acc,
                        *, s_len, n_grp):
    qb_i = pl.program_id(2)
    nb = s_len // MOBA_BLOCK
    n_h = LANE // HEAD_DIM
    tq = MOBA_BLOCK
    nq = n_h * tq
    scale = HEAD_DIM ** -0.5
    feat = lax.broadcasted_iota(I32, (LANE, MOBA_BLOCK), 0) // HEAD_DIM
    groups = range(n_grp)

    @pl.when(qb_i == 0)
    def _():
        kmean[...] = jnp.zeros(kmean.shape, F32)
        for g in groups:
            rows = slice(g * LANE, (g + 1) * LANE)
            for jb in range(nb):
                cols = slice(jb * MOBA_BLOCK, (jb + 1) * MOBA_BLOCK)
                kr = kt_ref[0, rows, cols].T
                kmean[g, jb:jb + 1, :] = jnp.sum(kr, axis=0, keepdims=True) * (1.0 / MOBA_BLOCK)
                krows[g, jb] = kr.astype(BF16)
                vblk = vt_ref[0, rows, cols]
                for hh in range(n_h):
                    vones[g, hh, jb] = jnp.where(feat == hh, vblk, 1.0).astype(BF16)

    key = lax.broadcasted_iota(I32, (MOBA_BLOCK, nq), 0)
    qcol = lax.broadcasted_iota(I32, (MOBA_BLOCK, nq), 1) % tq
    qs, m0, p_own = [], [], []
    for g in groups:
        qt = qt_ref[0, g * LANE:(g + 1) * LANE, :]
        qstack = jnp.concatenate([jnp.where(feat == hh, qt, 0.0) for hh in range(n_h)], axis=1)
        gate = jnp.dot(kmean[g], qstack, preferred_element_type=F32, precision=HIGHEST)
        brow = lax.broadcasted_iota(I32, gate.shape, 0)
        gate = jnp.where(brow < qb_i, gate, NEG_INF)
        sel = _top_k_mask(gate, brow, 0)
        for jb in range(nb):
            sel_sc[g, jb] = sel[jb:jb + 1, :]
        qs.append((qstack * (scale * LOG2_E)).astype(BF16))
        s_own = jnp.where(key <= qcol, jnp.dot(krows[g, qb_i], qs[g], preferred_element_type=F32), NEG_INF)
        m0.append(jnp.max(s_own, axis=0, keepdims=True))
        p_own.append(jnp.exp2(s_own - m0[g]).astype(BF16))

    def put_scores(g, j, slot):
        s_buf[g, slot] = jnp.dot(krows[g, j], qs[g], preferred_element_type=F32)

    def add_block(g, alpha, j, slot):
        for hh in range(n_h):
            acc[g, hh] = (alpha[:, hh * tq:(hh + 1) * tq] * acc[g, hh]
                          + jnp.dot(vones[g, hh, j], p_buf[g, slot, :, hh * tq:(hh + 1) * tq],
                                    preferred_element_type=F32))

    for g in groups:
        acc[g] = jnp.zeros((n_h, LANE, tq), F32)
        p_buf[g, 1] = p_own[g]
        put_scores(g, 0, 0)

    def half_trip(j, slot, carry):
        out = []
        for g in groups:
            m, alpha_prev = carry[g]
            add_block(g, alpha_prev, jnp.where(j == 0, qb_i, j - 1), 1 - slot)
            put_scores(g, jnp.minimum(j + 1, nb - 1), 1 - slot)
            chosen = sel_sc[g, j] > 0.0
            m_new = jnp.maximum(m, jnp.where(chosen, jnp.max(s_buf[g, slot], axis=0, keepdims=True), NEG_INF))
            p_buf[g, slot] = jnp.exp2(s_buf[g, slot] - jnp.where(chosen, m_new, jnp.inf)).astype(BF16)
            out.append((m_new, jnp.exp2(m - m_new)))
        return tuple(out)

    def body(t, carry):
        return half_trip(2 * t + 1, 1, half_trip(2 * t, 0, carry))

    n_trips = (qb_i + 1) // 2
    one = jnp.ones((1, nq), F32)
    state = lax.fori_loop(0, n_trips, body, tuple((m0[g], one) for g in groups))
    last = jnp.where(n_trips == 0, qb_i, 2 * n_trips - 1)
    for g in groups:
        add_block(g, state[g][1], last, 1)
        out_t = jnp.zeros((LANE, tq), F32)
        for hh in range(n_h):
            other = (1 - hh) * HEAD_DIM
            a_h = acc[g, hh]
            out_t = jnp.where(feat == hh, a_h / a_h[other:other + 1, :], out_t)
        o_ref[:, g * LANE:(g + 1) * LANE] = out_t.T


def _moba_prompt(qt, kt, vt, b, s, n_grp=2):
    d_att = qt.shape[1]
    nb = s // MOBA_BLOCK
    fw = n_grp * LANE
    assert s % MOBA_BLOCK == 0 and LANE // HEAD_DIM == 2 and d_att % fw == 0
    nb_pad = -(-nb // SUBLANE) * SUBLANE
    kv_spec = pl.BlockSpec((1, fw, s), lambda i, hp, qb: (i, hp, 0))
    return pl.pallas_call(
        functools.partial(_moba_prompt_kernel, s_len=s, n_grp=n_grp),
        out_shape=jax.ShapeDtypeStruct((b * s, d_att), F32),
        grid=(b, d_att // fw, nb),
        in_specs=[
            pl.BlockSpec((1, fw, MOBA_BLOCK), lambda i, hp, qb: (i, hp, qb)),
            kv_spec, kv_spec,
        ],
        out_specs=pl.BlockSpec((MOBA_BLOCK, fw), lambda i, hp, qb: (i * nb + qb, hp)),
        scratch_shapes=[
            pltpu.VMEM((n_grp, nb_pad, LANE), F32),
            pltpu.VMEM((n_grp, nb, MOBA_BLOCK, LANE), BF16),
            pltpu.VMEM((n_grp, 2, nb, LANE, MOBA_BLOCK), BF16),
            pltpu.VMEM((n_grp, nb, 1, 2 * MOBA_BLOCK), F32),
            pltpu.VMEM((n_grp, 2, MOBA_BLOCK, 2 * MOBA_BLOCK), F32),
            pltpu.VMEM((n_grp, 2, MOBA_BLOCK, 2 * MOBA_BLOCK), BF16),
            pltpu.VMEM((n_grp, 2, LANE, MOBA_BLOCK), F32),
        ],
        compiler_params=pltpu.CompilerParams(
            dimension_semantics=("arbitrary", "arbitrary", "arbitrary"), vmem_limit_bytes=VMEM_LIMIT),
        name="moba_prompt",
    )(qt, kt, vt)


def _head_diag(o, n_heads, ds):
    return jnp.concatenate(
        [o[h * ds:(h + 1) * ds, h * HEAD_DIM:(h + 1) * HEAD_DIM] for h in range(n_heads)], axis=0)


def _moba_sample_kernel(pt_ref, q_ref, kn_ref, vn_ref, *rest, ds, n_heads, nb, bps):
    pages = rest[:4 * bps]
    o_ref, qbd, g_all, m_all, l_all, o_all = rest[4 * bps:]
    jg = pl.program_id(1)
    rows = n_heads * ds
    d_att = n_heads * HEAD_DIM
    scale = HEAD_DIM ** -0.5
    blane = lax.broadcasted_iota(I32, (rows, LANE), 1)

    @pl.when(jg == 0)
    def _():
        q = q_ref[...]
        lane = lax.broadcasted_iota(I32, (ds, d_att), 1)
        for h in range(n_heads):
            qbd[h * ds:(h + 1) * ds, :] = jnp.where(lane // HEAD_DIM == h, q, 0.0)
        g_all[...] = jnp.full((rows, LANE), NEG_INF, F32)
        m_all[...] = jnp.zeros((rows, LANE), F32)
        l_all[...] = jnp.zeros((rows, LANE), F32)

    qb = qbd[...].astype(BF16)
    g_new, m_new, l_new = g_all[...], m_all[...], l_all[...]
    for bi in range(bps):
        k0_ref, k1_ref = pages[2 * bi], pages[2 * bi + 1]
        v0_ref, v1_ref = pages[2 * bps + 2 * bi], pages[2 * bps + 2 * bi + 1]
        j = jg * bps + bi
        kt = jnp.concatenate([k0_ref[0].astype(BF16), k1_ref[0].astype(BF16)], axis=1)
        vt = jnp.concatenate([v0_ref[0].astype(BF16), v1_ref[0].astype(BF16)], axis=1)
        s = jnp.dot(qb, kt, preferred_element_type=F32)
        g = jnp.sum(s, axis=-1, keepdims=True) * (1.0 / MOBA_BLOCK)
        m = jnp.max(s, axis=-1, keepdims=True) * scale
        p = jnp.exp(s * scale - m)
        l = jnp.sum(p, axis=-1, keepdims=True)
        o = lax.dot_general(p.astype(BF16), vt, NT_DIMS, preferred_element_type=F32)
        o_all[j] = _head_diag(o, n_heads, ds)
        here = blane == j
        g_new = jnp.where(here, g, g_new)
        m_new = jnp.where(here, m, m_new)
        l_new = jnp.where(here, l, l_new)
    g_all[...] = g_new
    m_all[...] = m_new
    l_all[...] = l_new

    @pl.when(jg == nb // bps - 1)
    def _():
        chosen = _top_k_mask(g_new, blane, -1) > 0.0
        s_own = lax.dot_general(qb, kn_ref[...].astype(BF16), NT_DIMS, preferred_element_type=F32) * scale
        r_i = lax.broadcasted_iota(I32, (rows, ds), 0) % ds
        c_i = lax.broadcasted_iota(I32, (rows, ds), 1)
        s_own = jnp.where(c_i <= r_i, s_own, NEG_INF)
        m_sel = jnp.max(jnp.where(chosen, m_new, NEG_INF), axis=-1, keepdims=True)
        m_tot = jnp.maximum(m_sel, jnp.max(s_own, axis=-1, keepdims=True))
        wgt = jnp.where(chosen, jnp.exp(m_new - m_tot), 0.0)
        p_own = jnp.exp(s_own - m_tot)
        l_tot = jnp.sum(wgt * l_new, axis=-1, keepdims=True) + jnp.sum(p_own, axis=-1, keepdims=True)
        acc = _head_diag(jnp.dot(p_own.astype(BF16), vn_ref[...].astype(BF16), preferred_element_type=F32),
                         n_heads, ds)
        for jj in range(nb):
            acc = acc + wgt[:, jj:jj + 1] * o_all[jj]
        o_ref[0] = acc / l_tot


def _moba_sample(qb, k_new, v_new, cache_k, cache_v, page_table, db, ds, bps=8):
    n_pool, page, n_heads, hd = cache_k.shape
    d_att = n_heads * hd
    n_pages = page_table.shape[1]
    ppb = MOBA_BLOCK // page
    nb = n_pages // ppb
    rows = n_heads * ds
    assert ppb == 2 and n_pages % ppb == 0 and ds <= MOBA_BLOCK and nb <= LANE and nb % bps == 0
    ck = cache_k.transpose(0, 2, 3, 1).reshape(n_pool, d_att, page)
    cv = cache_v.transpose(0, 2, 3, 1).reshape(n_pool, d_att, page)
    seq = pl.BlockSpec((ds, d_att), lambda i, j, pt: (i, 0))
    ppg = ppb * bps
    pg = lambda off: pl.BlockSpec((1, d_att, page), lambda i, j, pt: (pt[i * n_pages + ppg * j + off], 0, 0))
    out = pl.pallas_call(
        functools.partial(_moba_sample_kernel, ds=ds, n_heads=n_heads, nb=nb, bps=bps),
        out_shape=jax.ShapeDtypeStruct((db, rows, hd), F32),
        grid_spec=pltpu.PrefetchScalarGridSpec(
            num_scalar_prefetch=1,
            grid=(db, nb // bps),
            in_specs=[seq, seq, seq] + [pg(o) for o in range(ppg)] * 2,
            out_specs=pl.BlockSpec((1, rows, hd), lambda i, j, pt: (i, 0, 0)),
            scratch_shapes=[
                pltpu.VMEM((rows, d_att), F32),
                pltpu.VMEM((rows, LANE), F32),
                pltpu.VMEM((rows, LANE), F32),
                pltpu.VMEM((rows, LANE), F32),
                pltpu.VMEM((nb, rows, hd), F32),
            ],
        ),
        compiler_params=pltpu.CompilerParams(
            dimension_semantics=("arbitrary", "arbitrary"), vmem_limit_bytes=VMEM_LIMIT),
        name="moba_sample",
    )(page_table.reshape(-1), qb, k_new, v_new, *([ck] * ppg), *([cv] * ppg))
    return out.reshape(db, n_heads, ds, hd).transpose(0, 2, 1, 3).reshape(db * ds, d_att)


def kernel(x_prompt, x_sample, state_conv, cache_k, cache_v, cache_mem_k, cache_mem_v, page_table,
           mem_prompt, g_mix, g_ffn, g_final, g_mem, w_mem_kv, w_in_a, conv_w, conv_b, cln_g, cln_b,
           w_out_a, g_kv, w_kv, w_in_b, w_out_b, w_rg, b_rg, w_re, b_re, w1, w3, w2):
    depth = g_mix.shape[0]
    n_heads = w_kv.shape[1] // (2 * HEAD_DIM)
    d_att = n_heads * HEAD_DIM
    routers = [_router_weights(w_rg[l], b_rg[l], w_re[l], b_re[l]) for l in range(depth)]
    P = {
        "g_mix": g_mix, "g_ffn": g_ffn, "g_final": g_final, "g_kv": g_kv,
        "w_in_a_bf": w_in_a[0].astype(BF16), "w_out_a_bf": w_out_a[0].astype(BF16),
        "w_kv_t_bf": w_kv.T.astype(BF16),
        "w_q_t_bf": w_in_b[0][:, :d_att].T.astype(BF16), "w_qm_bf": w_in_b[0][:, d_att:].astype(BF16),
        "w_out_b_bf": w_out_b[0].astype(BF16),
        "conv_w": conv_w, "conv_b": conv_b, "cln_g": cln_g, "cln_b": cln_b,
        "wr": [r[0] for r in routers], "br": [r[1] for r in routers],
        "w1": w1, "w3": w3, "w2": w2,
    }
    b_p, s_p, d = x_prompt.shape
    db, ds, _ = x_sample.shape
    n_p, n_s = b_p * s_p, db * ds
    mem_len = mem_prompt.shape[1]
    past = page_table.shape[1] * cache_k.shape[1]
    assert past % MOBA_BLOCK == 0
    ts_p, tm_p = 256, 256
    ts_s, tm_s = ds, n_s // 2
    nb_s = math.gcd(db, 8)

    memkv_t = _memkv(mem_prompt, g_mem, w_mem_kv.transpose(0, 2, 1).astype(BF16))
    kv6 = memkv_t.reshape(depth, b_p, 2, MEM_H, HEAD_DIM, mem_len).transpose(2, 0, 1, 5, 3, 4)
    mem_k_p, mem_v_p = kv6[0], kv6[1]
    cmk = cache_mem_k.transpose(0, 1, 3, 4, 2).reshape(depth, db, D_MEMQ, mem_len)
    cmv = cache_mem_v.transpose(0, 1, 3, 4, 2).reshape(depth, db, D_MEMQ, mem_len)
    mem_p = (memkv_t, memkv_t, (0, 1))
    mem_s = (cmk, cmv, (0, 0))

    ctx_p = jnp.zeros((b_p, CONV_W - 1, state_conv.shape[-1]), F32)
    x1_p, hf_p, gate_p, eid_p, cnt_p, conv_p = _layer0(x_prompt.reshape(n_p, d), b_p, s_p, ctx_p, *mem_p, 0, P, ts_p)
    x1_s, hf_s, gate_s, eid_s, cnt_s, conv_s = _layer0(x_sample.reshape(n_s, d), db, ds, state_conv[0], *mem_s, 0, P, ts_s, nb_s)
    yr, (dest_p, dest_s) = _moe_layer(0, [(hf_p, eid_p, cnt_p), (hf_s, eid_s.T, cnt_s)], P)

    x2_p, kt_p, vt_p, qt_p, qm_p = _mid(x1_p, gate_p, yr, dest_p, jnp.arange(s_p), b_p, P, tm_p)
    x2_s, kt_s, vt_s, qt_s, qm_s = _mid(x1_s, gate_s, yr, dest_s, past + jnp.arange(ds), 1, P, tm_s)
    c1_p = _moba_prompt(qt_p, kt_p, vt_p, b_p, s_p)
    k_new, v_new = kt_s[0].T, vt_s[0].T
    c1_s = _moba_sample(qt_s[0].T, k_new, v_new, cache_k, cache_v, page_table, db, ds)
    x3_p, hf_p, gate_p, eid_p, cnt_p = _layer1(x2_p, c1_p, qm_p, b_p, s_p, *mem_p, 1, P, ts_p)
    x3_s, hf_s, gate_s, eid_s, cnt_s = _layer1(x2_s, c1_s, qm_s, db, ds, *mem_s, 1, P, ts_s, nb_s)
    yr, (dest_p, dest_s) = _moe_layer(1, [(hf_p, eid_p, cnt_p), (hf_s, eid_s.T, cnt_s)], P)
    y_p = _final(x3_p, gate_p, yr, dest_p, g_final, tm_p).reshape(b_p, s_p, d)
    y_s = _final(x3_s, gate_s, yr, dest_s, g_final, tm_s).reshape(db, ds, d)

    k_p = kt_p.reshape(b_p, n_heads, HEAD_DIM, s_p).transpose(0, 3, 1, 2)
    v_p = vt_p.reshape(b_p, n_heads, HEAD_DIM, s_p).transpose(0, 3, 1, 2)
    k_s = k_new.reshape(db, ds, n_heads, HEAD_DIM)
    v_s = v_new.reshape(db, ds, n_heads, HEAD_DIM)
    return (y_p, y_s, conv_p[None], conv_s[None], k_p, v_p, k_s, v_s, mem_k_p, mem_v_p)
```
